```python
import jax, jax.numpy as jnp
from jax import lax
import numpy as np

D_MODEL = 1024
BATCH = 8
SEQ = 8192
DEPTH = 1

N_META = 16
Q_BLOCK = 128
ATT_HEADS = 8
ATT_KV_HEADS = 2
HEAD_DIM = 64
ATT_W = ATT_HEADS * HEAD_DIM
KV_W = ATT_KV_HEADS * HEAD_DIM
IDX_HEADS = 8
IDX_DIM = 64
TOPK_MAX = 256
ROPE_THETA = 10000.0
CONV_W = D_MODEL // 2
CONV_K = 3
D_FF = 256 * ((8 * D_MODEL // 3 + 255) // 256)
LN_EPS = 1e-5
DEEPNORM_ALPHA = (2.0 * DEPTH) ** 0.25
DEEPNORM_BETA = (8.0 * DEPTH) ** -0.25
PROJ_WIDTHS = (ATT_W, KV_W, KV_W, IDX_HEADS * IDX_DIM, IDX_DIM, IDX_HEADS,
               CONV_W, CONV_W, CONV_W, D_MODEL, D_MODEL)
PROJ_W = sum(PROJ_WIDTHS)

kernel_name = "hybrid_dsa_shortconv_macaron_deepnorm"


def layer_norm(x, g, b):
    xf = x.astype(jnp.float32)
    mu = jnp.mean(xf, axis=-1, keepdims=True)
    xc = xf - mu
    var = jnp.mean(xc * xc, axis=-1, keepdims=True)
    y = xc * lax.rsqrt(var + LN_EPS) * g.astype(jnp.float32) + b.astype(jnp.float32)
    return y.astype(x.dtype)


def swiglu(x, w_gate, w_up, w_down):
    return (jax.nn.silu(x @ w_gate) * (x @ w_up)) @ w_down


def rope(x, pos):
    half = x.shape[-1] // 2
    inv_freq = ROPE_THETA ** (-jnp.arange(half, dtype=jnp.float32) / half)
    ang = pos.astype(jnp.float32)[:, None] * inv_freq[None, :]
    cos = jnp.cos(ang)[:, None, :]
    sin = jnp.sin(ang)[:, None, :]
    x1 = x[..., :half].astype(jnp.float32)
    x2 = x[..., half:].astype(jnp.float32)
    return jnp.concatenate([x1 * cos - x2 * sin, x2 * cos + x1 * sin], axis=-1).astype(x.dtype)


def dsa_attend(qi, wi, q, qpos, ki, k, v, topk):
    b_, t_ = q.shape[0], q.shape[1]
    kpos = jnp.arange(ki.shape[1], dtype=jnp.int32)
    s = jax.nn.relu(jnp.einsum('bthd,bsd->bths', qi, ki))
    score = jnp.einsum('bths,bth->bts', s, wi).astype(jnp.float32)
    causal = kpos[None, :] <= qpos[:, None]
    score = jnp.where(causal[None], score, -jnp.inf)
    _, idx = lax.top_k(score, topk)
    k_sel = jax.vmap(lambda kb, ib: kb[ib])(k, idx)
    v_sel = jax.vmap(lambda vb, ib: vb[ib])(v, idx)
    valid = idx <= qpos[None, :, None]
    qg = q.reshape(b_, t_, ATT_KV_HEADS, ATT_HEADS // ATT_KV_HEADS, HEAD_DIM)
    logits = jnp.einsum('btgrd,btngd->btgrn', qg, k_sel).astype(jnp.float32) * (HEAD_DIM ** -0.5)
    logits = jnp.where(valid[:, :, None, None, :], logits, -jnp.inf)
    p = jax.nn.softmax(logits, axis=-1).astype(v.dtype)
    o = jnp.einsum('btgrn,btngd->btgrd', p, v_sel)
    return o.reshape(b_, t_, ATT_W)


def short_conv(u, w):
    return lax.conv_general_dilated(
        u, w[:, None, :].astype(u.dtype), window_strides=(1,), padding=[(CONV_K - 1, 0)],
        dimension_numbers=('NWC', 'WIO', 'NWC'), feature_group_count=u.shape[-1])


def hybrid_mixer(h, w_in, conv_w, w_att_out, w_conv_out, w_o, pos, topk):
    b_, l_, _ = h.shape
    offs = np.cumsum(PROJ_WIDTHS)[:-1].tolist()
    proj = h @ w_in
    q, k, v, qi, ki, wi, u, gate_b, gate_c, g_att, g_conv = jnp.split(proj, offs, axis=-1)
    q = rope(q.reshape(b_, l_, ATT_HEADS, HEAD_DIM), pos)
    k = rope(k.reshape(b_, l_, ATT_KV_HEADS, HEAD_DIM), pos)
    v = v.reshape(b_, l_, ATT_KV_HEADS, HEAD_DIM)
    qi = rope(qi.reshape(b_, l_, IDX_HEADS, IDX_DIM), pos)
    ki = rope(ki[:, :, None, :], pos)[:, :, 0, :]
    wi = wi * ((IDX_HEADS * IDX_DIM) ** -0.5)

    att_meta = dsa_attend(qi[:, :N_META], wi[:, :N_META], q[:, :N_META], pos[:N_META], ki, k, v, topk)
    n_blk = (l_ - N_META) // Q_BLOCK

    def to_blocks(a):
        a = a[:, N_META:]
        return jnp.moveaxis(a.reshape((b_, n_blk, Q_BLOCK) + a.shape[2:]), 1, 0)

    qpos_blocks = pos[N_META:].reshape(n_blk, Q_BLOCK)
    att_real = lax.map(lambda args: dsa_attend(args[0], args[1], args[2], args[3], ki, k, v, topk),
                       (to_blocks(qi), to_blocks(wi), to_blocks(q), qpos_blocks))
    att_real = jnp.moveaxis(att_real, 0, 1).reshape(b_, l_ - N_META, ATT_W)
    att = jnp.concatenate([att_meta, att_real], axis=1)
    y_att = att @ w_att_out

    y_conv = (gate_b * short_conv(gate_c * u, conv_w)) @ w_conv_out

    merged = jax.nn.sigmoid(g_att) * y_att + jax.nn.sigmoid(g_conv) * y_conv
    return merged @ w_o


def setup_inputs(seed: int = 0) -> dict:
    key = jax.random.key(seed)
    ks = jax.random.split(key, 20)
    f32 = jnp.float32

    def nrm(k, shape, fan_in, scale=1.0):
        return jax.random.normal(k, shape, f32) * (scale * fan_in ** -0.5)

    def gain(k):
        return 1.0 + 0.02 * jax.random.normal(k, (DEPTH, D_MODEL), f32)

    def bias(k):
        return 0.02 * jax.random.normal(k, (DEPTH, D_MODEL), f32)

    beta = DEEPNORM_BETA
    return {
        "x": jax.random.normal(ks[0], (BATCH, SEQ, D_MODEL), f32),
        "meta_tokens": jax.random.normal(ks[1], (N_META, D_MODEL), f32),
        "ffn1_w_gate": nrm(ks[2], (DEPTH, D_MODEL, D_FF), D_MODEL),
        "ffn1_w_up": nrm(ks[3], (DEPTH, D_MODEL, D_FF), D_MODEL),
        "ffn1_w_down": nrm(ks[4], (DEPTH, D_FF, D_MODEL), D_FF, beta),
        "ln1_g": gain(ks[5]),
        "ln1_b": bias(ks[6]),
        "w_in": nrm(ks[7], (DEPTH, D_MODEL, PROJ_W), D_MODEL),
        "conv_w": nrm(ks[8], (DEPTH, CONV_K, CONV_W), CONV_K),
        "w_att_out": nrm(ks[9], (DEPTH, ATT_W, D_MODEL), ATT_W, beta),
        "w_conv_out": nrm(ks[10], (DEPTH, CONV_W, D_MODEL), CONV_W, beta),
        "w_o": nrm(ks[11], (DEPTH, D_MODEL, D_MODEL), D_MODEL, beta),
        "ln2_g": gain(ks[12]),
        "ln2_b": bias(ks[13]),
        "ffn2_w_gate": nrm(ks[14], (DEPTH, D_MODEL, D_FF), D_MODEL),
        "ffn2_w_up": nrm(ks[15], (DEPTH, D_MODEL, D_FF), D_MODEL),
        "ffn2_w_down": nrm(ks[16], (DEPTH, D_FF, D_MODEL), D_FF, beta),
        "ln3_g": gain(ks[17]),
        "ln3_b": bias(ks[18]),
    }


def reference(x, meta_tokens, ffn1_w_gate, ffn1_w_up, ffn1_w_down, ln1_g, ln1_b, w_in, conv_w,
              w_att_out, w_conv_out, w_o, ln2_g, ln2_b, ffn2_w_gate, ffn2_w_up, ffn2_w_down,
              ln3_g, ln3_b):
    b_ = x.shape[0]
    meta = jnp.broadcast_to(meta_tokens[None].astype(x.dtype), (b_, N_META, D_MODEL))
    h = jnp.concatenate([meta, x], axis=1)
    l_ = h.shape[1]
    pos = jnp.arange(l_, dtype=jnp.int32)
    topk = min(TOPK_MAX, l_ // 4)
    for l in range(DEPTH):
        h = layer_norm(DEEPNORM_ALPHA * h + 0.5 * swiglu(h, ffn1_w_gate[l], ffn1_w_up[l], ffn1_w_down[l]),
                       ln1_g[l], ln1_b[l])
        h = layer_norm(DEEPNORM_ALPHA * h + hybrid_mixer(h, w_in[l], conv_w[l], w_att_out[l], w_conv_out[l],
                                                         w_o[l], pos, topk),
                       ln2_g[l], ln2_b[l])
        h = layer_norm(DEEPNORM_ALPHA * h + 0.5 * swiglu(h, ffn2_w_gate[l], ffn2_w_up[l], ffn2_w_down[l]),
                       ln3_g[l], ln3_b[l])
    return h[:, N_META:]
```

```python
import functools

import numpy as np
import jax
import jax.numpy as jnp
from jax import lax
from jax.experimental import pallas as pl
from jax.experimental.pallas import tpu as pltpu

F32 = jnp.float32
BF16 = jnp.bfloat16
I32 = jnp.int32

D_MODEL = 1024
N_META = 16
ATT_HEADS = 8
ATT_KV_HEADS = 2
HEAD_DIM = 64
ATT_W = ATT_HEADS * HEAD_DIM
KV_W = ATT_KV_HEADS * HEAD_DIM
IDX_HEADS = 8
IDX_DIM = 64
TOPK_MAX = 256
ROPE_THETA = 10000.0
CONV_W = D_MODEL // 2
CONV_K = 3
D_FF = 2816
LN_EPS = 1e-5
DEPTH = 1
DEEPNORM_ALPHA = (2.0 * DEPTH) ** 0.25

LANES = 128
TQ = 128
CH = 256
INT_MIN = -(2 ** 31)
NEG_BIG = -1e30
VMEM_LIMIT = 52 * 1024 * 1024

_NT = (((1,), (1,)), ((), ()))


def _layer_norm(y, g, b):
    mu = jnp.mean(y, axis=-1, keepdims=True)
    yc = y - mu
    var = jnp.mean(yc * yc, axis=-1, keepdims=True)
    return yc * lax.rsqrt(var + LN_EPS) * g + b


def _ffn_ln_kernel(x_ref, wg_ref, wu_ref, wd_ref, g_ref, b_ref, o_ref, acc_ref, xb_ref):
    f = pl.program_id(1)

    @pl.when(f == 0)
    def _():
        acc_ref[...] = jnp.zeros_like(acc_ref)
        xb_ref[...] = x_ref[...].astype(BF16)

    xb = xb_ref[...]
    gt = jnp.dot(xb, wg_ref[...], preferred_element_type=F32)
    up = jnp.dot(xb, wu_ref[...], preferred_element_type=F32)
    a = (gt * jax.nn.sigmoid(gt)) * up
    acc_ref[...] += jnp.dot(a.astype(BF16), wd_ref[...], preferred_element_type=F32)

    @pl.when(f == pl.num_programs(1) - 1)
    def _():
        y = DEEPNORM_ALPHA * x_ref[...] + 0.5 * acc_ref[...]
        o_ref[...] = _layer_norm(y, g_ref[...], b_ref[...])


def _ffn_ln(x2d, wg, wu, wd, g, b):
    n = x2d.shape[0]
    tm = min(n, 1024)
    tf = 256
    assert n % tm == 0 and D_FF % tf == 0
    return pl.pallas_call(
        _ffn_ln_kernel,
        grid=(n // tm, D_FF // tf),
        in_specs=[
            pl.BlockSpec((tm, D_MODEL), lambda i, f: (i, 0)),
            pl.BlockSpec((D_MODEL, tf), lambda i, f: (0, f)),
            pl.BlockSpec((D_MODEL, tf), lambda i, f: (0, f)),
            pl.BlockSpec((tf, D_MODEL), lambda i, f: (f, 0)),
            pl.BlockSpec((1, D_MODEL), lambda i, f: (0, 0)),
            pl.BlockSpec((1, D_MODEL), lambda i, f: (0, 0)),
        ],
        out_specs=pl.BlockSpec((tm, D_MODEL), lambda i, f: (i, 0)),
        out_shape=jax.ShapeDtypeStruct((n, D_MODEL), F32),
        scratch_shapes=[pltpu.VMEM((tm, D_MODEL), F32), pltpu.VMEM((tm, D_MODEL), BF16)],
        compiler_params=pltpu.CompilerParams(
            dimension_semantics=("arbitrary", "arbitrary"), vmem_limit_bytes=VMEM_LIMIT),
        name="ffn_ln",
    )(x2d, wg, wu, wd, g, b)


_ROPE_SLABS = (ATT_W + IDX_HEADS * IDX_DIM + 2 * KV_W + 2 * IDX_DIM) // LANES
_PROJ_A_W = _ROPE_SLABS * LANES + 2 * KV_W + LANES


def _proj_rope_kernel(x_ref, w_ref, cos_ref, sin_ref, q_ref, qi_ref, kd_ref, ki_ref, vd_ref, wi_ref):
    tm = x_ref.shape[0]
    xb = x_ref[...].astype(BF16)
    cos = cos_ref[...]
    sin = sin_ref[...]
    lane = lax.broadcasted_iota(I32, (tm, LANES), 1)
    first_half = (lane & (HEAD_DIM // 2)) == 0

    def rope(s):
        partner = jnp.where(first_half, pltpu.roll(s, LANES - HEAD_DIM // 2, 1),
                            pltpu.roll(s, HEAD_DIM // 2, 1))
        return s * cos + partner * sin

    def dot_cols(c0, width):
        return jnp.dot(xb, w_ref[:, c0:c0 + width], preferred_element_type=F32)

    outs = ((q_ref, ATT_W, True, HEAD_DIM ** -0.5), (qi_ref, IDX_HEADS * IDX_DIM, True, 1.0),
            (kd_ref, 2 * KV_W, True, 1.0))
    c0 = 0
    for ref, width, _, scale in outs:
        for j in range(0, width, 2 * LANES):
            p = dot_cols(c0 + j, 2 * LANES)
            for s in range(2):
                r = rope(p[:, s * LANES:(s + 1) * LANES])
                if scale != 1.0:
                    r = r * scale
                ref[:, j + s * LANES:j + (s + 1) * LANES] = r.astype(ref.dtype)
        c0 += width
    p = dot_cols(c0, LANES)
    ki_ref[...] = rope(p).astype(ki_ref.dtype)
    c0 += LANES
    vd_ref[...] = dot_cols(c0, 2 * KV_W).astype(vd_ref.dtype)
    c0 += 2 * KV_W
    wi_ref[...] = dot_cols(c0, LANES) * ((IDX_HEADS * IDX_DIM) ** -0.5)


def _proj_rope(h2d, w_a, cos, sin, rows_per_seq):
    n = h2d.shape[0]
    tm = min(rows_per_seq, 512)
    tiles_per_seq = rows_per_seq // tm
    assert n % tm == 0 and rows_per_seq % tm == 0
    row = lambda i: (i, 0)
    tab = lambda i: (i % tiles_per_seq, 0)
    widths = (ATT_W, IDX_HEADS * IDX_DIM, 2 * KV_W, LANES, 2 * KV_W, LANES)
    dtypes = (BF16, BF16, BF16, BF16, BF16, F32)
    return pl.pallas_call(
        _proj_rope_kernel,
        grid=(n // tm,),
        in_specs=[
            pl.BlockSpec((tm, D_MODEL), row),
            pl.BlockSpec((D_MODEL, _PROJ_A_W), lambda i: (0, 0)),
            pl.BlockSpec((tm, LANES), tab),
            pl.BlockSpec((tm, LANES), tab),
        ],
        out_specs=[pl.BlockSpec((tm, w), row) for w in widths],
        out_shape=[jax.ShapeDtypeStruct((n, w), dt) for w, dt in zip(widths, dtypes)],
        compiler_params=pltpu.CompilerParams(
            dimension_semantics=("arbitrary",), vmem_limit_bytes=VMEM_LIMIT),
        name="proj_rope",
    )(h2d, w_a, cos, sin)


_PROJ_B_W = 3 * CONV_W + 2 * D_MODEL


def _proj_gate_kernel(x_ref, w_ref, cu_ref, gb_ref, ga_ref, gc_ref):
    xb = x_ref[...].astype(BF16)

    def dot_cols(c0, width):
        return jnp.dot(xb, w_ref[:, c0:c0 + width], preferred_element_type=F32)

    cu_ref[...] = dot_cols(0, CONV_W) * dot_cols(CONV_W, CONV_W)
    gb_ref[...] = dot_cols(2 * CONV_W, CONV_W)
    ga_ref[...] = dot_cols(3 * CONV_W, D_MODEL)
    gc_ref[...] = dot_cols(3 * CONV_W + D_MODEL, D_MODEL)


def _proj_gate(h2d, w_b):
    n = h2d.shape[0]
    tm = min(n, 512)
    assert n % tm == 0
    row = lambda i: (i, 0)
    widths = (CONV_W, CONV_W, D_MODEL, D_MODEL)
    return pl.pallas_call(
        _proj_gate_kernel,
        grid=(n // tm,),
        in_specs=[pl.BlockSpec((tm, D_MODEL), row), pl.BlockSpec((D_MODEL, _PROJ_B_W), lambda i: (0, 0))],
        out_specs=[pl.BlockSpec((tm, w), row) for w in widths],
        out_shape=[jax.ShapeDtypeStruct((n, w), F32) for w in widths],
        compiler_params=pltpu.CompilerParams(
            dimension_semantics=("arbitrary",), vmem_limit_bytes=VMEM_LIMIT),
        name="proj_gate",
    )(h2d, w_b)


def _attn_kernel(topk, qi_ref, wi_ref, q_ref, ki_ref, kd_ref, vd_ref, o_ref,
                 keys_ref, bias_ref, wb_ref, qim_ref, qm_ref, t_ref, j_ref, m_ref, l_ref, acc_ref):
    jblk = pl.program_id(1)
    qpos0 = N_META + jblk * TQ
    nch = (qpos0 + TQ - 1) // CH + 1
    kf = float(topk)

    lane = lax.broadcasted_iota(I32, (TQ, LANES), 1)
    lo_half = lane < HEAD_DIM
    rowpos1 = qpos0 + lax.broadcasted_iota(I32, (TQ, 1), 0)
    nvalid = (rowpos1 + 1).astype(F32)
    searchable = nvalid > kf

    for s in range(ATT_HEADS // 2):
        for src, dst in ((qi_ref, qim_ref), (q_ref, qm_ref)):
            slab = src[0, :, s * LANES:(s + 1) * LANES].astype(F32)
            dst[2 * s] = jnp.where(lo_half, slab, 0.0).astype(BF16)
            dst[2 * s + 1] = jnp.where(lo_half, 0.0, slab).astype(BF16)
    wi = wi_ref[0]
    for h in range(IDX_HEADS):
        wb_ref[h] = jnp.broadcast_to(wi[:, h:h + 1], (TQ, CH))

    col_iota = lax.broadcasted_iota(I32, (TQ, CH), 1)
    rowpos = qpos0 + lax.broadcasted_iota(I32, (TQ, CH), 0)

    def score_chunk(c, carry):
        off = pl.multiple_of(c * CH, CH)
        kic = ki_ref[0, pl.ds(off, CH), :]
        acc = jnp.zeros((TQ, CH), F32)
        for h in range(IDX_HEADS):
            d = lax.dot_general(qim_ref[h], kic, _NT, preferred_element_type=F32)
            acc = acc + jnp.maximum(d, 0.0) * wb_ref[h]
        acc = jnp.where(acc == 0.0, 0.0, acc)
        bits = pltpu.bitcast(acc, I32)
        key = jnp.where(bits < 0, bits ^ 0x7FFFFFFF, bits)
        key = jnp.where(off + col_iota <= rowpos, key, INT_MIN)
        keys_ref[:, pl.ds(off, CH)] = key
        return carry

    lax.fori_loop(0, nch, score_chunk, 0)

    def count(pred):
        def body(c, acc):
            off = pl.multiple_of(c * CH, CH)
            hit = jnp.where(pred(keys_ref[:, pl.ds(off, CH)], off + col_iota), 1.0, 0.0)
            return acc + hit[:, :LANES] + hit[:, LANES:]
        acc = lax.fori_loop(0, nch, body, jnp.zeros((TQ, LANES), F32))
        return jnp.sum(acc, axis=1, keepdims=True)

    def n_open(cnt):
        return jnp.max(jnp.where(searchable & (cnt != kf), 1.0, 0.0))

    cnt0 = count(lambda k, col: k >= 0)
    nonneg = cnt0 >= kf
    t0 = jnp.where(nonneg, 0, INT_MIN).astype(I32)
    c0 = jnp.where(nonneg, cnt0, 1e9)

    def bis_cond(st):
        bit, _, _, open_rows = st
        return (bit >= 0) & (open_rows > 0.0)

    def bis_body(st):
        bit, t, ct, _ = st
        cand = t + lax.shift_left(jnp.int32(1), bit)
        cnt = count(lambda k, col: k >= cand)
        ok = cnt >= kf
        t = jnp.where(ok, cand, t)
        ct = jnp.where(ok, cnt, ct)
        return bit - 1, t, ct, n_open(ct)

    _, t_fin, ct_fin, open_fin = lax.while_loop(bis_cond, bis_body, (jnp.int32(30), t0, c0, n_open(c0)))
    t_fin = jnp.where(searchable, t_fin, INT_MIN)
    t_ref[...] = jnp.broadcast_to(t_fin, (TQ, LANES))
    j_ref[...] = jnp.broadcast_to(jnp.where(searchable, 2 ** 30, -1).astype(I32), (TQ, LANES))

    @pl.when(open_fin > 0.0)
    def _():
        tie = searchable & (ct_fin > kf)
        need = kf - count(lambda k, col: k > t_fin)
        jb = jnp.zeros((TQ, 1), I32)
        for bit in range(13, -1, -1):
            cand = jb + (1 << bit)
            f_below = count(lambda k, col: (k == t_fin) & (col < cand))
            jb = jnp.where(f_below < need, cand, jb)
        j_ref[...] = jnp.where(tie, jnp.broadcast_to(jb, (TQ, LANES)), j_ref[...])

    t_row = t_ref[:, :1]
    j_row = j_ref[:, :1]

    def bias_chunk(c, carry):
        off = pl.multiple_of(c * CH, CH)
        k = keys_ref[:, pl.ds(off, CH)]
        sel = (k > t_row) | ((k == t_row) & (off + col_iota <= j_row))
        bias_ref[:, pl.ds(off, CH)] = jnp.where(sel, 0.0, NEG_BIG)
        return carry

    lax.fori_loop(0, nch, bias_chunk, 0)

    m_ref[...] = jnp.full(m_ref.shape, -jnp.inf, F32)
    l_ref[...] = jnp.zeros(l_ref.shape, F32)
    acc_ref[...] = jnp.zeros(acc_ref.shape, F32)

    def attn_chunk(c, carry):
        off = pl.multiple_of(c * CH, CH)
        bias = bias_ref[:, pl.ds(off, CH)]
        for g in range(ATT_KV_HEADS):
            kg = kd_ref[0, pl.ds(off, CH), g * LANES:(g + 1) * LANES]
            vg = vd_ref[0, pl.ds(off, CH), g * LANES:(g + 1) * LANES]
            for r in range(ATT_HEADS // ATT_KV_HEADS):
                h = g * (ATT_HEADS // ATT_KV_HEADS) + r
                logits = lax.dot_general(qm_ref[h], kg, _NT, preferred_element_type=F32) + bias
                m_prev = m_ref[h]
                m_new = jnp.maximum(m_prev, jnp.max(logits, axis=1, keepdims=True))
                alpha = jnp.exp(m_prev - m_new)
                p = jnp.exp(logits - m_new[:, :1])
                l_ref[h] = alpha * l_ref[h] + jnp.sum(p, axis=1, keepdims=True)
                acc_ref[h] = alpha * acc_ref[h] + jnp.dot(p.astype(BF16), vg, preferred_element_type=F32)
                m_ref[h] = m_new
        return carry

    lax.fori_loop(0, nch, attn_chunk, 0)

    for s in range(ATT_HEADS // 2):
        o_even = acc_ref[2 * s] / l_ref[2 * s]
        o_odd = acc_ref[2 * s + 1] / l_ref[2 * s + 1]
        o_ref[0, :, s * LANES:(s + 1) * LANES] = jnp.where(lo_half, o_even, o_odd).astype(o_ref.dtype)


def _dsa_attention(qi, wi, q, ki2, kd, vd, topk):
    b, s, _ = q.shape
    lk = kd.shape[1]
    assert s % TQ == 0 and lk % CH == 0 and lk >= N_META + s
    qblk = lambda bi, j: (bi, j, 0)
    kblk = lambda bi, j: (bi, 0, 0)
    return pl.pallas_call(
        functools.partial(_attn_kernel, topk),
        grid=(b, s // TQ),
        in_specs=[
            pl.BlockSpec((1, TQ, IDX_HEADS * IDX_DIM), qblk),
            pl.BlockSpec((1, TQ, LANES), qblk),
            pl.BlockSpec((1, TQ, ATT_W), qblk),
            pl.BlockSpec((1, lk, LANES), kblk),
            pl.BlockSpec((1, lk, 2 * KV_W), kblk),
            pl.BlockSpec((1, lk, 2 * KV_W), kblk),
        ],
        out_specs=pl.BlockSpec((1, TQ, ATT_W), qblk),
        out_shape=jax.ShapeDtypeStruct((b, s, ATT_W), BF16),
        scratch_shapes=[
            pltpu.VMEM((TQ, lk), I32),
            pltpu.VMEM((TQ, lk), F32),
            pltpu.VMEM((IDX_HEADS, TQ, CH), F32),
            pltpu.VMEM((IDX_HEADS, TQ, LANES), BF16),
            pltpu.VMEM((ATT_HEADS, TQ, LANES), BF16),
            pltpu.VMEM((TQ, LANES), I32),
            pltpu.VMEM((TQ, LANES), I32),
            pltpu.VMEM((ATT_HEADS, TQ, LANES), F32),
            pltpu.VMEM((ATT_HEADS, TQ, LANES), F32),
            pltpu.VMEM((ATT_HEADS, TQ, LANES), F32),
        ],
        compiler_params=pltpu.CompilerParams(
            dimension_semantics=("arbitrary", "arbitrary"), vmem_limit_bytes=VMEM_LIMIT),
        name="dsa_attn",
    )(qi, wi, q, ki2, kd, vd)


def _merge_ln_kernel(att_ref, cu_ref, halo_ref, gb_ref, ga_ref, gc_ref, h_ref, cw_ref,
                     wa_ref, wc_ref, wo_ref, g_ref, b_ref, o_ref):
    tm = cu_ref.shape[1]
    cu = cu_ref[0]
    halo = halo_ref[0, 0]
    row = lax.broadcasted_iota(I32, (tm, CONV_W), 0)
    prev1 = jnp.where(row == 0, halo[7:8], pltpu.roll(cu, 1, 0))
    prev2 = jnp.where(row == 0, halo[6:7], jnp.where(row == 1, halo[7:8], pltpu.roll(cu, 2, 0)))
    cw = cw_ref[...]
    conv = cw[0:1] * prev2 + cw[1:2] * prev1 + cw[2:3] * cu
    y_conv = jnp.dot((gb_ref[0] * conv).astype(BF16), wc_ref[...], preferred_element_type=F32)
    y_att = jnp.dot(att_ref[0], wa_ref[...], preferred_element_type=F32)
    merged = jax.nn.sigmoid(ga_ref[0]) * y_att + jax.nn.sigmoid(gc_ref[0]) * y_conv
    mix = jnp.dot(merged.astype(BF16), wo_ref[...], preferred_element_type=F32)
    o_ref[0] = _layer_norm(DEEPNORM_ALPHA * h_ref[0] + mix, g_ref[...], b_ref[...])


def _merge_ln(att, cu, halo, gb, ga, gc, h1, conv_w8, wa, wc, wo, g, b):
    bsz, s, _ = cu.shape
    tm = s // halo.shape[1]
    blk = lambda bi, i: (bi, i, 0)
    const = lambda bi, i: (0, 0)
    return pl.pallas_call(
        _merge_ln_kernel,
        grid=(bsz, s // tm),
        in_specs=[
            pl.BlockSpec((1, tm, ATT_W), blk),
            pl.BlockSpec((1, tm, CONV_W), blk),
            pl.BlockSpec((1, 1, 8, CONV_W), lambda bi, i: (bi, i, 0, 0)),
            pl.BlockSpec((1, tm, CONV_W), blk),
            pl.BlockSpec((1, tm, D_MODEL), blk),
            pl.BlockSpec((1, tm, D_MODEL), blk),
            pl.BlockSpec((1, tm, D_MODEL), blk),
            pl.BlockSpec((8, CONV_W), const),
            pl.BlockSpec((ATT_W, D_MODEL), const),
            pl.BlockSpec((CONV_W, D_MODEL), const),
            pl.BlockSpec((D_MODEL, D_MODEL), const),
            pl.BlockSpec((1, D_MODEL), const),
            pl.BlockSpec((1, D_MODEL), const),
        ],
        out_specs=pl.BlockSpec((1, tm, D_MODEL), blk),
        out_shape=jax.ShapeDtypeStruct((bsz, s, D_MODEL), F32),
        compiler_params=pltpu.CompilerParams(
            dimension_semantics=("arbitrary", "arbitrary"), vmem_limit_bytes=VMEM_LIMIT),
        name="merge_ln",
    )(att, cu, halo, gb, ga, gc, h1, conv_w8, wa, wc, wo, g, b)


def _rope_tables(pos):
    half = HEAD_DIM // 2
    inv_freq = ROPE_THETA ** (-jnp.arange(half, dtype=F32) / half)
    ang = pos.astype(F32)[:, None] * inv_freq[None, :]
    cos, sin = jnp.cos(ang), jnp.sin(ang)
    return jnp.tile(cos, (1, LANES // half)), jnp.tile(jnp.concatenate([-sin, sin], axis=1), (1, LANES // HEAD_DIM))


def _split_w_in(w_in):
    widths = (ATT_W, KV_W, KV_W, IDX_HEADS * IDX_DIM, IDX_DIM, IDX_HEADS,
              CONV_W, CONV_W, CONV_W, D_MODEL, D_MODEL)
    offs = np.cumsum(widths)[:-1].tolist()
    wq, wk, wv, wqi, wki, wwi, wu, wgb, wgc, wga, wgv = jnp.split(w_in, offs, axis=1)
    dup = lambda w: jnp.concatenate([w[:, :HEAD_DIM], w[:, :HEAD_DIM], w[:, HEAD_DIM:], w[:, HEAD_DIM:]], axis=1)
    w_a = jnp.concatenate(
        [wq, wqi, dup(wk), wki, wki, dup(wv), wwi, jnp.zeros((D_MODEL, LANES - IDX_HEADS), w_in.dtype)], axis=1)
    w_b = jnp.concatenate([wu, wgc, wgb, wga, wgv], axis=1)
    return w_a.astype(BF16), w_b.astype(BF16)


def kernel(x, meta_tokens, ffn1_w_gate, ffn1_w_up, ffn1_w_down, ln1_g, ln1_b, w_in, conv_w, w_att_out, w_conv_out, w_o, ln2_g, ln2_b, ffn2_w_gate, ffn2_w_up, ffn2_w_down, ln3_g, ln3_b):
    bsz, seq, _ = x.shape
    l_tot = N_META + seq
    topk = min(TOPK_MAX, l_tot // 4)
    lk = ((l_tot + CH - 1) // CH) * CH
    lyr = 0
    bf = lambda w: w.astype(BF16)
    vec = lambda v: v[lyr][None, :]

    w1 = (bf(ffn1_w_gate[lyr]), bf(ffn1_w_up[lyr]), bf(ffn1_w_down[lyr]), vec(ln1_g), vec(ln1_b))
    w_a, w_b = _split_w_in(w_in[lyr])
    pos = jnp.arange(l_tot, dtype=I32)
    cos, sin = _rope_tables(pos)

    hm = _ffn_ln(meta_tokens.astype(F32), *w1)
    _, _, kd_m, ki_m, vd_m, _ = _proj_rope(hm, w_a, cos[:N_META], sin[:N_META], N_META)
    cu_m = _proj_gate(hm, w_b)[0]

    h1 = _ffn_ln(x.reshape(bsz * seq, D_MODEL), *w1)
    q, qi, kd, ki2, vd, wi = _proj_rope(h1, w_a, cos[N_META:], sin[N_META:], seq)
    cu, gb, ga, gc = _proj_gate(h1, w_b)

    def with_meta(meta_rows, real):
        wdt = real.shape[-1]
        return jnp.concatenate(
            [jnp.broadcast_to(meta_rows[None], (bsz, N_META, wdt)), real.reshape(bsz, seq, wdt),
             jnp.zeros((bsz, lk - l_tot, wdt), real.dtype)], axis=1)

    per_seq = lambda a: a.reshape(bsz, seq, a.shape[-1])
    att = _dsa_attention(per_seq(qi), per_seq(wi), per_seq(q),
                         with_meta(ki_m, ki2), with_meta(kd_m, kd), with_meta(vd_m, vd), topk)

    tm = min(seq, 512)
    cu3 = per_seq(cu)
    tails = cu3.reshape(bsz, seq // tm, tm, CONV_W)[:, :-1, tm - 8:, :]
    halo = jnp.concatenate([jnp.broadcast_to(cu_m[None, None, N_META - 8:], (bsz, 1, 8, CONV_W)), tails], axis=1)
    conv_w8 = jnp.concatenate([conv_w[lyr].astype(F32), jnp.zeros((8 - CONV_K, CONV_W), F32)], axis=0)
    h2 = _merge_ln(att, cu3, halo, per_seq(gb), per_seq(ga), per_seq(gc), per_seq(h1), conv_w8,
                   bf(w_att_out[lyr]), bf(w_conv_out[lyr]), bf(w_o[lyr]), vec(ln2_g), vec(ln2_b))

    h3 = _ffn_ln(h2.reshape(bsz * seq, D_MODEL), bf(ffn2_w_gate[lyr]), bf(ffn2_w_up[lyr]), bf(ffn2_w_down[lyr]),
                 vec(ln3_g), vec(ln3_b))
    return h3.reshape(bsz, seq, D_MODEL)
```

```python
import functools

import numpy as np
import jax
import jax.numpy as jnp
from jax import lax
from jax.experimental import pallas as pl
from jax.experimental.pallas import tpu as pltpu

F32 = jnp.float32
BF16 = jnp.bfloat16
I32 = jnp.int32

D_MODEL = 1024
N_META = 16
ATT_HEADS = 8
ATT_KV_HEADS = 2
HEAD_DIM = 64
ATT_W = ATT_HEADS * HEAD_DIM
KV_W = ATT_KV_HEADS * HEAD_DIM
IDX_HEADS = 8
IDX_DIM = 64
TOPK_MAX = 256
ROPE_THETA = 10000.0
CONV_W = D_MODEL // 2
CONV_K = 3
D_FF = 2816
LN_EPS = 1e-5
DEPTH = 1
DEEPNORM_ALPHA = (2.0 * DEPTH) ** 0.25

LANES = 128
SUBLANES = 8
TQ = 128
CH = 256
UNROLL = 4
INT_MIN = -(2 ** 31)
NEG_BIG = -1e30
VMEM_LIMIT = 52 * 1024 * 1024


def _layer_norm(y, g, b):
    mu = jnp.mean(y, axis=-1, keepdims=True)
    yc = y - mu
    var = jnp.mean(yc * yc, axis=-1, keepdims=True)
    return yc * lax.rsqrt(var + LN_EPS) * g + b


def _ffn_ln_kernel(x_ref, wg_ref, wu_ref, wd_ref, g_ref, b_ref, o_ref, acc_ref, xb_ref):
    f = pl.program_id(1)

    @pl.when(f == 0)
    def _():
        acc_ref[...] = jnp.zeros_like(acc_ref)
        xb_ref[...] = x_ref[...].astype(BF16)

    xb = xb_ref[...]
    gt = jnp.dot(xb, wg_ref[...], preferred_element_type=F32)
    up = jnp.dot(xb, wu_ref[...], preferred_element_type=F32)
    a = (gt * jax.nn.sigmoid(gt)) * up
    acc_ref[...] += jnp.dot(a.astype(BF16), wd_ref[...], preferred_element_type=F32)

    @pl.when(f == pl.num_programs(1) - 1)
    def _():
        y = DEEPNORM_ALPHA * x_ref[...] + 0.5 * acc_ref[...]
        o_ref[...] = _layer_norm(y, g_ref[...], b_ref[...])


def _ffn_ln(x2d, wg, wu, wd, g, b):
    n = x2d.shape[0]
    tm = min(n, 1024)
    tf = 256
    assert n % tm == 0 and D_FF % tf == 0
    return pl.pallas_call(
        _ffn_ln_kernel,
        grid=(n // tm, D_FF // tf),
        in_specs=[
            pl.BlockSpec((tm, D_MODEL), lambda i, f: (i, 0)),
            pl.BlockSpec((D_MODEL, tf), lambda i, f: (0, f)),
            pl.BlockSpec((D_MODEL, tf), lambda i, f: (0, f)),
            pl.BlockSpec((tf, D_MODEL), lambda i, f: (f, 0)),
            pl.BlockSpec((1, D_MODEL), lambda i, f: (0, 0)),
            pl.BlockSpec((1, D_MODEL), lambda i, f: (0, 0)),
        ],
        out_specs=pl.BlockSpec((tm, D_MODEL), lambda i, f: (i, 0)),
        out_shape=jax.ShapeDtypeStruct((n, D_MODEL), F32),
        scratch_shapes=[pltpu.VMEM((tm, D_MODEL), F32), pltpu.VMEM((tm, D_MODEL), BF16)],
        compiler_params=pltpu.CompilerParams(
            dimension_semantics=("arbitrary", "arbitrary"), vmem_limit_bytes=VMEM_LIMIT),
        name="ffn_ln",
    )(x2d, wg, wu, wd, g, b)


_PROJ_A_W = ATT_W + IDX_HEADS * IDX_DIM + 2 * KV_W + LANES + 2 * KV_W + LANES


def _proj_rope_kernel(x_ref, w_ref, cos_ref, sin_ref, q_ref, qi_ref, kd_ref, ki_ref, vd_ref, wi_ref):
    tm = x_ref.shape[0]
    xb = x_ref[...].astype(BF16)
    cos = cos_ref[...]
    sin = sin_ref[...]
    lane = lax.broadcasted_iota(I32, (tm, LANES), 1)
    first_half = (lane & (HEAD_DIM // 2)) == 0

    def rope(s):
        partner = jnp.where(first_half, pltpu.roll(s, LANES - HEAD_DIM // 2, 1),
                            pltpu.roll(s, HEAD_DIM // 2, 1))
        return s * cos + partner * sin

    def dot_cols(c0, width):
        return jnp.dot(xb, w_ref[:, c0:c0 + width], preferred_element_type=F32)

    outs = ((q_ref, ATT_W, HEAD_DIM ** -0.5), (qi_ref, IDX_HEADS * IDX_DIM, 1.0), (kd_ref, 2 * KV_W, 1.0))
    c0 = 0
    for ref, width, scale in outs:
        for j in range(0, width, 2 * LANES):
            p = dot_cols(c0 + j, 2 * LANES)
            for s in range(2):
                r = rope(p[:, s * LANES:(s + 1) * LANES])
                if scale != 1.0:
                    r = r * scale
                ref[:, j + s * LANES:j + (s + 1) * LANES] = r.astype(ref.dtype)
        c0 += width
    p = dot_cols(c0, LANES)
    ki_ref[...] = rope(p).astype(ki_ref.dtype)
    c0 += LANES
    vd_ref[...] = dot_cols(c0, 2 * KV_W).astype(vd_ref.dtype)
    c0 += 2 * KV_W
    wi_ref[...] = dot_cols(c0, LANES) * ((IDX_HEADS * IDX_DIM) ** -0.5)


def _proj_rope(h2d, w_a, cos, sin, rows_per_seq):
    n = h2d.shape[0]
    tm = min(rows_per_seq, 512)
    tiles_per_seq = rows_per_seq // tm
    assert n % tm == 0 and rows_per_seq % tm == 0
    row = lambda i: (i, 0)
    tab = lambda i: (i % tiles_per_seq, 0)
    widths = (ATT_W, IDX_HEADS * IDX_DIM, 2 * KV_W, LANES, 2 * KV_W, LANES)
    dtypes = (BF16, BF16, BF16, BF16, BF16, F32)
    return pl.pallas_call(
        _proj_rope_kernel,
        grid=(n // tm,),
        in_specs=[
            pl.BlockSpec((tm, D_MODEL), row),
            pl.BlockSpec((D_MODEL, _PROJ_A_W), lambda i: (0, 0)),
            pl.BlockSpec((tm, LANES), tab),
            pl.BlockSpec((tm, LANES), tab),
        ],
        out_specs=[pl.BlockSpec((tm, w), row) for w in widths],
        out_shape=[jax.ShapeDtypeStruct((n, w), dt) for w, dt in zip(widths, dtypes)],
        compiler_params=pltpu.CompilerParams(
            dimension_semantics=("arbitrary",), vmem_limit_bytes=VMEM_LIMIT),
        name="proj_rope",
    )(h2d, w_a, cos, sin)


_PROJ_B_W = 3 * CONV_W + 2 * D_MODEL


def _proj_gate_kernel(x_ref, w_ref, cu_ref, gb_ref, ga_ref, gc_ref):
    xb = x_ref[...].astype(BF16)

    def dot_cols(c0, width):
        return jnp.dot(xb, w_ref[:, c0:c0 + width], preferred_element_type=F32)

    cu_ref[...] = dot_cols(0, CONV_W) * dot_cols(CONV_W, CONV_W)
    gb_ref[...] = dot_cols(2 * CONV_W, CONV_W)
    ga_ref[...] = dot_cols(3 * CONV_W, D_MODEL)
    gc_ref[...] = dot_cols(3 * CONV_W + D_MODEL, D_MODEL)


def _proj_gate(h2d, w_b):
    n = h2d.shape[0]
    tm = min(n, 512)
    assert n % tm == 0
    row = lambda i: (i, 0)
    widths = (CONV_W, CONV_W, D_MODEL, D_MODEL)
    return pl.pallas_call(
        _proj_gate_kernel,
        grid=(n // tm,),
        in_specs=[pl.BlockSpec((tm, D_MODEL), row), pl.BlockSpec((D_MODEL, _PROJ_B_W), lambda i: (0, 0))],
        out_specs=[pl.BlockSpec((tm, w), row) for w in widths],
        out_shape=[jax.ShapeDtypeStruct((n, w), F32) for w in widths],
        compiler_params=pltpu.CompilerParams(
            dimension_semantics=("arbitrary",), vmem_limit_bytes=VMEM_LIMIT),
        name="proj_gate",
    )(h2d, w_b)


def _fold_rows(x, op):
    parts = [x[i:i + SUBLANES] for i in range(0, x.shape[0], SUBLANES)]
    while len(parts) > 1:
        parts = [op(parts[i], parts[i + 1]) for i in range(0, len(parts), 2)]
    return parts[0]


def _chunk_loop(nch, body, init):
    n_main = nch // UNROLL

    def main(i, carry):
        for u in range(UNROLL):
            carry = body(i * UNROLL + u, carry)
        return carry

    carry = lax.fori_loop(0, n_main, main, init)
    return lax.fori_loop(n_main * UNROLL, nch, body, carry)


def _attn_kernel(topk, qi_ref, wit_ref, q_ref, ki_ref, kd_ref, vt_ref, o_ref,
                 keys_ref, bias_ref, p_ref, qit_ref, qt_ref, j_ref, m_ref, acc_ref):
    jblk = pl.program_id(1)
    qpos0 = N_META + jblk * TQ
    nch = (qpos0 + TQ - 1) // CH + 1
    kf = float(topk)
    n_pairs = ATT_HEADS // 2

    lane = lax.broadcasted_iota(I32, (TQ, LANES), 1)
    lo_half = lane < HEAD_DIM
    qpos_row = qpos0 + lax.broadcasted_iota(I32, (1, TQ), 1)
    searchable = (qpos_row + 1).astype(F32) > kf

    for s in range(n_pairs):
        for src, dst in ((qi_ref, qit_ref), (q_ref, qt_ref)):
            slab = src[0, :, s * LANES:(s + 1) * LANES].astype(F32)
            even = jnp.where(lo_half, slab, 0.0).T
            odd = jnp.where(lo_half, 0.0, slab).T
            dst[s] = jnp.concatenate([even, odd], axis=1).astype(BF16)

    kpos_iota = lax.broadcasted_iota(I32, (CH, TQ), 0)
    qpos = qpos0 + lax.broadcasted_iota(I32, (CH, TQ), 1)

    def score_chunk(c, carry):
        off = pl.multiple_of(c * CH, CH)
        kic = ki_ref[0, pl.ds(off, CH), :]
        acc = jnp.zeros((CH, TQ), F32)
        for s in range(IDX_HEADS // 2):
            d = jnp.dot(kic, qit_ref[s], preferred_element_type=F32)
            for e in range(2):
                w = wit_ref[0, 2 * s + e:2 * s + e + 1, :]
                acc = acc + jnp.maximum(d[:, e * LANES:(e + 1) * LANES], 0.0) * w
        acc = jnp.where(acc == 0.0, 0.0, acc)
        bits = pltpu.bitcast(acc, I32)
        key = jnp.where(bits < 0, bits ^ 0x7FFFFFFF, bits)
        key = jnp.where(off + kpos_iota <= qpos, key, INT_MIN)
        keys_ref[pl.ds(off, CH), :] = key
        return carry

    _chunk_loop(nch, score_chunk, 0)

    def count(pred):
        def body(c, acc):
            off = pl.multiple_of(c * CH, CH)
            hit = jnp.where(pred(keys_ref[pl.ds(off, CH), :], off + kpos_iota), 1.0, 0.0)
            return acc + _fold_rows(hit, jnp.add)
        acc = _chunk_loop(nch, body, jnp.zeros((SUBLANES, TQ), F32))
        return jnp.sum(acc, axis=0, keepdims=True)

    def n_open(cnt):
        return jnp.max(jnp.where(searchable & (cnt != kf), 1.0, 0.0))

    cnt0 = count(lambda k, pos: k >= 0)
    nonneg = cnt0 >= kf
    t0 = jnp.where(nonneg, 0, INT_MIN).astype(I32)
    c0 = jnp.where(nonneg, cnt0, 1e9)

    def bis_cond(st):
        bit, _, _, open_q = st
        return (bit >= 0) & (open_q > 0.0)

    def bis_body(st):
        bit, t, ct, _ = st
        cand = t + lax.shift_left(jnp.int32(1), bit)
        cnt = count(lambda k, pos: k >= cand)
        ok = cnt >= kf
        t = jnp.where(ok, cand, t)
        ct = jnp.where(ok, cnt, ct)
        return bit - 1, t, ct, n_open(ct)

    _, t_fin, ct_fin, open_fin = lax.while_loop(bis_cond, bis_body, (jnp.int32(30), t0, c0, n_open(c0)))
    t_fin = jnp.where(searchable, t_fin, INT_MIN)
    j_ref[...] = jnp.broadcast_to(jnp.where(searchable, 2 ** 30, -1).astype(I32), j_ref.shape)

    @pl.when(open_fin > 0.0)
    def _():
        tie = searchable & (ct_fin > kf)
        need = kf - count(lambda k, pos: k > t_fin)
        jb = jnp.zeros((1, TQ), I32)
        for bit in range(13, -1, -1):
            cand = jb + (1 << bit)
            f_below = count(lambda k, pos: (k == t_fin) & (pos < cand))
            jb = jnp.where(f_below < need, cand, jb)
        j_ref[...] = jnp.where(tie, jnp.broadcast_to(jb, j_ref.shape), j_ref[...])

    j_fin = j_ref[0:1, :]

    def logits(c, s):
        g = (2 * s) // (ATT_HEADS // ATT_KV_HEADS)
        kg = kd_ref[0, pl.ds(pl.multiple_of(c * CH, CH), CH), g * LANES:(g + 1) * LANES]
        return jnp.dot(kg, qt_ref[s], preferred_element_type=F32)

    def max_chunk(c, ms):
        off = pl.multiple_of(c * CH, CH)
        k = keys_ref[pl.ds(off, CH), :]
        sel = (k > t_fin) | ((k == t_fin) & (off + kpos_iota <= j_fin))
        bias = jnp.where(sel, 0.0, NEG_BIG)
        bias_ref[pl.ds(off, CH), :] = bias
        new = []
        for s in range(n_pairs):
            lt2 = logits(c, s)
            for e in range(2):
                lt = lt2[:, e * TQ:(e + 1) * TQ] + bias
                new.append(jnp.maximum(ms[2 * s + e], _fold_rows(lt, jnp.maximum)))
        return tuple(new)

    ms = _chunk_loop(nch, max_chunk, tuple(jnp.full((SUBLANES, TQ), -jnp.inf, F32) for _ in range(ATT_HEADS)))
    for h in range(ATT_HEADS):
        m_ref[h:h + 1, :] = jnp.max(ms[h], axis=0, keepdims=True)

    def p_chunk(c, ls):
        off = pl.multiple_of(c * CH, CH)
        bias = bias_ref[pl.ds(off, CH), :]
        new = []
        for s in range(n_pairs):
            lt2 = logits(c, s)
            ps = []
            for e in range(2):
                h = 2 * s + e
                p = jnp.exp(lt2[:, e * TQ:(e + 1) * TQ] + bias - m_ref[h:h + 1, :])
                new.append(ls[h] + _fold_rows(p, jnp.add))
                ps.append(p.astype(BF16))
            p_ref[s, pl.ds(off, CH), :] = jnp.concatenate(ps, axis=1)
        return tuple(new)

    ls = _chunk_loop(nch, p_chunk, tuple(jnp.zeros((SUBLANES, TQ), F32) for _ in range(ATT_HEADS)))

    def pv_span(off, width):
        for s in range(n_pairs):
            g = (2 * s) // (ATT_HEADS // ATT_KV_HEADS)
            vgt = vt_ref[0, g * HEAD_DIM:(g + 1) * HEAD_DIM, pl.ds(off, width)]
            acc_ref[s] += jnp.dot(vgt, p_ref[s, pl.ds(off, width), :], preferred_element_type=F32)

    acc_ref[...] = jnp.zeros(acc_ref.shape, F32)
    n_main = nch // UNROLL

    def pv_main(i, carry):
        pv_span(pl.multiple_of(i * (UNROLL * CH), UNROLL * CH), UNROLL * CH)
        return carry

    def pv_rest(c, carry):
        pv_span(pl.multiple_of(c * CH, CH), CH)
        return carry

    lax.fori_loop(0, n_main, pv_main, 0)
    lax.fori_loop(n_main * UNROLL, nch, pv_rest, 0)

    for s in range(n_pairs):
        o_t = jnp.concatenate(
            [acc_ref[s, :, e * TQ:(e + 1) * TQ] / jnp.sum(ls[2 * s + e], axis=0, keepdims=True)
             for e in range(2)], axis=0)
        o_ref[0, :, s * LANES:(s + 1) * LANES] = o_t.T.astype(o_ref.dtype)


def _dsa_attention(qi, wit, q, ki2, kd, vt, topk):
    b, s, _ = q.shape
    lk = kd.shape[1]
    assert s % TQ == 0 and lk % CH == 0 and lk >= N_META + s
    qblk = lambda bi, j: (bi, j, 0)
    kblk = lambda bi, j: (bi, 0, 0)
    return pl.pallas_call(
        functools.partial(_attn_kernel, topk),
        grid=(b, s // TQ),
        in_specs=[
            pl.BlockSpec((1, TQ, IDX_HEADS * IDX_DIM), qblk),
            pl.BlockSpec((1, IDX_HEADS, TQ), lambda bi, j: (bi, 0, j)),
            pl.BlockSpec((1, TQ, ATT_W), qblk),
            pl.BlockSpec((1, lk, LANES), kblk),
            pl.BlockSpec((1, lk, 2 * KV_W), kblk),
            pl.BlockSpec((1, KV_W, lk), kblk),
        ],
        out_specs=pl.BlockSpec((1, TQ, ATT_W), qblk),
        out_shape=jax.ShapeDtypeStruct((b, s, ATT_W), BF16),
        scratch_shapes=[
            pltpu.VMEM((lk, TQ), I32),
            pltpu.VMEM((lk, TQ), F32),
            pltpu.VMEM((ATT_HEADS // 2, lk, 2 * TQ), BF16),
            pltpu.VMEM((IDX_HEADS // 2, LANES, 2 * TQ), BF16),
            pltpu.VMEM((ATT_HEADS // 2, LANES, 2 * TQ), BF16),
            pltpu.VMEM((SUBLANES, TQ), I32),
            pltpu.VMEM((ATT_HEADS, TQ), F32),
            pltpu.VMEM((ATT_HEADS // 2, HEAD_DIM, 2 * TQ), F32),
        ],
        compiler_params=pltpu.CompilerParams(
            dimension_semantics=("arbitrary", "arbitrary"), vmem_limit_bytes=VMEM_LIMIT),
        name="dsa_attn",
    )(qi, wit, q, ki2, kd, vt)


def _merge_ln_kernel(att_ref, cu_ref, halo_ref, gb_ref, ga_ref, gc_ref, h_ref, cw_ref,
                     wa_ref, wc_ref, wo_ref, g_ref, b_ref, o_ref):
    tm = cu_ref.shape[1]
    cu = cu_ref[0]
    halo = halo_ref[0, 0]
    row = lax.broadcasted_iota(I32, (tm, CONV_W), 0)
    prev1 = jnp.where(row == 0, halo[7:8], pltpu.roll(cu, 1, 0))
    prev2 = jnp.where(row == 0, halo[6:7], jnp.where(row == 1, halo[7:8], pltpu.roll(cu, 2, 0)))
    cw = cw_ref[...]
    conv = cw[0:1] * prev2 + cw[1:2] * prev1 + cw[2:3] * cu
    y_conv = jnp.dot((gb_ref[0] * conv).astype(BF16), wc_ref[...], preferred_element_type=F32)
    y_att = jnp.dot(att_ref[0], wa_ref[...], preferred_element_type=F32)
    merged = jax.nn.sigmoid(ga_ref[0]) * y_att + jax.nn.sigmoid(gc_ref[0]) * y_conv
    mix = jnp.dot(merged.astype(BF16), wo_ref[...], preferred_element_type=F32)
    o_ref[0] = _layer_norm(DEEPNORM_ALPHA * h_ref[0] + mix, g_ref[...], b_ref[...])


def _merge_ln(att, cu, halo, gb, ga, gc, h1, conv_w8, wa, wc, wo, g, b):
    bsz, s, _ = cu.shape
    tm = s // halo.shape[1]
    blk = lambda bi, i: (bi, i, 0)
    const = lambda bi, i: (0, 0)
    return pl.pallas_call(
        _merge_ln_kernel,
        grid=(bsz, s // tm),
        in_specs=[
            pl.BlockSpec((1, tm, ATT_W), blk),
            pl.BlockSpec((1, tm, CONV_W), blk),
            pl.BlockSpec((1, 1, 8, CONV_W), lambda bi, i: (bi, i, 0, 0)),
            pl.BlockSpec((1, tm, CONV_W), blk),
            pl.BlockSpec((1, tm, D_MODEL), blk),
            pl.BlockSpec((1, tm, D_MODEL), blk),
            pl.BlockSpec((1, tm, D_MODEL), blk),
            pl.BlockSpec((8, CONV_W), const),
            pl.BlockSpec((ATT_W, D_MODEL), const),
            pl.BlockSpec((CONV_W, D_MODEL), const),
            pl.BlockSpec((D_MODEL, D_MODEL), const),
            pl.BlockSpec((1, D_MODEL), const),
            pl.BlockSpec((1, D_MODEL), const),
        ],
        out_specs=pl.BlockSpec((1, tm, D_MODEL), blk),
        out_shape=jax.ShapeDtypeStruct((bsz, s, D_MODEL), F32),
        compiler_params=pltpu.CompilerParams(
            dimension_semantics=("arbitrary", "arbitrary"), vmem_limit_bytes=VMEM_LIMIT),
        name="merge_ln",
    )(att, cu, halo, gb, ga, gc, h1, conv_w8, wa, wc, wo, g, b)


def _rope_tables(pos):
    half = HEAD_DIM // 2
    inv_freq = ROPE_THETA ** (-jnp.arange(half, dtype=F32) / half)
    ang = pos.astype(F32)[:, None] * inv_freq[None, :]
    cos, sin = jnp.cos(ang), jnp.sin(ang)
    return jnp.tile(cos, (1, LANES // half)), jnp.tile(jnp.concatenate([-sin, sin], axis=1), (1, LANES // HEAD_DIM))


def _split_w_in(w_in):
    widths = (ATT_W, KV_W, KV_W, IDX_HEADS * IDX_DIM, IDX_DIM, IDX_HEADS,
              CONV_W, CONV_W, CONV_W, D_MODEL, D_MODEL)
    offs = np.cumsum(widths)[:-1].tolist()
    wq, wk, wv, wqi, wki, wwi, wu, wgb, wgc, wga, wgv = jnp.split(w_in, offs, axis=1)
    dup = lambda w: jnp.concatenate([w[:, :HEAD_DIM], w[:, :HEAD_DIM], w[:, HEAD_DIM:], w[:, HEAD_DIM:]], axis=1)
    w_a = jnp.concatenate(
        [wq, wqi, dup(wk), wki, wki, dup(wv), wwi, jnp.zeros((D_MODEL, LANES - IDX_HEADS), w_in.dtype)], axis=1)
    w_b = jnp.concatenate([wu, wgc, wgb, wga, wgv], axis=1)
    return w_a.astype(BF16), w_b.astype(BF16)


def kernel(x, meta_tokens, ffn1_w_gate, ffn1_w_up, ffn1_w_down, ln1_g, ln1_b, w_in, conv_w, w_att_out, w_conv_out, w_o, ln2_g, ln2_b, ffn2_w_gate, ffn2_w_up, ffn2_w_down, ln3_g, ln3_b):
    bsz, seq, _ = x.shape
    l_tot = N_META + seq
    topk = min(TOPK_MAX, l_tot // 4)
    lk = ((l_tot + CH - 1) // CH) * CH
    lyr = 0
    bf = lambda w: w.astype(BF16)
    vec = lambda v: v[lyr][None, :]

    w1 = (bf(ffn1_w_gate[lyr]), bf(ffn1_w_up[lyr]), bf(ffn1_w_down[lyr]), vec(ln1_g), vec(ln1_b))
    w_a, w_b = _split_w_in(w_in[lyr])
    pos = jnp.arange(l_tot, dtype=I32)
    cos, sin = _rope_tables(pos)

    hm = _ffn_ln(meta_tokens.astype(F32), *w1)
    _, _, kd_m, ki_m, vd_m, _ = _proj_rope(hm, w_a, cos[:N_META], sin[:N_META], N_META)
    cu_m = _proj_gate(hm, w_b)[0]

    h1 = _ffn_ln(x.reshape(bsz * seq, D_MODEL), *w1)
    q, qi, kd, ki2, vd, wi = _proj_rope(h1, w_a, cos[N_META:], sin[N_META:], seq)
    cu, gb, ga, gc = _proj_gate(h1, w_b)

    def with_meta(meta_rows, real):
        wdt = real.shape[-1]
        return jnp.concatenate(
            [jnp.broadcast_to(meta_rows[None], (bsz, N_META, wdt)), real.reshape(bsz, seq, wdt),
             jnp.zeros((bsz, lk - l_tot, wdt), real.dtype)], axis=1)

    per_seq = lambda a: a.reshape(bsz, seq, a.shape[-1])
    undup = lambda a: jnp.concatenate([a[:, :HEAD_DIM], a[:, 2 * HEAD_DIM:3 * HEAD_DIM]], axis=1)
    vt = jnp.swapaxes(with_meta(undup(vd_m), undup(vd)), 1, 2)
    wit = jnp.swapaxes(per_seq(wi)[:, :, :IDX_HEADS], 1, 2)
    att = _dsa_attention(per_seq(qi), wit, per_seq(q), with_meta(ki_m, ki2), with_meta(kd_m, kd), vt, topk)

    tm = min(seq, 512)
    cu3 = per_seq(cu)
    tails = cu3.reshape(bsz, seq // tm, tm, CONV_W)[:, :-1, tm - 8:, :]
    halo = jnp.concatenate([jnp.broadcast_to(cu_m[None, None, N_META - 8:], (bsz, 1, 8, CONV_W)), tails], axis=1)
    conv_w8 = jnp.concatenate([conv_w[lyr].astype(F32), jnp.zeros((8 - CONV_K, CONV_W), F32)], axis=0)
    h2 = _merge_ln(att, cu3, halo, per_seq(gb), per_seq(ga), per_seq(gc), per_seq(h1), conv_w8,
                   bf(w_att_out[lyr]), bf(w_conv_out[lyr]), bf(w_o[lyr]), vec(ln2_g), vec(ln2_b))

    h3 = _ffn_ln(h2.reshape(bsz * seq, D_MODEL), bf(ffn2_w_gate[lyr]), bf(ffn2_w_up[lyr]), bf(ffn2_w_down[lyr]),
                 vec(ln3_g), vec(ln3_b))
    return h3.reshape(bsz, seq, D_MODEL)
```

```python
import functools

import numpy as np
import jax
import jax.numpy as jnp
from jax import lax
from jax.experimental import pallas as pl
from jax.experimental.pallas import tpu as pltpu

F32 = jnp.float32
BF16 = jnp.bfloat16
I32 = jnp.int32

D_MODEL = 1024
N_META = 16
ATT_HEADS = 8
ATT_KV_HEADS = 2
HEAD_DIM = 64
ATT_W = ATT_HEADS * HEAD_DIM
KV_W = ATT_KV_HEADS * HEAD_DIM
IDX_HEADS = 8
IDX_DIM = 64
TOPK_MAX = 256
ROPE_THETA = 10000.0
CONV_W = D_MODEL // 2
CONV_K = 3
D_FF = 2816
LN_EPS = 1e-5
DEPTH = 1
DEEPNORM_ALPHA = (2.0 * DEPTH) ** 0.25

LANES = 128
SUBLANES = 8
TQ = 128
CH = 256
UNROLL = 4
BITS_PER_EXIT_TEST = 4
LOG2E = 1.4426950408889634
INT_MIN = -(2 ** 31)
NEG_BIG = -1e30
VMEM_LIMIT = 52 * 1024 * 1024


def _layer_norm(y, g, b):
    mu = jnp.mean(y, axis=-1, keepdims=True)
    yc = y - mu
    var = jnp.mean(yc * yc, axis=-1, keepdims=True)
    return yc * lax.rsqrt(var + LN_EPS) * g + b


def _ffn_ln_kernel(x_ref, wg_ref, wu_ref, wd_ref, g_ref, b_ref, o_ref, acc_ref, xb_ref):
    f = pl.program_id(1)

    @pl.when(f == 0)
    def _():
        acc_ref[...] = jnp.zeros_like(acc_ref)
        xb_ref[...] = x_ref[...].astype(BF16)

    xb = xb_ref[...]
    gt = jnp.dot(xb, wg_ref[...], preferred_element_type=F32)
    up = jnp.dot(xb, wu_ref[...], preferred_element_type=F32)
    a = (gt * jax.nn.sigmoid(gt)) * up
    acc_ref[...] += jnp.dot(a.astype(BF16), wd_ref[...], preferred_element_type=F32)

    @pl.when(f == pl.num_programs(1) - 1)
    def _():
        y = DEEPNORM_ALPHA * x_ref[...] + 0.5 * acc_ref[...]
        o_ref[...] = _layer_norm(y, g_ref[...], b_ref[...])


def _ffn_ln(x2d, wg, wu, wd, g, b):
    n = x2d.shape[0]
    tm = min(n, 1024)
    tf = 256
    assert n % tm == 0 and D_FF % tf == 0
    return pl.pallas_call(
        _ffn_ln_kernel,
        grid=(n // tm, D_FF // tf),
        in_specs=[
            pl.BlockSpec((tm, D_MODEL), lambda i, f: (i, 0)),
            pl.BlockSpec((D_MODEL, tf), lambda i, f: (0, f)),
            pl.BlockSpec((D_MODEL, tf), lambda i, f: (0, f)),
            pl.BlockSpec((tf, D_MODEL), lambda i, f: (f, 0)),
            pl.BlockSpec((1, D_MODEL), lambda i, f: (0, 0)),
            pl.BlockSpec((1, D_MODEL), lambda i, f: (0, 0)),
        ],
        out_specs=pl.BlockSpec((tm, D_MODEL), lambda i, f: (i, 0)),
        out_shape=jax.ShapeDtypeStruct((n, D_MODEL), F32),
        scratch_shapes=[pltpu.VMEM((tm, D_MODEL), F32), pltpu.VMEM((tm, D_MODEL), BF16)],
        compiler_params=pltpu.CompilerParams(
            dimension_semantics=("arbitrary", "arbitrary"), vmem_limit_bytes=VMEM_LIMIT),
        name="ffn_ln",
    )(x2d, wg, wu, wd, g, b)


_PROJ_A_W = ATT_W + IDX_HEADS * IDX_DIM + 2 * KV_W + LANES + 2 * KV_W + LANES


def _proj_rope_kernel(x_ref, w_ref, cos_ref, sin_ref, q_ref, qi_ref, kd_ref, ki_ref, vd_ref, wi_ref):
    tm = x_ref.shape[0]
    xb = x_ref[...].astype(BF16)
    cos = cos_ref[...]
    sin = sin_ref[...]
    lane = lax.broadcasted_iota(I32, (tm, LANES), 1)
    first_half = (lane & (HEAD_DIM // 2)) == 0

    def rope(s):
        partner = jnp.where(first_half, pltpu.roll(s, LANES - HEAD_DIM // 2, 1),
                            pltpu.roll(s, HEAD_DIM // 2, 1))
        return s * cos + partner * sin

    def dot_cols(c0, width):
        return jnp.dot(xb, w_ref[:, c0:c0 + width], preferred_element_type=F32)

    outs = ((q_ref, ATT_W, HEAD_DIM ** -0.5 * LOG2E), (qi_ref, IDX_HEADS * IDX_DIM, 1.0), (kd_ref, 2 * KV_W, 1.0))
    c0 = 0
    for ref, width, scale in outs:
        for j in range(0, width, 2 * LANES):
            p = dot_cols(c0 + j, 2 * LANES)
            for s in range(2):
                r = rope(p[:, s * LANES:(s + 1) * LANES])
                if scale != 1.0:
                    r = r * scale
                ref[:, j + s * LANES:j + (s + 1) * LANES] = r.astype(ref.dtype)
        c0 += width
    p = dot_cols(c0, LANES)
    ki_ref[...] = rope(p).astype(ki_ref.dtype)
    c0 += LANES
    vd_ref[...] = dot_cols(c0, 2 * KV_W).astype(vd_ref.dtype)
    c0 += 2 * KV_W
    wi_ref[...] = dot_cols(c0, LANES) * ((IDX_HEADS * IDX_DIM) ** -0.5)


def _proj_rope(h2d, w_a, cos, sin, rows_per_seq):
    n = h2d.shape[0]
    tm = min(rows_per_seq, 512)
    tiles_per_seq = rows_per_seq // tm
    assert n % tm == 0 and rows_per_seq % tm == 0
    row = lambda i: (i, 0)
    tab = lambda i: (i % tiles_per_seq, 0)
    widths = (ATT_W, IDX_HEADS * IDX_DIM, 2 * KV_W, LANES, 2 * KV_W, LANES)
    dtypes = (BF16, BF16, BF16, BF16, BF16, F32)
    return pl.pallas_call(
        _proj_rope_kernel,
        grid=(n // tm,),
        in_specs=[
            pl.BlockSpec((tm, D_MODEL), row),
            pl.BlockSpec((D_MODEL, _PROJ_A_W), lambda i: (0, 0)),
            pl.BlockSpec((tm, LANES), tab),
            pl.BlockSpec((tm, LANES), tab),
        ],
        out_specs=[pl.BlockSpec((tm, w), row) for w in widths],
        out_shape=[jax.ShapeDtypeStruct((n, w), dt) for w, dt in zip(widths, dtypes)],
        compiler_params=pltpu.CompilerParams(
            dimension_semantics=("arbitrary",), vmem_limit_bytes=VMEM_LIMIT),
        name="proj_rope",
    )(h2d, w_a, cos, sin)


_PROJ_B_W = 3 * CONV_W + 2 * D_MODEL


def _proj_gate_kernel(x_ref, w_ref, cu_ref, gb_ref, ga_ref, gc_ref):
    xb = x_ref[...].astype(BF16)

    def dot_cols(c0, width):
        return jnp.dot(xb, w_ref[:, c0:c0 + width], preferred_element_type=F32)

    cu_ref[...] = dot_cols(0, CONV_W) * dot_cols(CONV_W, CONV_W)
    gb_ref[...] = dot_cols(2 * CONV_W, CONV_W)
    ga_ref[...] = dot_cols(3 * CONV_W, D_MODEL)
    gc_ref[...] = dot_cols(3 * CONV_W + D_MODEL, D_MODEL)


def _proj_gate(h2d, w_b):
    n = h2d.shape[0]
    tm = min(n, 512)
    assert n % tm == 0
    row = lambda i: (i, 0)
    widths = (CONV_W, CONV_W, D_MODEL, D_MODEL)
    return pl.pallas_call(
        _proj_gate_kernel,
        grid=(n // tm,),
        in_specs=[pl.BlockSpec((tm, D_MODEL), row), pl.BlockSpec((D_MODEL, _PROJ_B_W), lambda i: (0, 0))],
        out_specs=[pl.BlockSpec((tm, w), row) for w in widths],
        out_shape=[jax.ShapeDtypeStruct((n, w), F32) for w in widths],
        compiler_params=pltpu.CompilerParams(
            dimension_semantics=("arbitrary",), vmem_limit_bytes=VMEM_LIMIT),
        name="proj_gate",
    )(h2d, w_b)


def _fold_rows(x, op):
    parts = [x[i:i + SUBLANES] for i in range(0, x.shape[0], SUBLANES)]
    while len(parts) > 1:
        parts = [op(parts[i], parts[i + 1]) for i in range(0, len(parts), 2)]
    return parts[0]


def _chunk_loop(nch, body, init):
    n_main = nch // UNROLL

    def main(i, carry):
        for u in range(UNROLL):
            carry = body(i * UNROLL + u, carry)
        return carry

    carry = lax.fori_loop(0, n_main, main, init)
    return lax.fori_loop(n_main * UNROLL, nch, body, carry)


def _attn_kernel(topk, qi_ref, wit_ref, q_ref, ki_ref, kd_ref, vt_ref, o_ref,
                 keys_ref, stage_ref, p_ref, qit_ref, qt_ref, j_ref, m_ref, acc_ref):
    jblk = pl.program_id(1)
    qpos0 = N_META + jblk * TQ
    nch = (qpos0 + TQ - 1) // CH + 1
    kf = float(topk)
    n_pairs = ATT_HEADS // 2

    lane = lax.broadcasted_iota(I32, (TQ, LANES), 1)
    lo_half = lane < HEAD_DIM
    qpos_row = qpos0 + lax.broadcasted_iota(I32, (1, TQ), 1)
    searchable = (qpos_row + 1).astype(F32) > kf

    for s in range(n_pairs):
        for src, dst in ((qi_ref, qit_ref), (q_ref, qt_ref)):
            slab = src[0, :, s * LANES:(s + 1) * LANES].astype(F32)
            even = jnp.where(lo_half, slab, 0.0).T
            odd = jnp.where(lo_half, 0.0, slab).T
            dst[s] = jnp.concatenate([even, odd], axis=1).astype(BF16)

    kpos_iota = lax.broadcasted_iota(I32, (CH, TQ), 0)
    qpos = qpos0 + lax.broadcasted_iota(I32, (CH, TQ), 1)

    def score_chunk(c, carry):
        off = pl.multiple_of(c * CH, CH)
        kic = ki_ref[0, pl.ds(off, CH), :]
        acc = jnp.zeros((CH, TQ), F32)
        for s in range(IDX_HEADS // 2):
            d = jnp.dot(kic, qit_ref[s], preferred_element_type=F32)
            for e in range(2):
                w = wit_ref[0, 2 * s + e:2 * s + e + 1, :]
                acc = acc + jnp.maximum(d[:, e * LANES:(e + 1) * LANES], 0.0) * w
        acc = jnp.where(acc == 0.0, 0.0, acc)
        bits = pltpu.bitcast(acc, I32)
        key = jnp.where(bits < 0, bits ^ 0x7FFFFFFF, bits)
        key = jnp.where(off + kpos_iota <= qpos, key, INT_MIN)
        keys_ref[pl.ds(off, CH), :] = key
        return carry

    _chunk_loop(nch, score_chunk, 0)

    def count(pred):
        def body(c, acc):
            off = pl.multiple_of(c * CH, CH)
            hit = jnp.where(pred(keys_ref[pl.ds(off, CH), :], off + kpos_iota), 1.0, 0.0)
            return acc + _fold_rows(hit, jnp.add)
        acc = _chunk_loop(nch, body, jnp.zeros((SUBLANES, TQ), F32))
        return jnp.sum(acc, axis=0, keepdims=True)

    def n_open(cnt):
        return jnp.max(jnp.where(searchable & (cnt != kf), 1.0, 0.0))

    cnt0 = count(lambda k, pos: k >= 0)
    nonneg = cnt0 >= kf
    t0 = jnp.where(nonneg, 0, INT_MIN).astype(I32)
    c0 = jnp.where(nonneg, cnt0, 1e9)

    def bis_cond(st):
        bit, _, _, open_q = st
        return (bit >= 0) & (open_q > 0.0)

    def bis_body(st):
        bit, t, ct, _ = st
        for _ in range(BITS_PER_EXIT_TEST):
            step = jnp.where(bit >= 0, lax.shift_left(jnp.int32(1), jnp.maximum(bit, 0)), 0)
            cand = t + step
            cnt = count(lambda k, pos: k >= cand)
            ok = cnt >= kf
            t = jnp.where(ok, cand, t)
            ct = jnp.where(ok, cnt, ct)
            bit = bit - 1
        return bit, t, ct, n_open(ct)

    _, t_fin, ct_fin, open_fin = lax.while_loop(bis_cond, bis_body, (jnp.int32(30), t0, c0, n_open(c0)))
    t_fin = jnp.where(searchable, t_fin, INT_MIN)
    j_ref[...] = jnp.broadcast_to(jnp.where(searchable, 2 ** 30, -1).astype(I32), j_ref.shape)

    @pl.when(open_fin > 0.0)
    def _():
        tie = searchable & (ct_fin > kf)
        need = kf - count(lambda k, pos: k > t_fin)
        jb = jnp.zeros((1, TQ), I32)
        for bit in range(13, -1, -1):
            cand = jb + (1 << bit)
            f_below = count(lambda k, pos: (k == t_fin) & (pos < cand))
            jb = jnp.where(f_below < need, cand, jb)
        j_ref[...] = jnp.where(tie, jnp.broadcast_to(jb, j_ref.shape), j_ref[...])

    j_fin = j_ref[0:1, :]

    def logits(c, s):
        g = (2 * s) // (ATT_HEADS // ATT_KV_HEADS)
        kg = kd_ref[0, pl.ds(pl.multiple_of(c * CH, CH), CH), g * LANES:(g + 1) * LANES]
        return jnp.dot(kg, qt_ref[s], preferred_element_type=F32)

    n_main = nch // UNROLL

    def softmax_span(c0, n_sub, slot, carry):
        m_run, l_run = carry
        m_loc = [jnp.full((SUBLANES, TQ), -jnp.inf, F32) for _ in range(ATT_HEADS)]
        for u in range(n_sub):
            off = pl.multiple_of((c0 + u) * CH, CH)
            k = keys_ref[pl.ds(off, CH), :]
            sel = (k > t_fin) | ((k == t_fin) & (off + kpos_iota <= j_fin))
            bias = jnp.where(sel, 0.0, NEG_BIG)
            for s in range(n_pairs):
                lt2 = logits(c0 + u, s)
                for e in range(2):
                    lt = lt2[:, e * TQ:(e + 1) * TQ] + bias
                    stage_ref[s, u * CH:(u + 1) * CH, e * TQ:(e + 1) * TQ] = lt
                    m_loc[2 * s + e] = jnp.maximum(m_loc[2 * s + e], _fold_rows(lt, jnp.maximum))
        m_new = [jnp.maximum(m_run[h], jnp.max(m_loc[h], axis=0, keepdims=True)) for h in range(ATT_HEADS)]
        l_new = [jnp.exp2(m_run[h] - m_new[h]) * l_run[h] for h in range(ATT_HEADS)]
        m_ref[slot] = jnp.concatenate(m_new, axis=0)
        for u in range(n_sub):
            off = pl.multiple_of((c0 + u) * CH, CH)
            for s in range(n_pairs):
                ps = []
                for e in range(2):
                    h = 2 * s + e
                    p = jnp.exp2(stage_ref[s, u * CH:(u + 1) * CH, e * TQ:(e + 1) * TQ] - m_new[h])
                    l_new[h] = l_new[h] + _fold_rows(p, jnp.add)
                    ps.append(p.astype(BF16))
                p_ref[s, pl.ds(off, CH), :] = jnp.concatenate(ps, axis=1)
        return tuple(m_new), tuple(l_new)

    carry = (tuple(jnp.full((1, TQ), -jnp.inf, F32) for _ in range(ATT_HEADS)),
             tuple(jnp.zeros((SUBLANES, TQ), F32) for _ in range(ATT_HEADS)))
    carry = lax.fori_loop(0, n_main, lambda i, cr: softmax_span(i * UNROLL, UNROLL, i, cr), carry)
    m_fin, l_fin = lax.fori_loop(n_main * UNROLL, nch,
                                 lambda c, cr: softmax_span(c, 1, c - n_main * (UNROLL - 1), cr), carry)

    def pv_span(off, width, slot):
        m_span = m_ref[slot]
        for s in range(n_pairs):
            g = (2 * s) // (ATT_HEADS // ATT_KV_HEADS)
            vgt = vt_ref[0, g * HEAD_DIM:(g + 1) * HEAD_DIM, pl.ds(off, width)]
            pv = jnp.dot(vgt, p_ref[s, pl.ds(off, width), :], preferred_element_type=F32)
            for e in range(2):
                h = 2 * s + e
                scale = jnp.exp2(m_span[h:h + 1, :] - m_fin[h])
                acc_ref[s, :, e * TQ:(e + 1) * TQ] += scale * pv[:, e * TQ:(e + 1) * TQ]

    acc_ref[...] = jnp.zeros(acc_ref.shape, F32)

    def pv_main(i, carry):
        pv_span(pl.multiple_of(i * (UNROLL * CH), UNROLL * CH), UNROLL * CH, i)
        return carry

    def pv_rest(c, carry):
        pv_span(pl.multiple_of(c * CH, CH), CH, c - n_main * (UNROLL - 1))
        return carry

    lax.fori_loop(0, n_main, pv_main, 0)
    lax.fori_loop(n_main * UNROLL, nch, pv_rest, 0)

    for s in range(n_pairs):
        o_t = jnp.concatenate(
            [acc_ref[s, :, e * TQ:(e + 1) * TQ] / jnp.sum(l_fin[2 * s + e], axis=0, keepdims=True)
             for e in range(2)], axis=0)
        o_ref[0, :, s * LANES:(s + 1) * LANES] = o_t.T.astype(o_ref.dtype)


def _dsa_attention(qi, wit, q, ki2, kd, vt, topk):
    b, s, _ = q.shape
    lk = kd.shape[1]
    assert s % TQ == 0 and lk % CH == 0 and lk >= N_META + s
    qblk = lambda bi, j: (bi, j, 0)
    kblk = lambda bi, j: (bi, 0, 0)
    return pl.pallas_call(
        functools.partial(_attn_kernel, topk),
        grid=(b, s // TQ),
        in_specs=[
            pl.BlockSpec((1, TQ, IDX_HEADS * IDX_DIM), qblk),
            pl.BlockSpec((1, IDX_HEADS, TQ), lambda bi, j: (bi, 0, j)),
            pl.BlockSpec((1, TQ, ATT_W), qblk),
            pl.BlockSpec((1, lk, LANES), kblk),
            pl.BlockSpec((1, lk, 2 * KV_W), kblk),
            pl.BlockSpec((1, KV_W, lk), kblk),
        ],
        out_specs=pl.BlockSpec((1, TQ, ATT_W), qblk),
        out_shape=jax.ShapeDtypeStruct((b, s, ATT_W), BF16),
        scratch_shapes=[
            pltpu.VMEM((lk, TQ), I32),
            pltpu.VMEM((ATT_HEADS // 2, UNROLL * CH, 2 * TQ), F32),
            pltpu.VMEM((ATT_HEADS // 2, lk, 2 * TQ), BF16),
            pltpu.VMEM((IDX_HEADS // 2, LANES, 2 * TQ), BF16),
            pltpu.VMEM((ATT_HEADS // 2, LANES, 2 * TQ), BF16),
            pltpu.VMEM((SUBLANES, TQ), I32),
            pltpu.VMEM((lk // CH, ATT_HEADS, TQ), F32),
            pltpu.VMEM((ATT_HEADS // 2, HEAD_DIM, 2 * TQ), F32),
        ],
        compiler_params=pltpu.CompilerParams(
            dimension_semantics=("arbitrary", "arbitrary"), vmem_limit_bytes=VMEM_LIMIT),
        name="dsa_attn",
    )(qi, wit, q, ki2, kd, vt)


def _merge_ln_kernel(att_ref, cu_ref, halo_ref, gb_ref, ga_ref, gc_ref, h_ref, cw_ref,
                     wa_ref, wc_ref, wo_ref, g_ref, b_ref, o_ref):
    tm = cu_ref.shape[1]
    cu = cu_ref[0]
    halo = halo_ref[0, 0]
    row = lax.broadcasted_iota(I32, (tm, CONV_W), 0)
    prev1 = jnp.where(row == 0, halo[7:8], pltpu.roll(cu, 1, 0))
    prev2 = jnp.where(row == 0, halo[6:7], jnp.where(row == 1, halo[7:8], pltpu.roll(cu, 2, 0)))
    cw = cw_ref[...]
    conv = cw[0:1] * prev2 + cw[1:2] * prev1 + cw[2:3] * cu
    y_conv = jnp.dot((gb_ref[0] * conv).astype(BF16), wc_ref[...], preferred_element_type=F32)
    y_att = jnp.dot(att_ref[0], wa_ref[...], preferred_element_type=F32)
    merged = jax.nn.sigmoid(ga_ref[0]) * y_att + jax.nn.sigmoid(gc_ref[0]) * y_conv
    mix = jnp.dot(merged.astype(BF16), wo_ref[...], preferred_element_type=F32)
    o_ref[0] = _layer_norm(DEEPNORM_ALPHA * h_ref[0] + mix, g_ref[...], b_ref[...])


def _merge_ln(att, cu, halo, gb, ga, gc, h1, conv_w8, wa, wc, wo, g, b):
    bsz, s, _ = cu.shape
    tm = s // halo.shape[1]
    blk = lambda bi, i: (bi, i, 0)
    const = lambda bi, i: (0, 0)
    return pl.pallas_call(
        _merge_ln_kernel,
        grid=(bsz, s // tm),
        in_specs=[
            pl.BlockSpec((1, tm, ATT_W), blk),
            pl.BlockSpec((1, tm, CONV_W), blk),
            pl.BlockSpec((1, 1, 8, CONV_W), lambda bi, i: (bi, i, 0, 0)),
            pl.BlockSpec((1, tm, CONV_W), blk),
            pl.BlockSpec((1, tm, D_MODEL), blk),
            pl.BlockSpec((1, tm, D_MODEL), blk),
            pl.BlockSpec((1, tm, D_MODEL), blk),
            pl.BlockSpec((8, CONV_W), const),
            pl.BlockSpec((ATT_W, D_MODEL), const),
            pl.BlockSpec((CONV_W, D_MODEL), const),
            pl.BlockSpec((D_MODEL, D_MODEL), const),
            pl.BlockSpec((1, D_MODEL), const),
            pl.BlockSpec((1, D_MODEL), const),
        ],
        out_specs=pl.BlockSpec((1, tm, D_MODEL), blk),
        out_shape=jax.ShapeDtypeStruct((bsz, s, D_MODEL), F32),
        compiler_params=pltpu.CompilerParams(
            dimension_semantics=("arbitrary", "arbitrary"), vmem_limit_bytes=VMEM_LIMIT),
        name="merge_ln",
    )(att, cu, halo, gb, ga, gc, h1, conv_w8, wa, wc, wo, g, b)


def _rope_tables(pos):
    half = HEAD_DIM // 2
    inv_freq = ROPE_THETA ** (-jnp.arange(half, dtype=F32) / half)
    ang = pos.astype(F32)[:, None] * inv_freq[None, :]
    cos, sin = jnp.cos(ang), jnp.sin(ang)
    return jnp.tile(cos, (1, LANES // half)), jnp.tile(jnp.concatenate([-sin, sin], axis=1), (1, LANES // HEAD_DIM))


def _split_w_in(w_in):
    widths = (ATT_W, KV_W, KV_W, IDX_HEADS * IDX_DIM, IDX_DIM, IDX_HEADS,
              CONV_W, CONV_W, CONV_W, D_MODEL, D_MODEL)
    offs = np.cumsum(widths)[:-1].tolist()
    wq, wk, wv, wqi, wki, wwi, wu, wgb, wgc, wga, wgv = jnp.split(w_in, offs, axis=1)
    dup = lambda w: jnp.concatenate([w[:, :HEAD_DIM], w[:, :HEAD_DIM], w[:, HEAD_DIM:], w[:, HEAD_DIM:]], axis=1)
    w_a = jnp.concatenate(
        [wq, wqi, dup(wk), wki, wki, dup(wv), wwi, jnp.zeros((D_MODEL, LANES - IDX_HEADS), w_in.dtype)], axis=1)
    w_b = jnp.concatenate([wu, wgc, wgb, wga, wgv], axis=1)
    return w_a.astype(BF16), w_b.astype(BF16)


def kernel(x, meta_tokens, ffn1_w_gate, ffn1_w_up, ffn1_w_down, ln1_g, ln1_b, w_in, conv_w, w_att_out, w_conv_out, w_o, ln2_g, ln2_b, ffn2_w_gate, ffn2_w_up, ffn2_w_down, ln3_g, ln3_b):
    bsz, seq, _ = x.shape
    l_tot = N_META + seq
    topk = min(TOPK_MAX, l_tot // 4)
    lk = ((l_tot + CH - 1) // CH) * CH
    lyr = 0
    bf = lambda w: w.astype(BF16)
    vec = lambda v: v[lyr][None, :]

    w1 = (bf(ffn1_w_gate[lyr]), bf(ffn1_w_up[lyr]), bf(ffn1_w_down[lyr]), vec(ln1_g), vec(ln1_b))
    w_a, w_b = _split_w_in(w_in[lyr])
    pos = jnp.arange(l_tot, dtype=I32)
    cos, sin = _rope_tables(pos)

    hm = _ffn_ln(meta_tokens.astype(F32), *w1)
    _, _, kd_m, ki_m, vd_m, _ = _proj_rope(hm, w_a, cos[:N_META], sin[:N_META], N_META)
    cu_m = _proj_gate(hm, w_b)[0]

    h1 = _ffn_ln(x.reshape(bsz * seq, D_MODEL), *w1)
    q, qi, kd, ki2, vd, wi = _proj_rope(h1, w_a, cos[N_META:], sin[N_META:], seq)
    cu, gb, ga, gc = _proj_gate(h1, w_b)

    def with_meta(meta_rows, real):
        wdt = real.shape[-1]
        return jnp.concatenate(
            [jnp.broadcast_to(meta_rows[None], (bsz, N_META, wdt)), real.reshape(bsz, seq, wdt),
             jnp.zeros((bsz, lk - l_tot, wdt), real.dtype)], axis=1)

    per_seq = lambda a: a.reshape(bsz, seq, a.shape[-1])
    undup = lambda a: jnp.concatenate([a[:, :HEAD_DIM], a[:, 2 * HEAD_DIM:3 * HEAD_DIM]], axis=1)
    vt = jnp.swapaxes(with_meta(undup(vd_m), undup(vd)), 1, 2)
    wit = jnp.swapaxes(per_seq(wi)[:, :, :IDX_HEADS], 1, 2)
    att = _dsa_attention(per_seq(qi), wit, per_seq(q), with_meta(ki_m, ki2), with_meta(kd_m, kd), vt, topk)

    tm = min(seq, 512)
    cu3 = per_seq(cu)
    tails = cu3.reshape(bsz, seq // tm, tm, CONV_W)[:, :-1, tm - 8:, :]
    halo = jnp.concatenate([jnp.broadcast_to(cu_m[None, None, N_META - 8:], (bsz, 1, 8, CONV_W)), tails], axis=1)
    conv_w8 = jnp.concatenate([conv_w[lyr].astype(F32), jnp.zeros((8 - CONV_K, CONV_W), F32)], axis=0)
    h2 = _merge_ln(att, cu3, halo, per_seq(gb), per_seq(ga), per_seq(gc), per_seq(h1), conv_w8,
                   bf(w_att_out[lyr]), bf(w_conv_out[lyr]), bf(w_o[lyr]), vec(ln2_g), vec(ln2_b))

    h3 = _ffn_ln(h2.reshape(bsz * seq, D_MODEL), bf(ffn2_w_gate[lyr]), bf(ffn2_w_up[lyr]), bf(ffn2_w_down[lyr]),
                 vec(ln3_g), vec(ln3_b))
    return h3.reshape(bsz, seq, D_MODEL)
```

```python
import functools

import numpy as np
import jax
import jax.numpy as jnp
from jax import lax
from jax.experimental import pallas as pl
from jax.experimental.pallas import tpu as pltpu

F32 = jnp.float32
BF16 = jnp.bfloat16
I32 = jnp.int32

D_MODEL = 1024
N_META = 16
ATT_HEADS = 8
ATT_KV_HEADS = 2
HEAD_DIM = 64
ATT_W = ATT_HEADS * HEAD_DIM
KV_W = ATT_KV_HEADS * HEAD_DIM
IDX_HEADS = 8
IDX_DIM = 64
TOPK_MAX = 256
ROPE_THETA = 10000.0
CONV_W = D_MODEL // 2
CONV_K = 3
D_FF = 2816
LN_EPS = 1e-5
DEPTH = 1
DEEPNORM_ALPHA = (2.0 * DEPTH) ** 0.25

LANES = 128
SUBLANES = 8
TQ = 128
CH = 256
VT_ROWS = 80
UNROLL = 4
BITS_PER_EXIT_TEST = 4
KEY_STEP_PERIOD = 4
MAX_SEARCH_STEPS = 160
INT_MAX = 2 ** 31 - 1
LOG2E = 1.4426950408889634
INT_MIN = -(2 ** 31)
NEG_BIG = -1e30
VMEM_LIMIT = 52 * 1024 * 1024


def _layer_norm(y, g, b):
    mu = jnp.mean(y, axis=-1, keepdims=True)
    yc = y - mu
    var = jnp.mean(yc * yc, axis=-1, keepdims=True)
    return yc * lax.rsqrt(var + LN_EPS) * g + b


def _ffn_ln_kernel(x_ref, wg_ref, wu_ref, wd_ref, g_ref, b_ref, o_ref, acc_ref, xb_ref):
    f = pl.program_id(1)

    @pl.when(f == 0)
    def _():
        acc_ref[...] = jnp.zeros_like(acc_ref)
        xb_ref[...] = x_ref[...].astype(BF16)

    xb = xb_ref[...]
    gt = jnp.dot(xb, wg_ref[...], preferred_element_type=F32)
    up = jnp.dot(xb, wu_ref[...], preferred_element_type=F32)
    a = (gt * jax.nn.sigmoid(gt)) * up
    acc_ref[...] += jnp.dot(a.astype(BF16), wd_ref[...], preferred_element_type=F32)

    @pl.when(f == pl.num_programs(1) - 1)
    def _():
        y = DEEPNORM_ALPHA * x_ref[...] + 0.5 * acc_ref[...]
        o_ref[...] = _layer_norm(y, g_ref[...], b_ref[...])


def _ffn_ln(x2d, wg, wu, wd, g, b):
    n = x2d.shape[0]
    tm = min(n, 1024)
    tf = 256
    assert n % tm == 0 and D_FF % tf == 0
    return pl.pallas_call(
        _ffn_ln_kernel,
        grid=(n // tm, D_FF // tf),
        in_specs=[
            pl.BlockSpec((tm, D_MODEL), lambda i, f: (i, 0)),
            pl.BlockSpec((D_MODEL, tf), lambda i, f: (0, f)),
            pl.BlockSpec((D_MODEL, tf), lambda i, f: (0, f)),
            pl.BlockSpec((tf, D_MODEL), lambda i, f: (f, 0)),
            pl.BlockSpec((1, D_MODEL), lambda i, f: (0, 0)),
            pl.BlockSpec((1, D_MODEL), lambda i, f: (0, 0)),
        ],
        out_specs=pl.BlockSpec((tm, D_MODEL), lambda i, f: (i, 0)),
        out_shape=jax.ShapeDtypeStruct((n, D_MODEL), F32),
        scratch_shapes=[pltpu.VMEM((tm, D_MODEL), F32), pltpu.VMEM((tm, D_MODEL), BF16)],
        compiler_params=pltpu.CompilerParams(
            dimension_semantics=("arbitrary", "arbitrary"), vmem_limit_bytes=VMEM_LIMIT),
        name="ffn_ln",
    )(x2d, wg, wu, wd, g, b)


_PROJ_A_W = ATT_W + IDX_HEADS * IDX_DIM + 2 * KV_W + LANES + 2 * KV_W + LANES


def _proj_rope_kernel(x_ref, w_ref, cos_ref, sin_ref, q_ref, qi_ref, kd_ref, ki_ref, vd_ref, wi_ref):
    tm = x_ref.shape[0]
    xb = x_ref[...].astype(BF16)
    cos = cos_ref[...]
    sin = sin_ref[...]
    lane = lax.broadcasted_iota(I32, (tm, LANES), 1)
    first_half = (lane & (HEAD_DIM // 2)) == 0

    def rope(s):
        partner = jnp.where(first_half, pltpu.roll(s, LANES - HEAD_DIM // 2, 1),
                            pltpu.roll(s, HEAD_DIM // 2, 1))
        return s * cos + partner * sin

    def dot_cols(c0, width):
        return jnp.dot(xb, w_ref[:, c0:c0 + width], preferred_element_type=F32)

    outs = ((q_ref, ATT_W, HEAD_DIM ** -0.5 * LOG2E), (qi_ref, IDX_HEADS * IDX_DIM, 1.0), (kd_ref, 2 * KV_W, 1.0))
    c0 = 0
    for ref, width, scale in outs:
        for j in range(0, width, 2 * LANES):
            p = dot_cols(c0 + j, 2 * LANES)
            for s in range(2):
                r = rope(p[:, s * LANES:(s + 1) * LANES])
                if scale != 1.0:
                    r = r * scale
                ref[:, j + s * LANES:j + (s + 1) * LANES] = r.astype(ref.dtype)
        c0 += width
    p = dot_cols(c0, LANES)
    ki_ref[...] = rope(p).astype(ki_ref.dtype)
    c0 += LANES
    vd_ref[...] = dot_cols(c0, 2 * KV_W).astype(vd_ref.dtype)
    c0 += 2 * KV_W
    wi_ref[...] = dot_cols(c0, LANES) * ((IDX_HEADS * IDX_DIM) ** -0.5)


def _proj_rope(h2d, w_a, cos, sin, rows_per_seq):
    n = h2d.shape[0]
    tm = min(rows_per_seq, 512)
    tiles_per_seq = rows_per_seq // tm
    assert n % tm == 0 and rows_per_seq % tm == 0
    row = lambda i: (i, 0)
    tab = lambda i: (i % tiles_per_seq, 0)
    widths = (ATT_W, IDX_HEADS * IDX_DIM, 2 * KV_W, LANES, 2 * KV_W, LANES)
    dtypes = (BF16, BF16, BF16, BF16, BF16, F32)
    return pl.pallas_call(
        _proj_rope_kernel,
        grid=(n // tm,),
        in_specs=[
            pl.BlockSpec((tm, D_MODEL), row),
            pl.BlockSpec((D_MODEL, _PROJ_A_W), lambda i: (0, 0)),
            pl.BlockSpec((tm, LANES), tab),
            pl.BlockSpec((tm, LANES), tab),
        ],
        out_specs=[pl.BlockSpec((tm, w), row) for w in widths],
        out_shape=[jax.ShapeDtypeStruct((n, w), dt) for w, dt in zip(widths, dtypes)],
        compiler_params=pltpu.CompilerParams(
            dimension_semantics=("arbitrary",), vmem_limit_bytes=VMEM_LIMIT),
        name="proj_rope",
    )(h2d, w_a, cos, sin)


_PROJ_B_W = 3 * CONV_W + 2 * D_MODEL


def _proj_gate_kernel(x_ref, w_ref, cu_ref, gb_ref, ga_ref, gc_ref):
    xb = x_ref[...].astype(BF16)

    def dot_cols(c0, width):
        return jnp.dot(xb, w_ref[:, c0:c0 + width], preferred_element_type=F32)

    cu_ref[...] = dot_cols(0, CONV_W) * dot_cols(CONV_W, CONV_W)
    gb_ref[...] = dot_cols(2 * CONV_W, CONV_W)
    ga_ref[...] = dot_cols(3 * CONV_W, D_MODEL)
    gc_ref[...] = dot_cols(3 * CONV_W + D_MODEL, D_MODEL)


def _proj_gate(h2d, w_b):
    n = h2d.shape[0]
    tm = min(n, 512)
    assert n % tm == 0
    row = lambda i: (i, 0)
    widths = (CONV_W, CONV_W, D_MODEL, D_MODEL)
    return pl.pallas_call(
        _proj_gate_kernel,
        grid=(n // tm,),
        in_specs=[pl.BlockSpec((tm, D_MODEL), row), pl.BlockSpec((D_MODEL, _PROJ_B_W), lambda i: (0, 0))],
        out_specs=[pl.BlockSpec((tm, w), row) for w in widths],
        out_shape=[jax.ShapeDtypeStruct((n, w), F32) for w in widths],
        compiler_params=pltpu.CompilerParams(
            dimension_semantics=("arbitrary",), vmem_limit_bytes=VMEM_LIMIT),
        name="proj_gate",
    )(h2d, w_b)


def _fold_rows(x, op):
    parts = [x[i:i + SUBLANES] for i in range(0, x.shape[0], SUBLANES)]
    while len(parts) > 1:
        parts = [op(parts[i], parts[i + 1]) for i in range(0, len(parts), 2)]
    return parts[0]


def _key_to_val(key):
    return pltpu.bitcast(jnp.where(key < 0, key ^ 0x7FFFFFFF, key), F32)


def _val_to_key(val):
    bits = pltpu.bitcast(jnp.where(val == 0.0, 0.0, val), I32)
    return jnp.where(bits < 0, bits ^ 0x7FFFFFFF, bits)


def _chunk_loop(nch, body, init):
    n_main = nch // UNROLL

    def main(i, carry):
        for u in range(UNROLL):
            carry = body(i * UNROLL + u, carry)
        return carry

    carry = lax.fori_loop(0, n_main, main, init)
    return lax.fori_loop(n_main * UNROLL, nch, body, carry)


def _attn_kernel(topk, qi_ref, wit_ref, q_ref, ki_ref, kd_ref, vt_ref, o_ref,
                 keys_ref, kmax_ref, stage_ref, p_ref, qit_ref, qt_ref, j_ref, m_ref, acc_ref):
    jblk = pl.program_id(1)
    qpos0 = N_META + jblk * TQ
    nch = (qpos0 + TQ - 1) // CH + 1
    kf = float(topk)
    n_pairs = ATT_HEADS // 2

    lane = lax.broadcasted_iota(I32, (TQ, LANES), 1)
    lo_half = lane < HEAD_DIM
    qpos_row = qpos0 + lax.broadcasted_iota(I32, (1, TQ), 1)
    searchable = (qpos_row + 1).astype(F32) > kf

    for s in range(n_pairs):
        for src, dst in ((qi_ref, qit_ref), (q_ref, qt_ref)):
            slab = src[0, :, s * LANES:(s + 1) * LANES].astype(F32)
            even = jnp.where(lo_half, slab, 0.0).T
            odd = jnp.where(lo_half, 0.0, slab).T
            dst[s] = jnp.concatenate([even, odd], axis=1).astype(BF16)

    kpos_iota = lax.broadcasted_iota(I32, (CH, TQ), 0)
    qpos = qpos0 + lax.broadcasted_iota(I32, (CH, TQ), 1)

    def score_chunk(c, carry):
        off = pl.multiple_of(c * CH, CH)
        kic = ki_ref[0, pl.ds(off, CH), :]
        acc = jnp.zeros((CH, TQ), F32)
        for s in range(IDX_HEADS // 2):
            d = jnp.dot(kic, qit_ref[s], preferred_element_type=F32)
            for e in range(2):
                w = wit_ref[0, 2 * s + e:2 * s + e + 1, :]
                acc = acc + jnp.maximum(d[:, e * LANES:(e + 1) * LANES], 0.0) * w
        key = jnp.where(off + kpos_iota <= qpos, _val_to_key(acc), INT_MIN)
        keys_ref[pl.ds(off, CH), :] = key
        kmax_ref[...] = jnp.maximum(kmax_ref[...], key)
        return carry

    kmax_ref[...] = jnp.full(kmax_ref.shape, INT_MIN, I32)
    _chunk_loop(nch, score_chunk, 0)

    def count(pred):
        def body(c, acc):
            off = pl.multiple_of(c * CH, CH)
            hit = jnp.where(pred(keys_ref[pl.ds(off, CH), :], off + kpos_iota), 1.0, 0.0)
            return acc + _fold_rows(hit, jnp.add)
        acc = _chunk_loop(nch, body, jnp.zeros((SUBLANES, TQ), F32))
        return jnp.sum(acc, axis=0, keepdims=True)

    kmax = kmax_ref[...]
    lo0 = jnp.min(_fold_rows(kmax, jnp.minimum), axis=0, keepdims=True)
    hi0 = jnp.minimum(jnp.max(_fold_rows(kmax, jnp.maximum), axis=0, keepdims=True), INT_MAX - 1) + 1

    def live(lo, hi, c_lo):
        return searchable & (c_lo != kf) & (hi - lo != 1)

    def n_live(lo, hi, c_lo):
        return jnp.max(jnp.where(live(lo, hi, c_lo), 1.0, 0.0))

    def pick(it, lo, hi, c_lo):
        top = hi - 1
        mid_val = _val_to_key(0.5 * _key_to_val(lo) + 0.5 * _key_to_val(top))
        mid_key = lo + lax.shift_right_logical(hi - lo, 1)
        cand = jnp.where(it % KEY_STEP_PERIOD == KEY_STEP_PERIOD - 1, mid_key, mid_val)
        cand = jnp.where((lo < 0) & (top >= 0), 0, cand)
        cand = jnp.where((lo == 0) & (top >= 1), 1, cand)
        cand = jnp.minimum(jnp.maximum(cand, lo + 1), top)
        return jnp.where(live(lo, hi, c_lo), cand, lo)

    def bis_cond(st):
        it, _, _, _, open_q = st
        return (it < MAX_SEARCH_STEPS) & (open_q > 0.0)

    def bis_body(st):
        it, lo, hi, c_lo, _ = st
        for _ in range(BITS_PER_EXIT_TEST):
            cand = pick(it, lo, hi, c_lo)
            cnt = count(lambda k, pos: k >= cand)
            ok = cnt >= kf
            lo = jnp.where(ok, cand, lo)
            c_lo = jnp.where(ok, cnt, c_lo)
            hi = jnp.where(ok, hi, cand)
            it = it + 1
        return it, lo, hi, c_lo, n_live(lo, hi, c_lo)

    c_unknown = jnp.full((1, TQ), 1e9, F32)
    _, t_fin, _, ct_fin, _ = lax.while_loop(
        bis_cond, bis_body, (jnp.int32(0), lo0, hi0, c_unknown, n_live(lo0, hi0, c_unknown)))
    t_fin = jnp.where(searchable, t_fin, INT_MIN)
    j_ref[...] = jnp.broadcast_to(jnp.where(searchable, 2 ** 30, -1).astype(I32), j_ref.shape)
    open_fin = jnp.max(jnp.where(searchable & (ct_fin != kf), 1.0, 0.0))

    @pl.when(open_fin > 0.0)
    def _():
        tie = searchable & (ct_fin > kf)
        need = kf - count(lambda k, pos: k > t_fin)
        jb = jnp.zeros((1, TQ), I32)
        for bit in range(13, -1, -1):
            cand = jb + (1 << bit)
            f_below = count(lambda k, pos: (k == t_fin) & (pos < cand))
            jb = jnp.where(f_below < need, cand, jb)
        j_ref[...] = jnp.where(tie, jnp.broadcast_to(jb, j_ref.shape), j_ref[...])

    j_fin = j_ref[0:1, :]

    def logits(c, s):
        g = (2 * s) // (ATT_HEADS // ATT_KV_HEADS)
        kg = kd_ref[0, pl.ds(pl.multiple_of(c * CH, CH), CH), g * LANES:(g + 1) * LANES]
        return jnp.dot(kg, qt_ref[s], preferred_element_type=F32)

    n_main = nch // UNROLL

    def span_step(c_new, c_old, n_sub, m_old, new, old):
        m_loc = [jnp.full((SUBLANES, TQ), -jnp.inf, F32) for _ in range(ATT_HEADS)]
        for u in range(n_sub):
            rows = slice(u * CH, (u + 1) * CH)
            if new:
                off_new = pl.multiple_of((c_new + u) * CH, CH)
                k = keys_ref[pl.ds(off_new, CH), :]
                sel = (k > t_fin) | ((k == t_fin) & (off_new + kpos_iota <= j_fin))
                bias = jnp.where(sel, 0.0, NEG_BIG)
            for s in range(n_pairs):
                if old:
                    ps = [jnp.exp2(stage_ref[s, rows, e * TQ:(e + 1) * TQ] - m_old[2 * s + e]).astype(BF16)
                          for e in range(2)]
                    p_ref[s, pl.ds(pl.multiple_of((c_old + u) * CH, CH), CH), :] = jnp.concatenate(ps, axis=1)
                if new:
                    lt2 = logits(c_new + u, s)
                    for e in range(2):
                        h = 2 * s + e
                        lt = lt2[:, e * TQ:(e + 1) * TQ] + bias
                        stage_ref[s, rows, e * TQ:(e + 1) * TQ] = lt
                        m_loc[h] = jnp.maximum(m_loc[h], _fold_rows(lt, jnp.maximum))
        return m_loc

    def advance_max(m_run, m_loc, slot):
        m_new = tuple(jnp.maximum(m_run[h], jnp.max(m_loc[h], axis=0, keepdims=True)) for h in range(ATT_HEADS))
        m_ref[slot] = jnp.concatenate(m_new, axis=0)
        return m_new

    def first_span(i, m_run):
        return advance_max(m_run, span_step(0, 0, UNROLL, None, new=True, old=False), 0)

    def middle_span(i, m_run):
        return advance_max(m_run, span_step(i * UNROLL, (i - 1) * UNROLL, UNROLL, m_run, new=True, old=True), i)

    def last_span(i, m_run):
        span_step(0, (n_main - 1) * UNROLL, UNROLL, m_run, new=False, old=True)
        return m_run

    def single_chunk(c, m_run):
        m_new = advance_max(m_run, span_step(c, c, 1, None, new=True, old=False), c - n_main * (UNROLL - 1))
        span_step(c, c, 1, m_new, new=False, old=True)
        return m_new

    m_run = tuple(jnp.full((1, TQ), -jnp.inf, F32) for _ in range(ATT_HEADS))
    has_main = jnp.minimum(n_main, 1)
    m_run = lax.fori_loop(0, has_main, first_span, m_run)
    m_run = lax.fori_loop(1, n_main, middle_span, m_run)
    m_run = lax.fori_loop(0, has_main, last_span, m_run)
    m_fin = lax.fori_loop(n_main * UNROLL, nch, single_chunk, m_run)

    def pv_span(off, width, slot):
        m_span = m_ref[slot]
        for s in range(n_pairs):
            g = (2 * s) // (ATT_HEADS // ATT_KV_HEADS)
            vgt = vt_ref[0, g * VT_ROWS:(g + 1) * VT_ROWS, pl.ds(off, width)]
            pv = jnp.dot(vgt, p_ref[s, pl.ds(off, width), :], preferred_element_type=F32)
            for e in range(2):
                h = 2 * s + e
                scale = jnp.exp2(m_span[h:h + 1, :] - m_fin[h])
                acc_ref[s, :, e * TQ:(e + 1) * TQ] += scale * pv[:, e * TQ:(e + 1) * TQ]

    acc_ref[...] = jnp.zeros(acc_ref.shape, F32)

    def pv_main(i, carry):
        pv_span(pl.multiple_of(i * (UNROLL * CH), UNROLL * CH), UNROLL * CH, i)
        return carry

    def pv_rest(c, carry):
        pv_span(pl.multiple_of(c * CH, CH), CH, c - n_main * (UNROLL - 1))
        return carry

    lax.fori_loop(0, n_main, pv_main, 0)
    lax.fori_loop(n_main * UNROLL, nch, pv_rest, 0)

    for s in range(n_pairs):
        o_t = jnp.concatenate(
            [acc_ref[s, :HEAD_DIM, e * TQ:(e + 1) * TQ] / acc_ref[s, HEAD_DIM:HEAD_DIM + 1, e * TQ:(e + 1) * TQ]
             for e in range(2)], axis=0)
        o_ref[0, :, s * LANES:(s + 1) * LANES] = o_t.T.astype(o_ref.dtype)


def _dsa_attention(qi, wit, q, ki2, kd, vt, topk):
    b, s, _ = q.shape
    lk = kd.shape[1]
    assert s % TQ == 0 and lk % CH == 0 and lk >= N_META + s and topk <= CH
    qblk = lambda bi, j: (bi, j, 0)
    kblk = lambda bi, j: (bi, 0, 0)
    return pl.pallas_call(
        functools.partial(_attn_kernel, topk),
        grid=(b, s // TQ),
        in_specs=[
            pl.BlockSpec((1, TQ, IDX_HEADS * IDX_DIM), qblk),
            pl.BlockSpec((1, IDX_HEADS, TQ), lambda bi, j: (bi, 0, j)),
            pl.BlockSpec((1, TQ, ATT_W), qblk),
            pl.BlockSpec((1, lk, LANES), kblk),
            pl.BlockSpec((1, lk, 2 * KV_W), kblk),
            pl.BlockSpec((1, ATT_KV_HEADS * VT_ROWS, lk), kblk),
        ],
        out_specs=pl.BlockSpec((1, TQ, ATT_W), qblk),
        out_shape=jax.ShapeDtypeStruct((b, s, ATT_W), BF16),
        scratch_shapes=[
            pltpu.VMEM((lk, TQ), I32),
            pltpu.VMEM((CH, TQ), I32),
            pltpu.VMEM((ATT_HEADS // 2, UNROLL * CH, 2 * TQ), F32),
            pltpu.VMEM((ATT_HEADS // 2, lk, 2 * TQ), BF16),
            pltpu.VMEM((IDX_HEADS // 2, LANES, 2 * TQ), BF16),
            pltpu.VMEM((ATT_HEADS // 2, LANES, 2 * TQ), BF16),
            pltpu.VMEM((SUBLANES, TQ), I32),
            pltpu.VMEM((lk // CH, ATT_HEADS, TQ), F32),
            pltpu.VMEM((ATT_HEADS // 2, VT_ROWS, 2 * TQ), F32),
        ],
        compiler_params=pltpu.CompilerParams(
            dimension_semantics=("arbitrary", "arbitrary"), vmem_limit_bytes=VMEM_LIMIT),
        name="dsa_attn",
    )(qi, wit, q, ki2, kd, vt)


def _merge_ln_kernel(att_ref, cu_ref, halo_ref, gb_ref, ga_ref, gc_ref, h_ref, cw_ref,
                     wa_ref, wc_ref, wo_ref, g_ref, b_ref, o_ref):
    tm = cu_ref.shape[1]
    cu = cu_ref[0]
    halo = halo_ref[0, 0]
    row = lax.broadcasted_iota(I32, (tm, CONV_W), 0)
    prev1 = jnp.where(row == 0, halo[7:8], pltpu.roll(cu, 1, 0))
    prev2 = jnp.where(row == 0, halo[6:7], jnp.where(row == 1, halo[7:8], pltpu.roll(cu, 2, 0)))
    cw = cw_ref[...]
    conv = cw[0:1] * prev2 + cw[1:2] * prev1 + cw[2:3] * cu
    y_conv = jnp.dot((gb_ref[0] * conv).astype(BF16), wc_ref[...], preferred_element_type=F32)
    y_att = jnp.dot(att_ref[0], wa_ref[...], preferred_element_type=F32)
    merged = jax.nn.sigmoid(ga_ref[0]) * y_att + jax.nn.sigmoid(gc_ref[0]) * y_conv
    mix = jnp.dot(merged.astype(BF16), wo_ref[...], preferred_element_type=F32)
    o_ref[0] = _layer_norm(DEEPNORM_ALPHA * h_ref[0] + mix, g_ref[...], b_ref[...])


def _merge_ln(att, cu, halo, gb, ga, gc, h1, conv_w8, wa, wc, wo, g, b):
    bsz, s, _ = cu.shape
    tm = s // halo.shape[1]
    blk = lambda bi, i: (bi, i, 0)
    const = lambda bi, i: (0, 0)
    return pl.pallas_call(
        _merge_ln_kernel,
        grid=(bsz, s // tm),
        in_specs=[
            pl.BlockSpec((1, tm, ATT_W), blk),
            pl.BlockSpec((1, tm, CONV_W), blk),
            pl.BlockSpec((1, 1, 8, CONV_W), lambda bi, i: (bi, i, 0, 0)),
            pl.BlockSpec((1, tm, CONV_W), blk),
            pl.BlockSpec((1, tm, D_MODEL), blk),
            pl.BlockSpec((1, tm, D_MODEL), blk),
            pl.BlockSpec((1, tm, D_MODEL), blk),
            pl.BlockSpec((8, CONV_W), const),
            pl.BlockSpec((ATT_W, D_MODEL), const),
            pl.BlockSpec((CONV_W, D_MODEL), const),
            pl.BlockSpec((D_MODEL, D_MODEL), const),
            pl.BlockSpec((1, D_MODEL), const),
            pl.BlockSpec((1, D_MODEL), const),
        ],
        out_specs=pl.BlockSpec((1, tm, D_MODEL), blk),
        out_shape=jax.ShapeDtypeStruct((bsz, s, D_MODEL), F32),
        compiler_params=pltpu.CompilerParams(
            dimension_semantics=("arbitrary", "arbitrary"), vmem_limit_bytes=VMEM_LIMIT),
        name="merge_ln",
    )(att, cu, halo, gb, ga, gc, h1, conv_w8, wa, wc, wo, g, b)


def _rope_tables(pos):
    half = HEAD_DIM // 2
    inv_freq = ROPE_THETA ** (-jnp.arange(half, dtype=F32) / half)
    ang = pos.astype(F32)[:, None] * inv_freq[None, :]
    cos, sin = jnp.cos(ang), jnp.sin(ang)
    return jnp.tile(cos, (1, LANES // half)), jnp.tile(jnp.concatenate([-sin, sin], axis=1), (1, LANES // HEAD_DIM))


def _split_w_in(w_in):
    widths = (ATT_W, KV_W, KV_W, IDX_HEADS * IDX_DIM, IDX_DIM, IDX_HEADS,
              CONV_W, CONV_W, CONV_W, D_MODEL, D_MODEL)
    offs = np.cumsum(widths)[:-1].tolist()
    wq, wk, wv, wqi, wki, wwi, wu, wgb, wgc, wga, wgv = jnp.split(w_in, offs, axis=1)
    dup = lambda w: jnp.concatenate([w[:, :HEAD_DIM], w[:, :HEAD_DIM], w[:, HEAD_DIM:], w[:, HEAD_DIM:]], axis=1)
    w_a = jnp.concatenate(
        [wq, wqi, dup(wk), wki, wki, dup(wv), wwi, jnp.zeros((D_MODEL, LANES - IDX_HEADS), w_in.dtype)], axis=1)
    w_b = jnp.concatenate([wu, wgc, wgb, wga, wgv], axis=1)
    return w_a.astype(BF16), w_b.astype(BF16)


def kernel(x, meta_tokens, ffn1_w_gate, ffn1_w_up, ffn1_w_down, ln1_g, ln1_b, w_in, conv_w, w_att_out, w_conv_out, w_o, ln2_g, ln2_b, ffn2_w_gate, ffn2_w_up, ffn2_w_down, ln3_g, ln3_b):
    bsz, seq, _ = x.shape
    l_tot = N_META + seq
    topk = min(TOPK_MAX, l_tot // 4)
    lk = ((l_tot + CH - 1) // CH) * CH
    lyr = 0
    bf = lambda w: w.astype(BF16)
    vec = lambda v: v[lyr][None, :]

    w1 = (bf(ffn1_w_gate[lyr]), bf(ffn1_w_up[lyr]), bf(ffn1_w_down[lyr]), vec(ln1_g), vec(ln1_b))
    w_a, w_b = _split_w_in(w_in[lyr])
    pos = jnp.arange(l_tot, dtype=I32)
    cos, sin = _rope_tables(pos)

    hm = _ffn_ln(meta_tokens.astype(F32), *w1)
    _, _, kd_m, ki_m, vd_m, _ = _proj_rope(hm, w_a, cos[:N_META], sin[:N_META], N_META)
    cu_m = _proj_gate(hm, w_b)[0]

    h1 = _ffn_ln(x.reshape(bsz * seq, D_MODEL), *w1)
    q, qi, kd, ki2, vd, wi = _proj_rope(h1, w_a, cos[N_META:], sin[N_META:], seq)
    cu, gb, ga, gc = _proj_gate(h1, w_b)

    def with_meta(meta_rows, real):
        wdt = real.shape[-1]
        return jnp.concatenate(
            [jnp.broadcast_to(meta_rows[None], (bsz, N_META, wdt)), real.reshape(bsz, seq, wdt),
             jnp.zeros((bsz, lk - l_tot, wdt), real.dtype)], axis=1)

    per_seq = lambda a: a.reshape(bsz, seq, a.shape[-1])
    v_keys = with_meta(vd_m, vd)
    ones_pad = jnp.zeros((bsz, lk, VT_ROWS - HEAD_DIM), BF16).at[:, :, 0].set(1.0)
    vt = jnp.swapaxes(jnp.concatenate(
        [v_keys[:, :, :HEAD_DIM], ones_pad, v_keys[:, :, 2 * HEAD_DIM:3 * HEAD_DIM], ones_pad], axis=2), 1, 2)
    wit = jnp.swapaxes(per_seq(wi)[:, :, :IDX_HEADS], 1, 2)
    att = _dsa_attention(per_seq(qi), wit, per_seq(q), with_meta(ki_m, ki2), with_meta(kd_m, kd), vt, topk)

    tm = min(seq, 512)
    cu3 = per_seq(cu)
    tails = cu3.reshape(bsz, seq // tm, tm, CONV_W)[:, :-1, tm - 8:, :]
    halo = jnp.concatenate([jnp.broadcast_to(cu_m[None, None, N_META - 8:], (bsz, 1, 8, CONV_W)), tails], axis=1)
    conv_w8 = jnp.concatenate([conv_w[lyr].astype(F32), jnp.zeros((8 - CONV_K, CONV_W), F32)], axis=0)
    h2 = _merge_ln(att, cu3, halo, per_seq(gb), per_seq(ga), per_seq(gc), per_seq(h1), conv_w8,
                   bf(w_att_out[lyr]), bf(w_conv_out[lyr]), bf(w_o[lyr]), vec(ln2_g), vec(ln2_b))

    h3 = _ffn_ln(h2.reshape(bsz * seq, D_MODEL), bf(ffn2_w_gate[lyr]), bf(ffn2_w_up[lyr]), bf(ffn2_w_down[lyr]),
                 vec(ln3_g), vec(ln3_b))
    return h3.reshape(bsz, seq, D_MODEL)
```

```python
import functools

import numpy as np
import jax
import jax.numpy as jnp
from jax import lax
from jax.experimental import pallas as pl
from jax.experimental.pallas import tpu as pltpu

F32 = jnp.float32
BF16 = jnp.bfloat16
I32 = jnp.int32

D_MODEL = 1024
N_META = 16
ATT_HEADS = 8
ATT_KV_HEADS = 2
HEAD_DIM = 64
ATT_W = ATT_HEADS * HEAD_DIM
KV_W = ATT_KV_HEADS * HEAD_DIM
IDX_HEADS = 8
IDX_DIM = 64
TOPK_MAX = 256
ROPE_THETA = 10000.0
CONV_W = D_MODEL // 2
CONV_K = 3
D_FF = 2816
LN_EPS = 1e-5
DEPTH = 1
DEEPNORM_ALPHA = (2.0 * DEPTH) ** 0.25

LANES = 128
SUBLANES = 8
TQ = 128
CH = 256
VT_ROWS = 80
UNROLL = 4
STEPS_PER_EXIT_TEST = 2
UNTESTED_STEPS = 12
KEY_STEP_PERIOD = 4
MAX_SEARCH_STEPS = 160
INT_MAX = 2 ** 31 - 1
LOG2E = 1.4426950408889634
INT_MIN = -(2 ** 31)
NEG_BIG = -1e30
VMEM_LIMIT = 52 * 1024 * 1024


def _layer_norm(y, g, b):
    mu = jnp.mean(y, axis=-1, keepdims=True)
    yc = y - mu
    var = jnp.mean(yc * yc, axis=-1, keepdims=True)
    return yc * lax.rsqrt(var + LN_EPS) * g + b


def _ffn_ln_kernel(x_ref, wg_ref, wu_ref, wd_ref, g_ref, b_ref, o_ref, acc_ref, xb_ref):
    f = pl.program_id(1)

    @pl.when(f == 0)
    def _():
        acc_ref[...] = jnp.zeros_like(acc_ref)
        xb_ref[...] = x_ref[...].astype(BF16)

    xb = xb_ref[...]
    gt = jnp.dot(xb, wg_ref[...], preferred_element_type=F32)
    up = jnp.dot(xb, wu_ref[...], preferred_element_type=F32)
    a = (gt * jax.nn.sigmoid(gt)) * up
    acc_ref[...] += jnp.dot(a.astype(BF16), wd_ref[...], preferred_element_type=F32)

    @pl.when(f == pl.num_programs(1) - 1)
    def _():
        y = DEEPNORM_ALPHA * x_ref[...] + 0.5 * acc_ref[...]
        o_ref[...] = _layer_norm(y, g_ref[...], b_ref[...])


def _ffn_ln(x2d, wg, wu, wd, g, b):
    n = x2d.shape[0]
    tm = min(n, 1024)
    tf = 256
    assert n % tm == 0 and D_FF % tf == 0
    return pl.pallas_call(
        _ffn_ln_kernel,
        grid=(n // tm, D_FF // tf),
        in_specs=[
            pl.BlockSpec((tm, D_MODEL), lambda i, f: (i, 0)),
            pl.BlockSpec((D_MODEL, tf), lambda i, f: (0, f)),
            pl.BlockSpec((D_MODEL, tf), lambda i, f: (0, f)),
            pl.BlockSpec((tf, D_MODEL), lambda i, f: (f, 0)),
            pl.BlockSpec((1, D_MODEL), lambda i, f: (0, 0)),
            pl.BlockSpec((1, D_MODEL), lambda i, f: (0, 0)),
        ],
        out_specs=pl.BlockSpec((tm, D_MODEL), lambda i, f: (i, 0)),
        out_shape=jax.ShapeDtypeStruct((n, D_MODEL), F32),
        scratch_shapes=[pltpu.VMEM((tm, D_MODEL), F32), pltpu.VMEM((tm, D_MODEL), BF16)],
        compiler_params=pltpu.CompilerParams(
            dimension_semantics=("arbitrary", "arbitrary"), vmem_limit_bytes=VMEM_LIMIT),
        name="ffn_ln",
    )(x2d, wg, wu, wd, g, b)


_PROJ_A_W = ATT_W + IDX_HEADS * IDX_DIM + 2 * KV_W + LANES + 2 * KV_W + LANES


def _proj_rope_kernel(x_ref, w_ref, cos_ref, sin_ref, q_ref, qi_ref, kd_ref, ki_ref, vd_ref, wi_ref):
    tm = x_ref.shape[0]
    xb = x_ref[...].astype(BF16)
    cos = cos_ref[...]
    sin = sin_ref[...]
    lane = lax.broadcasted_iota(I32, (tm, LANES), 1)
    first_half = (lane & (HEAD_DIM // 2)) == 0

    def rope(s):
        partner = jnp.where(first_half, pltpu.roll(s, LANES - HEAD_DIM // 2, 1),
                            pltpu.roll(s, HEAD_DIM // 2, 1))
        return s * cos + partner * sin

    def dot_cols(c0, width):
        return jnp.dot(xb, w_ref[:, c0:c0 + width], preferred_element_type=F32)

    outs = ((q_ref, ATT_W, HEAD_DIM ** -0.5 * LOG2E), (qi_ref, IDX_HEADS * IDX_DIM, 1.0), (kd_ref, 2 * KV_W, 1.0))
    c0 = 0
    for ref, width, scale in outs:
        for j in range(0, width, 2 * LANES):
            p = dot_cols(c0 + j, 2 * LANES)
            for s in range(2):
                r = rope(p[:, s * LANES:(s + 1) * LANES])
                if scale != 1.0:
                    r = r * scale
                ref[:, j + s * LANES:j + (s + 1) * LANES] = r.astype(ref.dtype)
        c0 += width
    p = dot_cols(c0, LANES)
    ki_ref[...] = rope(p).astype(ki_ref.dtype)
    c0 += LANES
    vd_ref[...] = dot_cols(c0, 2 * KV_W).astype(vd_ref.dtype)
    c0 += 2 * KV_W
    wi_ref[...] = dot_cols(c0, LANES) * ((IDX_HEADS * IDX_DIM) ** -0.5)


def _proj_rope(h2d, w_a, cos, sin, rows_per_seq):
    n = h2d.shape[0]
    tm = min(rows_per_seq, 512)
    tiles_per_seq = rows_per_seq // tm
    assert n % tm == 0 and rows_per_seq % tm == 0
    row = lambda i: (i, 0)
    tab = lambda i: (i % tiles_per_seq, 0)
    widths = (ATT_W, IDX_HEADS * IDX_DIM, 2 * KV_W, LANES, 2 * KV_W, LANES)
    dtypes = (BF16, BF16, BF16, BF16, BF16, F32)
    return pl.pallas_call(
        _proj_rope_kernel,
        grid=(n // tm,),
        in_specs=[
            pl.BlockSpec((tm, D_MODEL), row),
            pl.BlockSpec((D_MODEL, _PROJ_A_W), lambda i: (0, 0)),
            pl.BlockSpec((tm, LANES), tab),
            pl.BlockSpec((tm, LANES), tab),
        ],
        out_specs=[pl.BlockSpec((tm, w), row) for w in widths],
        out_shape=[jax.ShapeDtypeStruct((n, w), dt) for w, dt in zip(widths, dtypes)],
        compiler_params=pltpu.CompilerParams(
            dimension_semantics=("arbitrary",), vmem_limit_bytes=VMEM_LIMIT),
        name="proj_rope",
    )(h2d, w_a, cos, sin)


_PROJ_B_W = 3 * CONV_W + 2 * D_MODEL


def _proj_gate_kernel(x_ref, w_ref, cu_ref, gb_ref, ga_ref, gc_ref):
    xb = x_ref[...].astype(BF16)

    def dot_cols(c0, width):
        return jnp.dot(xb, w_ref[:, c0:c0 + width], preferred_element_type=F32)

    cu_ref[...] = dot_cols(0, CONV_W) * dot_cols(CONV_W, CONV_W)
    gb_ref[...] = dot_cols(2 * CONV_W, CONV_W)
    ga_ref[...] = dot_cols(3 * CONV_W, D_MODEL)
    gc_ref[...] = dot_cols(3 * CONV_W + D_MODEL, D_MODEL)


def _proj_gate(h2d, w_b):
    n = h2d.shape[0]
    tm = min(n, 512)
    assert n % tm == 0
    row = lambda i: (i, 0)
    widths = (CONV_W, CONV_W, D_MODEL, D_MODEL)
    return pl.pallas_call(
        _proj_gate_kernel,
        grid=(n // tm,),
        in_specs=[pl.BlockSpec((tm, D_MODEL), row), pl.BlockSpec((D_MODEL, _PROJ_B_W), lambda i: (0, 0))],
        out_specs=[pl.BlockSpec((tm, w), row) for w in widths],
        out_shape=[jax.ShapeDtypeStruct((n, w), F32) for w in widths],
        compiler_params=pltpu.CompilerParams(
            dimension_semantics=("arbitrary",), vmem_limit_bytes=VMEM_LIMIT),
        name="proj_gate",
    )(h2d, w_b)


def _fold_rows(x, op):
    parts = [x[i:i + SUBLANES] for i in range(0, x.shape[0], SUBLANES)]
    while len(parts) > 1:
        parts = [op(parts[i], parts[i + 1]) for i in range(0, len(parts), 2)]
    return parts[0]


def _key_to_val(key):
    return pltpu.bitcast(jnp.where(key < 0, key ^ 0x7FFFFFFF, key), F32)


def _val_to_key(val):
    bits = pltpu.bitcast(jnp.where(val == 0.0, 0.0, val), I32)
    return jnp.where(bits < 0, bits ^ 0x7FFFFFFF, bits)


def _chunk_loop(nch, body, init):
    n_main = nch // UNROLL

    def main(i, carry):
        for u in range(UNROLL):
            carry = body(i * UNROLL + u, carry)
        return carry

    carry = lax.fori_loop(0, n_main, main, init)
    return lax.fori_loop(n_main * UNROLL, nch, body, carry)


def _attn_kernel(topk, qi_ref, wit_ref, q_ref, ki_ref, kd_ref, vt_ref, o_ref,
                 keys_ref, kmax_ref, stage_ref, p_ref, qit_ref, qt_ref, j_ref, m_ref, acc_ref):
    jblk = pl.program_id(1)
    qpos0 = N_META + jblk * TQ
    nch = (qpos0 + TQ - 1) // CH + 1
    kf = float(topk)
    n_pairs = ATT_HEADS // 2

    lane = lax.broadcasted_iota(I32, (TQ, LANES), 1)
    lo_half = lane < HEAD_DIM
    qpos_row = qpos0 + lax.broadcasted_iota(I32, (1, TQ), 1)
    searchable = (qpos_row + 1).astype(F32) > kf

    for s in range(n_pairs):
        for src, dst in ((qi_ref, qit_ref), (q_ref, qt_ref)):
            slab = src[0, :, s * LANES:(s + 1) * LANES].astype(F32)
            even = jnp.where(lo_half, slab, 0.0).T
            odd = jnp.where(lo_half, 0.0, slab).T
            dst[s] = jnp.concatenate([even, odd], axis=1).astype(BF16)

    kpos_iota = lax.broadcasted_iota(I32, (CH, TQ), 0)
    qpos = qpos0 + lax.broadcasted_iota(I32, (CH, TQ), 1)

    def score_chunk(c, carry):
        off = pl.multiple_of(c * CH, CH)
        kic = ki_ref[0, pl.ds(off, CH), :]
        acc = jnp.zeros((CH, TQ), F32)
        for s in range(IDX_HEADS // 2):
            d = jnp.dot(kic, qit_ref[s], preferred_element_type=F32)
            for e in range(2):
                w = wit_ref[0, 2 * s + e:2 * s + e + 1, :]
                acc = acc + jnp.maximum(d[:, e * LANES:(e + 1) * LANES], 0.0) * w
        key = jnp.where(off + kpos_iota <= qpos, _val_to_key(acc), INT_MIN)
        keys_ref[pl.ds(off, CH), :] = key
        kmax_ref[...] = jnp.maximum(kmax_ref[...], key)
        return carry

    n_span = (nch + UNROLL - 1) // UNROLL

    def score_span(i, carry):
        for u in range(UNROLL):
            carry = score_chunk(i * UNROLL + u, carry)
        return carry

    kmax_ref[...] = jnp.full(kmax_ref.shape, INT_MIN, I32)
    lax.fori_loop(0, n_span, score_span, 0)

    def count(pred):
        def body(c, acc):
            off = pl.multiple_of(c * CH, CH)
            hit = jnp.where(pred(keys_ref[pl.ds(off, CH), :], off + kpos_iota), 1.0, 0.0)
            return acc + _fold_rows(hit, jnp.add)
        acc = _chunk_loop(nch, body, jnp.zeros((SUBLANES, TQ), F32))
        return jnp.sum(acc, axis=0, keepdims=True)

    kmax = kmax_ref[...]
    lo0 = jnp.min(_fold_rows(kmax, jnp.minimum), axis=0, keepdims=True)
    hi0 = jnp.minimum(jnp.max(_fold_rows(kmax, jnp.maximum), axis=0, keepdims=True), INT_MAX - 1) + 1

    def live(lo, hi, c_lo):
        return searchable & (c_lo != kf) & (hi - lo != 1)

    def n_live(lo, hi, c_lo):
        return jnp.max(jnp.where(live(lo, hi, c_lo), 1.0, 0.0))

    def pick(it, lo, hi, c_lo):
        top = hi - 1
        mid_val = _val_to_key(0.5 * _key_to_val(lo) + 0.5 * _key_to_val(top))
        mid_key = lo + lax.shift_right_logical(hi - lo, 1)
        cand = jnp.where(it % KEY_STEP_PERIOD == KEY_STEP_PERIOD - 1, mid_key, mid_val)
        cand = jnp.where((lo < 0) & (top >= 0), 0, cand)
        cand = jnp.where((lo == 0) & (top >= 1), 1, cand)
        cand = jnp.minimum(jnp.maximum(cand, lo + 1), top)
        return jnp.where(live(lo, hi, c_lo), cand, lo)

    def search_steps(st, n_steps):
        it, lo, hi, c_lo = st
        for _ in range(n_steps):
            cand = pick(it, lo, hi, c_lo)
            cnt = count(lambda k, pos: k >= cand)
            ok = cnt >= kf
            lo = jnp.where(ok, cand, lo)
            c_lo = jnp.where(ok, cnt, c_lo)
            hi = jnp.where(ok, hi, cand)
            it = it + 1
        return it, lo, hi, c_lo

    def bis_cond(st):
        return (st[0] < MAX_SEARCH_STEPS) & (st[4] > 0.0)

    def bis_body(st):
        it, lo, hi, c_lo = search_steps(st[:4], STEPS_PER_EXIT_TEST)
        return it, lo, hi, c_lo, n_live(lo, hi, c_lo)

    c_unknown = jnp.full((1, TQ), 1e9, F32)
    st = lax.fori_loop(0, UNTESTED_STEPS // STEPS_PER_EXIT_TEST,
                       lambda _, st: search_steps(st, STEPS_PER_EXIT_TEST), (jnp.int32(0), lo0, hi0, c_unknown))
    _, t_fin, _, ct_fin, _ = lax.while_loop(bis_cond, bis_body, st + (n_live(*st[1:]),))
    t_fin = jnp.where(searchable, t_fin, INT_MIN)
    j_ref[...] = jnp.broadcast_to(jnp.where(searchable, 2 ** 30, -1).astype(I32), j_ref.shape)
    open_fin = jnp.max(jnp.where(searchable & (ct_fin != kf), 1.0, 0.0))

    @pl.when(open_fin > 0.0)
    def _():
        tie = searchable & (ct_fin > kf)
        need = kf - count(lambda k, pos: k > t_fin)
        jb = jnp.zeros((1, TQ), I32)
        for bit in range(13, -1, -1):
            cand = jb + (1 << bit)
            f_below = count(lambda k, pos: (k == t_fin) & (pos < cand))
            jb = jnp.where(f_below < need, cand, jb)
        j_ref[...] = jnp.where(tie, jnp.broadcast_to(jb, j_ref.shape), j_ref[...])

    j_fin = j_ref[0:1, :]

    def logits(c, s):
        g = (2 * s) // (ATT_HEADS // ATT_KV_HEADS)
        kg = kd_ref[0, pl.ds(pl.multiple_of(c * CH, CH), CH), g * LANES:(g + 1) * LANES]
        return jnp.dot(kg, qt_ref[s], preferred_element_type=F32)

    def span_step(c_new, c_old, n_sub, m_old, new, old):
        m_loc = [jnp.full((SUBLANES, TQ), -jnp.inf, F32) for _ in range(ATT_HEADS)]
        for u in range(n_sub):
            rows = slice(u * CH, (u + 1) * CH)
            if new:
                off_new = pl.multiple_of((c_new + u) * CH, CH)
                k = keys_ref[pl.ds(off_new, CH), :]
                sel = (k > t_fin) | ((k == t_fin) & (off_new + kpos_iota <= j_fin))
                bias = jnp.where(sel, 0.0, NEG_BIG)
            for s in range(n_pairs):
                if old:
                    ps = [jnp.exp2(stage_ref[s, rows, e * TQ:(e + 1) * TQ] - m_old[2 * s + e]).astype(BF16)
                          for e in range(2)]
                    p_ref[s, pl.ds(pl.multiple_of((c_old + u) * CH, CH), CH), :] = jnp.concatenate(ps, axis=1)
                if new:
                    lt2 = logits(c_new + u, s)
                    for e in range(2):
                        h = 2 * s + e
                        lt = lt2[:, e * TQ:(e + 1) * TQ] + bias
                        stage_ref[s, rows, e * TQ:(e + 1) * TQ] = lt
                        m_loc[h] = jnp.maximum(m_loc[h], _fold_rows(lt, jnp.maximum))
        return m_loc

    def advance_max(m_run, m_loc, slot):
        m_new = tuple(jnp.maximum(m_run[h], jnp.max(m_loc[h], axis=0, keepdims=True)) for h in range(ATT_HEADS))
        m_ref[slot] = jnp.concatenate(m_new, axis=0)
        return m_new

    def first_span(i, m_run):
        return advance_max(m_run, span_step(0, 0, UNROLL, None, new=True, old=False), 0)

    def middle_span(i, m_run):
        return advance_max(m_run, span_step(i * UNROLL, (i - 1) * UNROLL, UNROLL, m_run, new=True, old=True), i)

    m_run = first_span(0, tuple(jnp.full((1, TQ), -jnp.inf, F32) for _ in range(ATT_HEADS)))
    m_fin = lax.fori_loop(1, n_span, middle_span, m_run)
    span_step(0, (n_span - 1) * UNROLL, UNROLL, m_fin, new=False, old=True)

    def pv_span(off, width, slot):
        m_span = m_ref[slot]
        for s in range(n_pairs):
            g = (2 * s) // (ATT_HEADS // ATT_KV_HEADS)
            vgt = vt_ref[0, g * VT_ROWS:(g + 1) * VT_ROWS, pl.ds(off, width)]
            pv = jnp.dot(vgt, p_ref[s, pl.ds(off, width), :], preferred_element_type=F32)
            for e in range(2):
                h = 2 * s + e
                scale = jnp.exp2(m_span[h:h + 1, :] - m_fin[h])
                acc_ref[s, :, e * TQ:(e + 1) * TQ] += scale * pv[:, e * TQ:(e + 1) * TQ]

    acc_ref[...] = jnp.zeros(acc_ref.shape, F32)

    def pv_main(i, carry):
        pv_span(pl.multiple_of(i * (UNROLL * CH), UNROLL * CH), UNROLL * CH, i)
        return carry

    lax.fori_loop(0, n_span, pv_main, 0)

    for s in range(n_pairs):
        o_t = jnp.concatenate(
            [acc_ref[s, :HEAD_DIM, e * TQ:(e + 1) * TQ] / acc_ref[s, HEAD_DIM:HEAD_DIM + 1, e * TQ:(e + 1) * TQ]
             for e in range(2)], axis=0)
        o_ref[0, :, s * LANES:(s + 1) * LANES] = o_t.T.astype(o_ref.dtype)


def _dsa_attention(qi, wit, q, ki2, kd, vt, topk):
    b, s, _ = q.shape
    lk = kd.shape[1]
    assert s % TQ == 0 and lk % (UNROLL * CH) == 0 and lk >= N_META + s and topk <= CH
    qblk = lambda bi, j: (bi, j, 0)
    kblk = lambda bi, j: (bi, 0, 0)
    return pl.pallas_call(
        functools.partial(_attn_kernel, topk),
        grid=(b, s // TQ),
        in_specs=[
            pl.BlockSpec((1, TQ, IDX_HEADS * IDX_DIM), qblk),
            pl.BlockSpec((1, IDX_HEADS, TQ), lambda bi, j: (bi, 0, j)),
            pl.BlockSpec((1, TQ, ATT_W), qblk),
            pl.BlockSpec((1, lk, LANES), kblk),
            pl.BlockSpec((1, lk, 2 * KV_W), kblk),
            pl.BlockSpec((1, ATT_KV_HEADS * VT_ROWS, lk), kblk),
        ],
        out_specs=pl.BlockSpec((1, TQ, ATT_W), qblk),
        out_shape=jax.ShapeDtypeStruct((b, s, ATT_W), BF16),
        scratch_shapes=[
            pltpu.VMEM((lk, TQ), I32),
            pltpu.VMEM((CH, TQ), I32),
            pltpu.VMEM((ATT_HEADS // 2, UNROLL * CH, 2 * TQ), F32),
            pltpu.VMEM((ATT_HEADS // 2, lk, 2 * TQ), BF16),
            pltpu.VMEM((IDX_HEADS // 2, LANES, 2 * TQ), BF16),
            pltpu.VMEM((ATT_HEADS // 2, LANES, 2 * TQ), BF16),
            pltpu.VMEM((SUBLANES, TQ), I32),
            pltpu.VMEM((lk // CH, ATT_HEADS, TQ), F32),
            pltpu.VMEM((ATT_HEADS // 2, VT_ROWS, 2 * TQ), F32),
        ],
        compiler_params=pltpu.CompilerParams(
            dimension_semantics=("arbitrary", "arbitrary"), vmem_limit_bytes=VMEM_LIMIT),
        name="dsa_attn",
    )(qi, wit, q, ki2, kd, vt)


def _merge_ln_kernel(att_ref, cu_ref, halo_ref, gb_ref, ga_ref, gc_ref, h_ref, cw_ref,
                     wa_ref, wc_ref, wo_ref, g_ref, b_ref, o_ref):
    tm = cu_ref.shape[1]
    cu = cu_ref[0]
    halo = halo_ref[0, 0]
    row = lax.broadcasted_iota(I32, (tm, CONV_W), 0)
    prev1 = jnp.where(row == 0, halo[7:8], pltpu.roll(cu, 1, 0))
    prev2 = jnp.where(row == 0, halo[6:7], jnp.where(row == 1, halo[7:8], pltpu.roll(cu, 2, 0)))
    cw = cw_ref[...]
    conv = cw[0:1] * prev2 + cw[1:2] * prev1 + cw[2:3] * cu
    y_conv = jnp.dot((gb_ref[0] * conv).astype(BF16), wc_ref[...], preferred_element_type=F32)
    y_att = jnp.dot(att_ref[0], wa_ref[...], preferred_element_type=F32)
    merged = jax.nn.sigmoid(ga_ref[0]) * y_att + jax.nn.sigmoid(gc_ref[0]) * y_conv
    mix = jnp.dot(merged.astype(BF16), wo_ref[...], preferred_element_type=F32)
    o_ref[0] = _layer_norm(DEEPNORM_ALPHA * h_ref[0] + mix, g_ref[...], b_ref[...])


def _merge_ln(att, cu, halo, gb, ga, gc, h1, conv_w8, wa, wc, wo, g, b):
    bsz, s, _ = cu.shape
    tm = s // halo.shape[1]
    blk = lambda bi, i: (bi, i, 0)
    const = lambda bi, i: (0, 0)
    return pl.pallas_call(
        _merge_ln_kernel,
        grid=(bsz, s // tm),
        in_specs=[
            pl.BlockSpec((1, tm, ATT_W), blk),
            pl.BlockSpec((1, tm, CONV_W), blk),
            pl.BlockSpec((1, 1, 8, CONV_W), lambda bi, i: (bi, i, 0, 0)),
            pl.BlockSpec((1, tm, CONV_W), blk),
            pl.BlockSpec((1, tm, D_MODEL), blk),
            pl.BlockSpec((1, tm, D_MODEL), blk),
            pl.BlockSpec((1, tm, D_MODEL), blk),
            pl.BlockSpec((8, CONV_W), const),
            pl.BlockSpec((ATT_W, D_MODEL), const),
            pl.BlockSpec((CONV_W, D_MODEL), const),
            pl.BlockSpec((D_MODEL, D_MODEL), const),
            pl.BlockSpec((1, D_MODEL), const),
            pl.BlockSpec((1, D_MODEL), const),
        ],
        out_specs=pl.BlockSpec((1, tm, D_MODEL), blk),
        out_shape=jax.ShapeDtypeStruct((bsz, s, D_MODEL), F32),
        compiler_params=pltpu.CompilerParams(
            dimension_semantics=("arbitrary", "arbitrary"), vmem_limit_bytes=VMEM_LIMIT),
        name="merge_ln",
    )(att, cu, halo, gb, ga, gc, h1, conv_w8, wa, wc, wo, g, b)


def _rope_tables(pos):
    half = HEAD_DIM // 2
    inv_freq = ROPE_THETA ** (-jnp.arange(half, dtype=F32) / half)
    ang = pos.astype(F32)[:, None] * inv_freq[None, :]
    cos, sin = jnp.cos(ang), jnp.sin(ang)
    return jnp.tile(cos, (1, LANES // half)), jnp.tile(jnp.concatenate([-sin, sin], axis=1), (1, LANES // HEAD_DIM))


def _split_w_in(w_in):
    widths = (ATT_W, KV_W, KV_W, IDX_HEADS * IDX_DIM, IDX_DIM, IDX_HEADS,
              CONV_W, CONV_W, CONV_W, D_MODEL, D_MODEL)
    offs = np.cumsum(widths)[:-1].tolist()
    wq, wk, wv, wqi, wki, wwi, wu, wgb, wgc, wga, wgv = jnp.split(w_in, offs, axis=1)
    dup = lambda w: jnp.concatenate([w[:, :HEAD_DIM], w[:, :HEAD_DIM], w[:, HEAD_DIM:], w[:, HEAD_DIM:]], axis=1)
    w_a = jnp.concatenate(
        [wq, wqi, dup(wk), wki, wki, dup(wv), wwi, jnp.zeros((D_MODEL, LANES - IDX_HEADS), w_in.dtype)], axis=1)
    w_b = jnp.concatenate([wu, wgc, wgb, wga, wgv], axis=1)
    return w_a.astype(BF16), w_b.astype(BF16)


def kernel(x, meta_tokens, ffn1_w_gate, ffn1_w_up, ffn1_w_down, ln1_g, ln1_b, w_in, conv_w, w_att_out, w_conv_out, w_o, ln2_g, ln2_b, ffn2_w_gate, ffn2_w_up, ffn2_w_down, ln3_g, ln3_b):
    bsz, seq, _ = x.shape
    l_tot = N_META + seq
    topk = min(TOPK_MAX, l_tot // 4)
    lk = -(-l_tot // (UNROLL * CH)) * (UNROLL * CH)
    lyr = 0
    bf = lambda w: w.astype(BF16)
    vec = lambda v: v[lyr][None, :]

    w1 = (bf(ffn1_w_gate[lyr]), bf(ffn1_w_up[lyr]), bf(ffn1_w_down[lyr]), vec(ln1_g), vec(ln1_b))
    w_a, w_b = _split_w_in(w_in[lyr])
    pos = jnp.arange(l_tot, dtype=I32)
    cos, sin = _rope_tables(pos)

    hm = _ffn_ln(meta_tokens.astype(F32), *w1)
    _, _, kd_m, ki_m, vd_m, _ = _proj_rope(hm, w_a, cos[:N_META], sin[:N_META], N_META)
    cu_m = _proj_gate(hm, w_b)[0]

    h1 = _ffn_ln(x.reshape(bsz * seq, D_MODEL), *w1)
    q, qi, kd, ki2, vd, wi = _proj_rope(h1, w_a, cos[N_META:], sin[N_META:], seq)
    cu, gb, ga, gc = _proj_gate(h1, w_b)

    def with_meta(meta_rows, real):
        wdt = real.shape[-1]
        return jnp.concatenate(
            [jnp.broadcast_to(meta_rows[None], (bsz, N_META, wdt)), real.reshape(bsz, seq, wdt),
             jnp.zeros((bsz, lk - l_tot, wdt), real.dtype)], axis=1)

    per_seq = lambda a: a.reshape(bsz, seq, a.shape[-1])
    v_keys = with_meta(vd_m, vd)
    ones_pad = jnp.zeros((bsz, lk, VT_ROWS - HEAD_DIM), BF16).at[:, :, 0].set(1.0)
    vt = jnp.swapaxes(jnp.concatenate(
        [v_keys[:, :, :HEAD_DIM], ones_pad, v_keys[:, :, 2 * HEAD_DIM:3 * HEAD_DIM], ones_pad], axis=2), 1, 2)
    wit = jnp.swapaxes(per_seq(wi)[:, :, :IDX_HEADS], 1, 2)
    att = _dsa_attention(per_seq(qi), wit, per_seq(q), with_meta(ki_m, ki2), with_meta(kd_m, kd), vt, topk)

    tm = min(seq, 512)
    cu3 = per_seq(cu)
    tails = cu3.reshape(bsz, seq // tm, tm, CONV_W)[:, :-1, tm - 8:, :]
    halo = jnp.concatenate([jnp.broadcast_to(cu_m[None, None, N_META - 8:], (bsz, 1, 8, CONV_W)), tails], axis=1)
    conv_w8 = jnp.concatenate([conv_w[lyr].astype(F32), jnp.zeros((8 - CONV_K, CONV_W), F32)], axis=0)
    h2 = _merge_ln(att, cu3, halo, per_seq(gb), per_seq(ga), per_seq(gc), per_seq(h1), conv_w8,
                   bf(w_att_out[lyr]), bf(w_conv_out[lyr]), bf(w_o[lyr]), vec(ln2_g), vec(ln2_b))

    h3 = _ffn_ln(h2.reshape(bsz * seq, D_MODEL), bf(ffn2_w_gate[lyr]), bf(ffn2_w_up[lyr]), bf(ffn2_w_down[lyr]),
                 vec(ln3_g), vec(ln3_b))
    return h3.reshape(bsz, seq, D_MODEL)
```

```python
import functools

import numpy as np
import jax
import jax.numpy as jnp
from jax import lax
from jax.experimental import pallas as pl
from jax.experimental.pallas import tpu as pltpu

F32 = jnp.float32
BF16 = jnp.bfloat16
I32 = jnp.int32

D_MODEL = 1024
N_META = 16
ATT_HEADS = 8
ATT_KV_HEADS = 2
HEAD_DIM = 64
ATT_W = ATT_HEADS * HEAD_DIM
KV_W = ATT_KV_HEADS * HEAD_DIM
IDX_HEADS = 8
IDX_DIM = 64
TOPK_MAX = 256
ROPE_THETA = 10000.0
CONV_W = D_MODEL // 2
CONV_K = 3
D_FF = 2816
LN_EPS = 1e-5
DEPTH = 1
DEEPNORM_ALPHA = (2.0 * DEPTH) ** 0.25

LANES = 128
SUBLANES = 8
TQ = 128
CH = 256
VT_ROWS = 80
UNROLL = 4
STEPS_PER_EXIT_TEST = 2
UNTESTED_STEPS = 12
KEY_STEP_PERIOD = 4
MAX_SEARCH_STEPS = 160
INT_MAX = 2 ** 31 - 1
LOG2E = 1.4426950408889634
INT_MIN = -(2 ** 31)
NEG_BIG = -1e30
VMEM_LIMIT = 52 * 1024 * 1024


def _layer_norm(y, g, b):
    mu = jnp.mean(y, axis=-1, keepdims=True)
    yc = y - mu
    var = jnp.mean(yc * yc, axis=-1, keepdims=True)
    return yc * lax.rsqrt(var + LN_EPS) * g + b


def _ffn_ln_kernel(x_ref, wg_ref, wu_ref, wd_ref, g_ref, b_ref, o_ref, acc_ref, xb_ref):
    f = pl.program_id(1)

    @pl.when(f == 0)
    def _():
        acc_ref[...] = jnp.zeros_like(acc_ref)
        xb_ref[...] = x_ref[...].astype(BF16)

    xb = xb_ref[...]
    gt = jnp.dot(xb, wg_ref[...], preferred_element_type=F32)
    up = jnp.dot(xb, wu_ref[...], preferred_element_type=F32)
    a = (gt * jax.nn.sigmoid(gt)) * up
    acc_ref[...] += jnp.dot(a.astype(BF16), wd_ref[...], preferred_element_type=F32)

    @pl.when(f == pl.num_programs(1) - 1)
    def _():
        y = DEEPNORM_ALPHA * x_ref[...] + 0.5 * acc_ref[...]
        o_ref[...] = _layer_norm(y, g_ref[...], b_ref[...])


def _ffn_ln(x2d, wg, wu, wd, g, b):
    n = x2d.shape[0]
    tm = min(n, 1024)
    tf = 256
    assert n % tm == 0 and D_FF % tf == 0
    return pl.pallas_call(
        _ffn_ln_kernel,
        grid=(n // tm, D_FF // tf),
        in_specs=[
            pl.BlockSpec((tm, D_MODEL), lambda i, f: (i, 0)),
            pl.BlockSpec((D_MODEL, tf), lambda i, f: (0, f)),
            pl.BlockSpec((D_MODEL, tf), lambda i, f: (0, f)),
            pl.BlockSpec((tf, D_MODEL), lambda i, f: (f, 0)),
            pl.BlockSpec((1, D_MODEL), lambda i, f: (0, 0)),
            pl.BlockSpec((1, D_MODEL), lambda i, f: (0, 0)),
        ],
        out_specs=pl.BlockSpec((tm, D_MODEL), lambda i, f: (i, 0)),
        out_shape=jax.ShapeDtypeStruct((n, D_MODEL), F32),
        scratch_shapes=[pltpu.VMEM((tm, D_MODEL), F32), pltpu.VMEM((tm, D_MODEL), BF16)],
        compiler_params=pltpu.CompilerParams(
            dimension_semantics=("arbitrary", "arbitrary"), vmem_limit_bytes=VMEM_LIMIT),
        name="ffn_ln",
    )(x2d, wg, wu, wd, g, b)


_PROJ_A_W = ATT_W + IDX_HEADS * IDX_DIM + 2 * KV_W + LANES + 2 * KV_W + LANES


def _proj_rope_kernel(x_ref, w_ref, cos_ref, sin_ref, q_ref, qi_ref, kd_ref, ki_ref, vd_ref, wi_ref):
    tm = x_ref.shape[0]
    xb = x_ref[...].astype(BF16)
    cos = cos_ref[...]
    sin = sin_ref[...]
    lane = lax.broadcasted_iota(I32, (tm, LANES), 1)
    first_half = (lane & (HEAD_DIM // 2)) == 0

    def rope(s):
        partner = jnp.where(first_half, pltpu.roll(s, LANES - HEAD_DIM // 2, 1),
                            pltpu.roll(s, HEAD_DIM // 2, 1))
        return s * cos + partner * sin

    def dot_cols(c0, width):
        return jnp.dot(xb, w_ref[:, c0:c0 + width], preferred_element_type=F32)

    outs = ((q_ref, ATT_W, HEAD_DIM ** -0.5 * LOG2E), (qi_ref, IDX_HEADS * IDX_DIM, 1.0), (kd_ref, 2 * KV_W, 1.0))
    c0 = 0
    for ref, width, scale in outs:
        for j in range(0, width, 2 * LANES):
            p = dot_cols(c0 + j, 2 * LANES)
            for s in range(2):
                r = rope(p[:, s * LANES:(s + 1) * LANES])
                if scale != 1.0:
                    r = r * scale
                ref[:, j + s * LANES:j + (s + 1) * LANES] = r.astype(ref.dtype)
        c0 += width
    p = dot_cols(c0, LANES)
    ki_ref[...] = rope(p).astype(ki_ref.dtype)
    c0 += LANES
    vd_ref[...] = dot_cols(c0, 2 * KV_W).astype(vd_ref.dtype)
    c0 += 2 * KV_W
    wi_ref[...] = dot_cols(c0, LANES) * ((IDX_HEADS * IDX_DIM) ** -0.5)


def _proj_rope(h2d, w_a, cos, sin, rows_per_seq):
    n = h2d.shape[0]
    tm = min(rows_per_seq, 512)
    tiles_per_seq = rows_per_seq // tm
    assert n % tm == 0 and rows_per_seq % tm == 0
    row = lambda i: (i, 0)
    tab = lambda i: (i % tiles_per_seq, 0)
    widths = (ATT_W, IDX_HEADS * IDX_DIM, 2 * KV_W, LANES, 2 * KV_W, LANES)
    dtypes = (BF16, BF16, BF16, BF16, BF16, F32)
    return pl.pallas_call(
        _proj_rope_kernel,
        grid=(n // tm,),
        in_specs=[
            pl.BlockSpec((tm, D_MODEL), row),
            pl.BlockSpec((D_MODEL, _PROJ_A_W), lambda i: (0, 0)),
            pl.BlockSpec((tm, LANES), tab),
            pl.BlockSpec((tm, LANES), tab),
        ],
        out_specs=[pl.BlockSpec((tm, w), row) for w in widths],
        out_shape=[jax.ShapeDtypeStruct((n, w), dt) for w, dt in zip(widths, dtypes)],
        compiler_params=pltpu.CompilerParams(
            dimension_semantics=("arbitrary",), vmem_limit_bytes=VMEM_LIMIT),
        name="proj_rope",
    )(h2d, w_a, cos, sin)


_PROJ_B_W = 3 * CONV_W + 2 * D_MODEL


def _proj_gate_kernel(x_ref, w_ref, cu_ref, gb_ref, ga_ref, gc_ref):
    xb = x_ref[...].astype(BF16)

    def dot_cols(c0, width):
        return jnp.dot(xb, w_ref[:, c0:c0 + width], preferred_element_type=F32)

    cu_ref[...] = dot_cols(0, CONV_W) * dot_cols(CONV_W, CONV_W)
    gb_ref[...] = dot_cols(2 * CONV_W, CONV_W)
    ga_ref[...] = dot_cols(3 * CONV_W, D_MODEL)
    gc_ref[...] = dot_cols(3 * CONV_W + D_MODEL, D_MODEL)


def _proj_gate(h2d, w_b):
    n = h2d.shape[0]
    tm = min(n, 512)
    assert n % tm == 0
    row = lambda i: (i, 0)
    widths = (CONV_W, CONV_W, D_MODEL, D_MODEL)
    return pl.pallas_call(
        _proj_gate_kernel,
        grid=(n // tm,),
        in_specs=[pl.BlockSpec((tm, D_MODEL), row), pl.BlockSpec((D_MODEL, _PROJ_B_W), lambda i: (0, 0))],
        out_specs=[pl.BlockSpec((tm, w), row) for w in widths],
        out_shape=[jax.ShapeDtypeStruct((n, w), F32) for w in widths],
        compiler_params=pltpu.CompilerParams(
            dimension_semantics=("arbitrary",), vmem_limit_bytes=VMEM_LIMIT),
        name="proj_gate",
    )(h2d, w_b)


def _fold_rows(x, op):
    parts = [x[i:i + SUBLANES] for i in range(0, x.shape[0], SUBLANES)]
    while len(parts) > 1:
        parts = [op(parts[i], parts[i + 1]) for i in range(0, len(parts), 2)]
    return parts[0]


def _key_to_val(key):
    return pltpu.bitcast(jnp.where(key < 0, key ^ 0x7FFFFFFF, key), F32)


def _val_to_key(val):
    bits = pltpu.bitcast(jnp.where(val == 0.0, 0.0, val), I32)
    return jnp.where(bits < 0, bits ^ 0x7FFFFFFF, bits)


def _chunk_loop(nch, body, init):
    n_main = nch // UNROLL

    def main(i, carry):
        for u in range(UNROLL):
            carry = body(i * UNROLL + u, carry)
        return carry

    carry = lax.fori_loop(0, n_main, main, init)
    return lax.fori_loop(n_main * UNROLL, nch, body, carry)


def _attn_kernel(topk, qi_ref, wit_ref, q_ref, ki_ref, kd_ref, vt_ref, o_ref,
                 keys_ref, kmax_ref, stage_ref, p_ref, qit_ref, qt_ref, j_ref, m_ref, acc_ref):
    jblk = pl.program_id(1)
    qpos0 = N_META + jblk * TQ
    nch = (qpos0 + TQ - 1) // CH + 1
    kf = float(topk)
    n_pairs = ATT_HEADS // 2

    lane = lax.broadcasted_iota(I32, (TQ, LANES), 1)
    lo_half = lane < HEAD_DIM
    qpos_row = qpos0 + lax.broadcasted_iota(I32, (1, TQ), 1)
    searchable = (qpos_row + 1).astype(F32) > kf

    for s in range(n_pairs):
        for src, dst in ((qi_ref, qit_ref), (q_ref, qt_ref)):
            slab = src[0, :, s * LANES:(s + 1) * LANES].astype(F32)
            even = jnp.where(lo_half, slab, 0.0).T
            odd = jnp.where(lo_half, 0.0, slab).T
            dst[s] = jnp.concatenate([even, odd], axis=1).astype(BF16)

    kpos_iota = lax.broadcasted_iota(I32, (CH, TQ), 0)
    qpos = qpos0 + lax.broadcasted_iota(I32, (CH, TQ), 1)

    def score_chunk(c, carry):
        off = pl.multiple_of(c * CH, CH)
        kic = ki_ref[0, pl.ds(off, CH), :]
        acc = jnp.zeros((CH, TQ), F32)
        for s in range(IDX_HEADS // 2):
            d = jnp.dot(kic, qit_ref[s], preferred_element_type=F32)
            for e in range(2):
                w = wit_ref[0, 2 * s + e:2 * s + e + 1, :]
                acc = acc + jnp.maximum(d[:, e * LANES:(e + 1) * LANES], 0.0) * w
        key = jnp.where(off + kpos_iota <= qpos, _val_to_key(acc), INT_MIN)
        keys_ref[pl.ds(off, CH), :] = key
        kmax_ref[...] = jnp.maximum(kmax_ref[...], key)
        return carry

    n_span = (nch + UNROLL - 1) // UNROLL

    def score_span(i, carry):
        for u in range(UNROLL):
            carry = score_chunk(i * UNROLL + u, carry)
        return carry

    kmax_ref[...] = jnp.full(kmax_ref.shape, INT_MIN, I32)
    lax.fori_loop(0, n_span, score_span, 0)

    def count(pred):
        def body(c, acc):
            off = pl.multiple_of(c * CH, CH)
            hit = jnp.where(pred(keys_ref[pl.ds(off, CH), :], off + kpos_iota), 1.0, 0.0)
            return acc + _fold_rows(hit, jnp.add)
        acc = _chunk_loop(nch, body, jnp.zeros((SUBLANES, TQ), F32))
        return jnp.sum(acc, axis=0, keepdims=True)

    kmax = kmax_ref[...]
    lo0 = jnp.min(_fold_rows(kmax, jnp.minimum), axis=0, keepdims=True)
    hi0 = jnp.minimum(jnp.max(_fold_rows(kmax, jnp.maximum), axis=0, keepdims=True), INT_MAX - 1) + 1

    def live(lo, hi, c_lo):
        return searchable & (c_lo != kf) & (hi - lo != 1)

    def n_live(lo, hi, c_lo):
        return jnp.max(jnp.where(live(lo, hi, c_lo), 1.0, 0.0))

    def pick(it, lo, hi, c_lo):
        top = hi - 1
        mid_val = _val_to_key(0.5 * _key_to_val(lo) + 0.5 * _key_to_val(top))
        mid_key = lo + lax.shift_right_logical(hi - lo, 1)
        cand = jnp.where(it % KEY_STEP_PERIOD == KEY_STEP_PERIOD - 1, mid_key, mid_val)
        cand = jnp.where((lo < 0) & (top >= 0), 0, cand)
        cand = jnp.where((lo == 0) & (top >= 1), 1, cand)
        cand = jnp.minimum(jnp.maximum(cand, lo + 1), top)
        return jnp.where(live(lo, hi, c_lo), cand, lo)

    def search_steps(st, n_steps):
        it, lo, hi, c_lo, c_hi = st
        for _ in range(n_steps):
            cand = pick(it, lo, hi, c_lo)
            cnt = count(lambda k, pos: k >= cand)
            ok = cnt >= kf
            lo = jnp.where(ok, cand, lo)
            c_lo = jnp.where(ok, cnt, c_lo)
            hi = jnp.where(ok, hi, cand)
            c_hi = jnp.where(ok, c_hi, cnt)
            it = it + 1
        return it, lo, hi, c_lo, c_hi

    def bis_cond(st):
        return (st[0] < MAX_SEARCH_STEPS) & (st[5] > 0.0)

    def bis_body(st):
        st = search_steps(st[:5], STEPS_PER_EXIT_TEST)
        return st + (n_live(st[1], st[2], st[3]),)

    c_unknown = jnp.full((1, TQ), 1e9, F32)
    st = (jnp.int32(0), lo0, hi0, c_unknown, jnp.zeros((1, TQ), F32))
    st = lax.fori_loop(0, UNTESTED_STEPS // STEPS_PER_EXIT_TEST,
                       lambda _, st: search_steps(st, STEPS_PER_EXIT_TEST), st)
    _, t_fin, _, ct_fin, c_above, _ = lax.while_loop(bis_cond, bis_body, st + (n_live(st[1], st[2], st[3]),))
    t_fin = jnp.where(searchable, t_fin, INT_MIN)
    j_ref[...] = jnp.broadcast_to(jnp.where(searchable, 2 ** 30, -1).astype(I32), j_ref.shape)
    open_fin = jnp.max(jnp.where(searchable & (ct_fin != kf), 1.0, 0.0))

    @pl.when(open_fin > 0.0)
    def _():
        tie_q = searchable & (ct_fin != kf)
        need = kf - c_above
        tri = jnp.where(lax.broadcasted_iota(I32, (CH, CH), 0) >= lax.broadcasted_iota(I32, (CH, CH), 1),
                        1.0, 0.0).astype(BF16)

        def body(c, carry):
            seen, jmax = carry
            off = pl.multiple_of(c * CH, CH)
            is_tie = keys_ref[pl.ds(off, CH), :] == t_fin
            rank = jnp.dot(tri, jnp.where(is_tie, 1.0, 0.0).astype(BF16), preferred_element_type=F32) + seen
            kept_pos = jnp.where(is_tie & (rank <= need), off + kpos_iota, -1)
            return rank[CH - 1:CH, :], jnp.maximum(jmax, _fold_rows(kept_pos, jnp.maximum))

        _, jmax = _chunk_loop(nch, body, (jnp.zeros((1, TQ), F32), jnp.full((SUBLANES, TQ), -1, I32)))
        j_tie = jnp.max(jmax, axis=0, keepdims=True)
        j_ref[...] = jnp.where(tie_q, jnp.broadcast_to(j_tie, j_ref.shape), j_ref[...])

    j_fin = j_ref[0:1, :]

    def logits(c, s):
        g = (2 * s) // (ATT_HEADS // ATT_KV_HEADS)
        kg = kd_ref[0, pl.ds(pl.multiple_of(c * CH, CH), CH), g * LANES:(g + 1) * LANES]
        return jnp.dot(kg, qt_ref[s], preferred_element_type=F32)

    def span_step(c_new, c_old, n_sub, m_old, new, old):
        m_loc = [jnp.full((SUBLANES, TQ), -jnp.inf, F32) for _ in range(ATT_HEADS)]
        for u in range(n_sub):
            rows = slice(u * CH, (u + 1) * CH)
            if new:
                off_new = pl.multiple_of((c_new + u) * CH, CH)
                k = keys_ref[pl.ds(off_new, CH), :]
                sel = (k > t_fin) | ((k == t_fin) & (off_new + kpos_iota <= j_fin))
                bias = jnp.where(sel, 0.0, NEG_BIG)
            for s in range(n_pairs):
                if old:
                    ps = [jnp.exp2(stage_ref[s, rows, e * TQ:(e + 1) * TQ] - m_old[2 * s + e]).astype(BF16)
                          for e in range(2)]
                    p_ref[s, pl.ds(pl.multiple_of((c_old + u) * CH, CH), CH), :] = jnp.concatenate(ps, axis=1)
                if new:
                    lt2 = logits(c_new + u, s)
                    for e in range(2):
                        h = 2 * s + e
                        lt = lt2[:, e * TQ:(e + 1) * TQ] + bias
                        stage_ref[s, rows, e * TQ:(e + 1) * TQ] = lt
                        m_loc[h] = jnp.maximum(m_loc[h], _fold_rows(lt, jnp.maximum))
        return m_loc

    def advance_max(m_run, m_loc, slot):
        m_new = tuple(jnp.maximum(m_run[h], jnp.max(m_loc[h], axis=0, keepdims=True)) for h in range(ATT_HEADS))
        m_ref[slot] = jnp.concatenate(m_new, axis=0)
        return m_new

    def first_span(i, m_run):
        return advance_max(m_run, span_step(0, 0, UNROLL, None, new=True, old=False), 0)

    def middle_span(i, m_run):
        return advance_max(m_run, span_step(i * UNROLL, (i - 1) * UNROLL, UNROLL, m_run, new=True, old=True), i)

    m_run = first_span(0, tuple(jnp.full((1, TQ), -jnp.inf, F32) for _ in range(ATT_HEADS)))
    m_fin = lax.fori_loop(1, n_span, middle_span, m_run)
    span_step(0, (n_span - 1) * UNROLL, UNROLL, m_fin, new=False, old=True)

    def pv_span(off, width, slot):
        m_span = m_ref[slot]
        for s in range(n_pairs):
            g = (2 * s) // (ATT_HEADS // ATT_KV_HEADS)
            vgt = vt_ref[0, g * VT_ROWS:(g + 1) * VT_ROWS, pl.ds(off, width)]
            pv = jnp.dot(vgt, p_ref[s, pl.ds(off, width), :], preferred_element_type=F32)
            for e in range(2):
                h = 2 * s + e
                scale = jnp.exp2(m_span[h:h + 1, :] - m_fin[h])
                acc_ref[s, :, e * TQ:(e + 1) * TQ] += scale * pv[:, e * TQ:(e + 1) * TQ]

    acc_ref[...] = jnp.zeros(acc_ref.shape, F32)

    def pv_main(i, carry):
        pv_span(pl.multiple_of(i * (UNROLL * CH), UNROLL * CH), UNROLL * CH, i)
        return carry

    lax.fori_loop(0, n_span, pv_main, 0)

    for s in range(n_pairs):
        o_t = jnp.concatenate(
            [acc_ref[s, :HEAD_DIM, e * TQ:(e + 1) * TQ] / acc_ref[s, HEAD_DIM:HEAD_DIM + 1, e * TQ:(e + 1) * TQ]
             for e in range(2)], axis=0)
        o_ref[0, :, s * LANES:(s + 1) * LANES] = o_t.T.astype(o_ref.dtype)


def _dsa_attention(qi, wit, q, ki2, kd, vt, topk):
    b, s, _ = q.shape
    lk = kd.shape[1]
    assert s % TQ == 0 and lk % (UNROLL * CH) == 0 and lk >= N_META + s and topk <= CH
    qblk = lambda bi, j: (bi, j, 0)
    kblk = lambda bi, j: (bi, 0, 0)
    return pl.pallas_call(
        functools.partial(_attn_kernel, topk),
        grid=(b, s // TQ),
        in_specs=[
            pl.BlockSpec((1, TQ, IDX_HEADS * IDX_DIM), qblk),
            pl.BlockSpec((1, IDX_HEADS, TQ), lambda bi, j: (bi, 0, j)),
            pl.BlockSpec((1, TQ, ATT_W), qblk),
            pl.BlockSpec((1, lk, LANES), kblk),
            pl.BlockSpec((1, lk, 2 * KV_W), kblk),
            pl.BlockSpec((1, ATT_KV_HEADS * VT_ROWS, lk), kblk),
        ],
        out_specs=pl.BlockSpec((1, TQ, ATT_W), qblk),
        out_shape=jax.ShapeDtypeStruct((b, s, ATT_W), BF16),
        scratch_shapes=[
            pltpu.VMEM((lk, TQ), I32),
            pltpu.VMEM((CH, TQ), I32),
            pltpu.VMEM((ATT_HEADS // 2, UNROLL * CH, 2 * TQ), F32),
            pltpu.VMEM((ATT_HEADS // 2, lk, 2 * TQ), BF16),
            pltpu.VMEM((IDX_HEADS // 2, LANES, 2 * TQ), BF16),
            pltpu.VMEM((ATT_HEADS // 2, LANES, 2 * TQ), BF16),
            pltpu.VMEM((SUBLANES, TQ), I32),
            pltpu.VMEM((lk // CH, ATT_HEADS, TQ), F32),
            pltpu.VMEM((ATT_HEADS // 2, VT_ROWS, 2 * TQ), F32),
        ],
        compiler_params=pltpu.CompilerParams(
            dimension_semantics=("arbitrary", "arbitrary"), vmem_limit_bytes=VMEM_LIMIT),
        name="dsa_attn",
    )(qi, wit, q, ki2, kd, vt)


def _merge_ln_kernel(att_ref, cu_ref, halo_ref, gb_ref, ga_ref, gc_ref, h_ref, cw_ref,
                     wa_ref, wc_ref, wo_ref, g_ref, b_ref, o_ref):
    tm = cu_ref.shape[1]
    cu = cu_ref[0]
    halo = halo_ref[0, 0]
    row = lax.broadcasted_iota(I32, (tm, CONV_W), 0)
    prev1 = jnp.where(row == 0, halo[7:8], pltpu.roll(cu, 1, 0))
    prev2 = jnp.where(row == 0, halo[6:7], jnp.where(row == 1, halo[7:8], pltpu.roll(cu, 2, 0)))
    cw = cw_ref[...]
    conv = cw[0:1] * prev2 + cw[1:2] * prev1 + cw[2:3] * cu
    y_conv = jnp.dot((gb_ref[0] * conv).astype(BF16), wc_ref[...], preferred_element_type=F32)
    y_att = jnp.dot(att_ref[0], wa_ref[...], preferred_element_type=F32)
    merged = jax.nn.sigmoid(ga_ref[0]) * y_att + jax.nn.sigmoid(gc_ref[0]) * y_conv
    mix = jnp.dot(merged.astype(BF16), wo_ref[...], preferred_element_type=F32)
    o_ref[0] = _layer_norm(DEEPNORM_ALPHA * h_ref[0] + mix, g_ref[...], b_ref[...])


def _merge_ln(att, cu, halo, gb, ga, gc, h1, conv_w8, wa, wc, wo, g, b):
    bsz, s, _ = cu.shape
    tm = s // halo.shape[1]
    blk = lambda bi, i: (bi, i, 0)
    const = lambda bi, i: (0, 0)
    return pl.pallas_call(
        _merge_ln_kernel,
        grid=(bsz, s // tm),
        in_specs=[
            pl.BlockSpec((1, tm, ATT_W), blk),
            pl.BlockSpec((1, tm, CONV_W), blk),
            pl.BlockSpec((1, 1, 8, CONV_W), lambda bi, i: (bi, i, 0, 0)),
            pl.BlockSpec((1, tm, CONV_W), blk),
            pl.BlockSpec((1, tm, D_MODEL), blk),
            pl.BlockSpec((1, tm, D_MODEL), blk),
            pl.BlockSpec((1, tm, D_MODEL), blk),
            pl.BlockSpec((8, CONV_W), const),
            pl.BlockSpec((ATT_W, D_MODEL), const),
            pl.BlockSpec((CONV_W, D_MODEL), const),
            pl.BlockSpec((D_MODEL, D_MODEL), const),
            pl.BlockSpec((1, D_MODEL), const),
            pl.BlockSpec((1, D_MODEL), const),
        ],
        out_specs=pl.BlockSpec((1, tm, D_MODEL), blk),
        out_shape=jax.ShapeDtypeStruct((bsz, s, D_MODEL), F32),
        compiler_params=pltpu.CompilerParams(
            dimension_semantics=("arbitrary", "arbitrary"), vmem_limit_bytes=VMEM_LIMIT),
        name="merge_ln",
    )(att, cu, halo, gb, ga, gc, h1, conv_w8, wa, wc, wo, g, b)


def _rope_tables(pos):
    half = HEAD_DIM // 2
    inv_freq = ROPE_THETA ** (-jnp.arange(half, dtype=F32) / half)
    ang = pos.astype(F32)[:, None] * inv_freq[None, :]
    cos, sin = jnp.cos(ang), jnp.sin(ang)
    return jnp.tile(cos, (1, LANES // half)), jnp.tile(jnp.concatenate([-sin, sin], axis=1), (1, LANES // HEAD_DIM))


def _split_w_in(w_in):
    widths = (ATT_W, KV_W, KV_W, IDX_HEADS * IDX_DIM, IDX_DIM, IDX_HEADS,
              CONV_W, CONV_W, CONV_W, D_MODEL, D_MODEL)
    offs = np.cumsum(widths)[:-1].tolist()
    wq, wk, wv, wqi, wki, wwi, wu, wgb, wgc, wga, wgv = jnp.split(w_in, offs, axis=1)
    dup = lambda w: jnp.concatenate([w[:, :HEAD_DIM], w[:, :HEAD_DIM], w[:, HEAD_DIM:], w[:, HEAD_DIM:]], axis=1)
    w_a = jnp.concatenate(
        [wq, wqi, dup(wk), wki, wki, dup(wv), wwi, jnp.zeros((D_MODEL, LANES - IDX_HEADS), w_in.dtype)], axis=1)
    w_b = jnp.concatenate([wu, wgc, wgb, wga, wgv], axis=1)
    return w_a.astype(BF16), w_b.astype(BF16)


def kernel(x, meta_tokens, ffn1_w_gate, ffn1_w_up, ffn1_w_down, ln1_g, ln1_b, w_in, conv_w, w_att_out, w_conv_out, w_o, ln2_g, ln2_b, ffn2_w_gate, ffn2_w_up, ffn2_w_down, ln3_g, ln3_b):
    bsz, seq, _ = x.shape
    l_tot = N_META + seq
    topk = min(TOPK_MAX, l_tot // 4)
    lk = -(-l_tot // (UNROLL * CH)) * (UNROLL * CH)
    lyr = 0
    bf = lambda w: w.astype(BF16)
    vec = lambda v: v[lyr][None, :]

    w1 = (bf(ffn1_w_gate[lyr]), bf(ffn1_w_up[lyr]), bf(ffn1_w_down[lyr]), vec(ln1_g), vec(ln1_b))
    w_a, w_b = _split_w_in(w_in[lyr])
    pos = jnp.arange(l_tot, dtype=I32)
    cos, sin = _rope_tables(pos)

    hm = _ffn_ln(meta_tokens.astype(F32), *w1)
    _, _, kd_m, ki_m, vd_m, _ = _proj_rope(hm, w_a, cos[:N_META], sin[:N_META], N_META)
    cu_m = _proj_gate(hm, w_b)[0]

    h1 = _ffn_ln(x.reshape(bsz * seq, D_MODEL), *w1)
    q, qi, kd, ki2, vd, wi = _proj_rope(h1, w_a, cos[N_META:], sin[N_META:], seq)
    cu, gb, ga, gc = _proj_gate(h1, w_b)

    def with_meta(meta_rows, real):
        wdt = real.shape[-1]
        return jnp.concatenate(
            [jnp.broadcast_to(meta_rows[None], (bsz, N_META, wdt)), real.reshape(bsz, seq, wdt),
             jnp.zeros((bsz, lk - l_tot, wdt), real.dtype)], axis=1)

    per_seq = lambda a: a.reshape(bsz, seq, a.shape[-1])
    v_keys = with_meta(vd_m, vd)
    ones_pad = jnp.zeros((bsz, lk, VT_ROWS - HEAD_DIM), BF16).at[:, :, 0].set(1.0)
    vt = jnp.swapaxes(jnp.concatenate(
        [v_keys[:, :, :HEAD_DIM], ones_pad, v_keys[:, :, 2 * HEAD_DIM:3 * HEAD_DIM], ones_pad], axis=2), 1, 2)
    wit = jnp.swapaxes(per_seq(wi)[:, :, :IDX_HEADS], 1, 2)
    att = _dsa_attention(per_seq(qi), wit, per_seq(q), with_meta(ki_m, ki2), with_meta(kd_m, kd), vt, topk)

    tm = min(seq, 512)
    cu3 = per_seq(cu)
    tails = cu3.reshape(bsz, seq // tm, tm, CONV_W)[:, :-1, tm - 8:, :]
    halo = jnp.concatenate([jnp.broadcast_to(cu_m[None, None, N_META - 8:], (bsz, 1, 8, CONV_W)), tails], axis=1)
    conv_w8 = jnp.concatenate([conv_w[lyr].astype(F32), jnp.zeros((8 - CONV_K, CONV_W), F32)], axis=0)
    h2 = _merge_ln(att, cu3, halo, per_seq(gb), per_seq(ga), per_seq(gc), per_seq(h1), conv_w8,
                   bf(w_att_out[lyr]), bf(w_conv_out[lyr]), bf(w_o[lyr]), vec(ln2_g), vec(ln2_b))

    h3 = _ffn_ln(h2.reshape(bsz * seq, D_MODEL), bf(ffn2_w_gate[lyr]), bf(ffn2_w_up[lyr]), bf(ffn2_w_down[lyr]),
                 vec(ln3_g), vec(ln3_b))
    return h3.reshape(bsz, seq, D_MODEL)
```

```python
import functools

import numpy as np
import jax
import jax.numpy as jnp
from jax import lax
from jax.experimental import pallas as pl
from jax.experimental.pallas import tpu as pltpu

F32 = jnp.float32
BF16 = jnp.bfloat16
I32 = jnp.int32

D_MODEL = 1024
N_META = 16
ATT_HEADS = 8
ATT_KV_HEADS = 2
HEAD_DIM = 64
ATT_W = ATT_HEADS * HEAD_DIM
KV_W = ATT_KV_HEADS * HEAD_DIM
IDX_HEADS = 8
IDX_DIM = 64
TOPK_MAX = 256
ROPE_THETA = 10000.0
CONV_W = D_MODEL // 2
CONV_K = 3
D_FF = 2816
LN_EPS = 1e-5
DEPTH = 1
DEEPNORM_ALPHA = (2.0 * DEPTH) ** 0.25

LANES = 128
SUBLANES = 8
TQ = 256
CH = 256
VT_ROWS = 80
UNROLL = 4
STEPS_PER_EXIT_TEST = 2
UNTESTED_STEPS = 12
KEY_STEP_PERIOD = 4
MAX_SEARCH_STEPS = 160
INT_MAX = 2 ** 31 - 1
LOG2E = 1.4426950408889634
INT_MIN = -(2 ** 31)
NEG_BIG = -1e30
VMEM_LIMIT = 52 * 1024 * 1024


def _layer_norm(y, g, b):
    mu = jnp.mean(y, axis=-1, keepdims=True)
    yc = y - mu
    var = jnp.mean(yc * yc, axis=-1, keepdims=True)
    return yc * lax.rsqrt(var + LN_EPS) * g + b


def _ffn_ln_kernel(x_ref, wg_ref, wu_ref, wd_ref, g_ref, b_ref, o_ref, acc_ref, xb_ref):
    f = pl.program_id(1)

    @pl.when(f == 0)
    def _():
        acc_ref[...] = jnp.zeros_like(acc_ref)
        xb_ref[...] = x_ref[...].astype(BF16)

    xb = xb_ref[...]
    gt = jnp.dot(xb, wg_ref[...], preferred_element_type=F32)
    up = jnp.dot(xb, wu_ref[...], preferred_element_type=F32)
    a = (gt * jax.nn.sigmoid(gt)) * up
    acc_ref[...] += jnp.dot(a.astype(BF16), wd_ref[...], preferred_element_type=F32)

    @pl.when(f == pl.num_programs(1) - 1)
    def _():
        y = DEEPNORM_ALPHA * x_ref[...] + 0.5 * acc_ref[...]
        o_ref[...] = _layer_norm(y, g_ref[...], b_ref[...])


def _ffn_ln(x2d, wg, wu, wd, g, b):
    n = x2d.shape[0]
    tm = min(n, 1024)
    tf = 256
    assert n % tm == 0 and D_FF % tf == 0
    return pl.pallas_call(
        _ffn_ln_kernel,
        grid=(n // tm, D_FF // tf),
        in_specs=[
            pl.BlockSpec((tm, D_MODEL), lambda i, f: (i, 0)),
            pl.BlockSpec((D_MODEL, tf), lambda i, f: (0, f)),
            pl.BlockSpec((D_MODEL, tf), lambda i, f: (0, f)),
            pl.BlockSpec((tf, D_MODEL), lambda i, f: (f, 0)),
            pl.BlockSpec((1, D_MODEL), lambda i, f: (0, 0)),
            pl.BlockSpec((1, D_MODEL), lambda i, f: (0, 0)),
        ],
        out_specs=pl.BlockSpec((tm, D_MODEL), lambda i, f: (i, 0)),
        out_shape=jax.ShapeDtypeStruct((n, D_MODEL), F32),
        scratch_shapes=[pltpu.VMEM((tm, D_MODEL), F32), pltpu.VMEM((tm, D_MODEL), BF16)],
        compiler_params=pltpu.CompilerParams(
            dimension_semantics=("arbitrary", "arbitrary"), vmem_limit_bytes=VMEM_LIMIT),
        name="ffn_ln",
    )(x2d, wg, wu, wd, g, b)


_PROJ_A_W = ATT_W + IDX_HEADS * IDX_DIM + 2 * KV_W + LANES + 2 * KV_W + LANES


def _proj_rope_kernel(x_ref, w_ref, cos_ref, sin_ref, q_ref, qi_ref, kd_ref, ki_ref, vd_ref, wi_ref):
    tm = x_ref.shape[0]
    xb = x_ref[...].astype(BF16)
    cos = cos_ref[...]
    sin = sin_ref[...]
    lane = lax.broadcasted_iota(I32, (tm, LANES), 1)
    first_half = (lane & (HEAD_DIM // 2)) == 0

    def rope(s):
        partner = jnp.where(first_half, pltpu.roll(s, LANES - HEAD_DIM // 2, 1),
                            pltpu.roll(s, HEAD_DIM // 2, 1))
        return s * cos + partner * sin

    def dot_cols(c0, width):
        return jnp.dot(xb, w_ref[:, c0:c0 + width], preferred_element_type=F32)

    outs = ((q_ref, ATT_W, HEAD_DIM ** -0.5 * LOG2E), (qi_ref, IDX_HEADS * IDX_DIM, 1.0), (kd_ref, 2 * KV_W, 1.0))
    c0 = 0
    for ref, width, scale in outs:
        for j in range(0, width, 2 * LANES):
            p = dot_cols(c0 + j, 2 * LANES)
            for s in range(2):
                r = rope(p[:, s * LANES:(s + 1) * LANES])
                if scale != 1.0:
                    r = r * scale
                ref[:, j + s * LANES:j + (s + 1) * LANES] = r.astype(ref.dtype)
        c0 += width
    p = dot_cols(c0, LANES)
    ki_ref[...] = rope(p).astype(ki_ref.dtype)
    c0 += LANES
    vd_ref[...] = dot_cols(c0, 2 * KV_W).astype(vd_ref.dtype)
    c0 += 2 * KV_W
    wi_ref[...] = dot_cols(c0, LANES) * ((IDX_HEADS * IDX_DIM) ** -0.5)


def _proj_rope(h2d, w_a, cos, sin, rows_per_seq):
    n = h2d.shape[0]
    tm = min(rows_per_seq, 512)
    tiles_per_seq = rows_per_seq // tm
    assert n % tm == 0 and rows_per_seq % tm == 0
    row = lambda i: (i, 0)
    tab = lambda i: (i % tiles_per_seq, 0)
    widths = (ATT_W, IDX_HEADS * IDX_DIM, 2 * KV_W, LANES, 2 * KV_W, LANES)
    dtypes = (BF16, BF16, BF16, BF16, BF16, F32)
    return pl.pallas_call(
        _proj_rope_kernel,
        grid=(n // tm,),
        in_specs=[
            pl.BlockSpec((tm, D_MODEL), row),
            pl.BlockSpec((D_MODEL, _PROJ_A_W), lambda i: (0, 0)),
            pl.BlockSpec((tm, LANES), tab),
            pl.BlockSpec((tm, LANES), tab),
        ],
        out_specs=[pl.BlockSpec((tm, w), row) for w in widths],
        out_shape=[jax.ShapeDtypeStruct((n, w), dt) for w, dt in zip(widths, dtypes)],
        compiler_params=pltpu.CompilerParams(
            dimension_semantics=("arbitrary",), vmem_limit_bytes=VMEM_LIMIT),
        name="proj_rope",
    )(h2d, w_a, cos, sin)


_PROJ_B_W = 3 * CONV_W + 2 * D_MODEL


def _proj_gate_kernel(x_ref, w_ref, cu_ref, gb_ref, ga_ref, gc_ref):
    xb = x_ref[...].astype(BF16)

    def dot_cols(c0, width):
        return jnp.dot(xb, w_ref[:, c0:c0 + width], preferred_element_type=F32)

    cu_ref[...] = dot_cols(0, CONV_W) * dot_cols(CONV_W, CONV_W)
    gb_ref[...] = dot_cols(2 * CONV_W, CONV_W)
    ga_ref[...] = dot_cols(3 * CONV_W, D_MODEL)
    gc_ref[...] = dot_cols(3 * CONV_W + D_MODEL, D_MODEL)


def _proj_gate(h2d, w_b):
    n = h2d.shape[0]
    tm = min(n, 512)
    assert n % tm == 0
    row = lambda i: (i, 0)
    widths = (CONV_W, CONV_W, D_MODEL, D_MODEL)
    return pl.pallas_call(
        _proj_gate_kernel,
        grid=(n // tm,),
        in_specs=[pl.BlockSpec((tm, D_MODEL), row), pl.BlockSpec((D_MODEL, _PROJ_B_W), lambda i: (0, 0))],
        out_specs=[pl.BlockSpec((tm, w), row) for w in widths],
        out_shape=[jax.ShapeDtypeStruct((n, w), F32) for w in widths],
        compiler_params=pltpu.CompilerParams(
            dimension_semantics=("arbitrary",), vmem_limit_bytes=VMEM_LIMIT),
        name="proj_gate",
    )(h2d, w_b)


def _fold_rows(x, op):
    parts = [x[i:i + SUBLANES] for i in range(0, x.shape[0], SUBLANES)]
    while len(parts) > 1:
        parts = [op(parts[i], parts[i + 1]) for i in range(0, len(parts), 2)]
    return parts[0]


def _key_to_val(key):
    return pltpu.bitcast(jnp.where(key < 0, key ^ 0x7FFFFFFF, key), F32)


def _val_to_key(val):
    bits = pltpu.bitcast(jnp.where(val == 0.0, 0.0, val), I32)
    return jnp.where(bits < 0, bits ^ 0x7FFFFFFF, bits)


def _chunk_loop(nch, body, init):
    n_main = nch // UNROLL

    def main(i, carry):
        for u in range(UNROLL):
            carry = body(i * UNROLL + u, carry)
        return carry

    carry = lax.fori_loop(0, n_main, main, init)
    return lax.fori_loop(n_main * UNROLL, nch, body, carry)


def _attn_kernel(topk, qi_ref, wit_ref, q_ref, ki_ref, kd_ref, vt_ref, o_ref,
                 keys_ref, kmax_ref, stage_ref, p_ref, qit_ref, qt_ref, j_ref, acc_ref):
    jblk = pl.program_id(1)
    qpos0 = N_META + jblk * TQ
    nch = (qpos0 + TQ - 1) // CH + 1
    kf = float(topk)
    n_pairs = ATT_HEADS // 2

    lane = lax.broadcasted_iota(I32, (TQ, LANES), 1)
    lo_half = lane < HEAD_DIM
    qpos_row = qpos0 + lax.broadcasted_iota(I32, (1, TQ), 1)
    searchable = (qpos_row + 1).astype(F32) > kf

    for s in range(n_pairs):
        for src, dst in ((qi_ref, qit_ref), (q_ref, qt_ref)):
            slab = src[0, :, s * LANES:(s + 1) * LANES].astype(F32)
            even = jnp.where(lo_half, slab, 0.0).T
            odd = jnp.where(lo_half, 0.0, slab).T
            dst[s] = jnp.concatenate([even, odd], axis=1).astype(BF16)

    kpos_iota = lax.broadcasted_iota(I32, (CH, TQ), 0)
    qpos = qpos0 + lax.broadcasted_iota(I32, (CH, TQ), 1)

    def score_chunk(c, carry):
        off = pl.multiple_of(c * CH, CH)
        kic = ki_ref[0, pl.ds(off, CH), :]
        acc = jnp.zeros((CH, TQ), F32)
        for s in range(IDX_HEADS // 2):
            d = jnp.dot(kic, qit_ref[s], preferred_element_type=F32)
            for e in range(2):
                w = wit_ref[0, 2 * s + e:2 * s + e + 1, :]
                acc = acc + jnp.maximum(d[:, e * TQ:(e + 1) * TQ], 0.0) * w
        key = jnp.where(off + kpos_iota <= qpos, _val_to_key(acc), INT_MIN)
        keys_ref[pl.ds(off, CH), :] = key
        kmax_ref[...] = jnp.maximum(kmax_ref[...], key)
        return carry

    n_span = (nch + UNROLL - 1) // UNROLL

    def score_span(i, carry):
        for u in range(UNROLL):
            carry = score_chunk(i * UNROLL + u, carry)
        return carry

    kmax_ref[...] = jnp.full(kmax_ref.shape, INT_MIN, I32)
    lax.fori_loop(0, n_span, score_span, 0)

    def count(pred):
        def body(c, acc):
            off = pl.multiple_of(c * CH, CH)
            hit = jnp.where(pred(keys_ref[pl.ds(off, CH), :], off + kpos_iota), 1.0, 0.0)
            return acc + _fold_rows(hit, jnp.add)
        acc = _chunk_loop(nch, body, jnp.zeros((SUBLANES, TQ), F32))
        return jnp.sum(acc, axis=0, keepdims=True)

    kmax = kmax_ref[...]
    lo0 = jnp.min(_fold_rows(kmax, jnp.minimum), axis=0, keepdims=True)
    hi0 = jnp.minimum(jnp.max(_fold_rows(kmax, jnp.maximum), axis=0, keepdims=True), INT_MAX - 1) + 1

    def live(lo, hi, c_lo):
        return searchable & (c_lo != kf) & (hi - lo != 1)

    def n_live(lo, hi, c_lo):
        return jnp.max(jnp.where(live(lo, hi, c_lo), 1.0, 0.0))

    def pick(it, lo, hi, c_lo):
        top = hi - 1
        mid_val = _val_to_key(0.5 * _key_to_val(lo) + 0.5 * _key_to_val(top))
        mid_key = lo + lax.shift_right_logical(hi - lo, 1)
        cand = jnp.where(it % KEY_STEP_PERIOD == KEY_STEP_PERIOD - 1, mid_key, mid_val)
        cand = jnp.where((lo < 0) & (top >= 0), 0, cand)
        cand = jnp.where((lo == 0) & (top >= 1), 1, cand)
        cand = jnp.minimum(jnp.maximum(cand, lo + 1), top)
        return jnp.where(live(lo, hi, c_lo), cand, lo)

    def search_steps(st, n_steps):
        it, lo, hi, c_lo, c_hi = st
        for _ in range(n_steps):
            cand = pick(it, lo, hi, c_lo)
            cnt = count(lambda k, pos: k >= cand)
            ok = cnt >= kf
            lo = jnp.where(ok, cand, lo)
            c_lo = jnp.where(ok, cnt, c_lo)
            hi = jnp.where(ok, hi, cand)
            c_hi = jnp.where(ok, c_hi, cnt)
            it = it + 1
        return it, lo, hi, c_lo, c_hi

    def bis_cond(st):
        return (st[0] < MAX_SEARCH_STEPS) & (st[5] > 0.0)

    def bis_body(st):
        st = search_steps(st[:5], STEPS_PER_EXIT_TEST)
        return st + (n_live(st[1], st[2], st[3]),)

    c_unknown = jnp.full((1, TQ), 1e9, F32)
    st = (jnp.int32(0), lo0, hi0, c_unknown, jnp.zeros((1, TQ), F32))
    st = lax.fori_loop(0, UNTESTED_STEPS // STEPS_PER_EXIT_TEST,
                       lambda _, st: search_steps(st, STEPS_PER_EXIT_TEST), st)
    _, t_fin, _, ct_fin, c_above, _ = lax.while_loop(bis_cond, bis_body, st + (n_live(st[1], st[2], st[3]),))
    t_fin = jnp.where(searchable, t_fin, INT_MIN)
    j_ref[...] = jnp.broadcast_to(jnp.where(searchable, 2 ** 30, -1).astype(I32), j_ref.shape)
    open_fin = jnp.max(jnp.where(searchable & (ct_fin != kf), 1.0, 0.0))

    @pl.when(open_fin > 0.0)
    def _():
        tie_q = searchable & (ct_fin != kf)
        need = kf - c_above
        tri = jnp.where(lax.broadcasted_iota(I32, (CH, CH), 0) >= lax.broadcasted_iota(I32, (CH, CH), 1),
                        1.0, 0.0).astype(BF16)

        def body(c, carry):
            seen, jmax = carry
            off = pl.multiple_of(c * CH, CH)
            is_tie = keys_ref[pl.ds(off, CH), :] == t_fin
            rank = jnp.dot(tri, jnp.where(is_tie, 1.0, 0.0).astype(BF16), preferred_element_type=F32) + seen
            kept_pos = jnp.where(is_tie & (rank <= need), off + kpos_iota, -1)
            return rank[CH - 1:CH, :], jnp.maximum(jmax, _fold_rows(kept_pos, jnp.maximum))

        _, jmax = _chunk_loop(nch, body, (jnp.zeros((1, TQ), F32), jnp.full((SUBLANES, TQ), -1, I32)))
        j_tie = jnp.max(jmax, axis=0, keepdims=True)
        j_ref[...] = jnp.where(tie_q, jnp.broadcast_to(j_tie, j_ref.shape), j_ref[...])

    j_fin = j_ref[0:1, :]

    def logits(c, s):
        g = (2 * s) // (ATT_HEADS // ATT_KV_HEADS)
        kg = kd_ref[0, pl.ds(pl.multiple_of(c * CH, CH), CH), g * LANES:(g + 1) * LANES]
        return jnp.dot(kg, qt_ref[s], preferred_element_type=F32)

    def span_step(c_new, c_old, n_sub, m_old, new, old):
        m_loc = [jnp.full((SUBLANES, TQ), -jnp.inf, F32) for _ in range(ATT_HEADS)]
        for u in range(n_sub):
            rows = slice(u * CH, (u + 1) * CH)
            if new:
                off_new = pl.multiple_of((c_new + u) * CH, CH)
                k = keys_ref[pl.ds(off_new, CH), :]
                sel = (k > t_fin) | ((k == t_fin) & (off_new + kpos_iota <= j_fin))
                bias = jnp.where(sel, 0.0, NEG_BIG)
            for s in range(n_pairs):
                if old:
                    ps = [jnp.exp2(stage_ref[s, rows, e * TQ:(e + 1) * TQ] - m_old[2 * s + e]).astype(BF16)
                          for e in range(2)]
                    p_ref[s, rows, :] = jnp.concatenate(ps, axis=1)
                if new:
                    lt2 = logits(c_new + u, s)
                    for e in range(2):
                        h = 2 * s + e
                        lt = lt2[:, e * TQ:(e + 1) * TQ] + bias
                        stage_ref[s, rows, e * TQ:(e + 1) * TQ] = lt
                        m_loc[h] = jnp.maximum(m_loc[h], _fold_rows(lt, jnp.maximum))
        return m_loc

    def advance_max(m_run, m_loc):
        return tuple(jnp.maximum(m_run[h], jnp.max(m_loc[h], axis=0, keepdims=True)) for h in range(ATT_HEADS))

    def pv_span(c0, m_before, m_shift):
        def body(_, carry):
            off = pl.multiple_of(c0 * CH, UNROLL * CH)
            for s in range(n_pairs):
                g = (2 * s) // (ATT_HEADS // ATT_KV_HEADS)
                vgt = vt_ref[0, g * VT_ROWS:(g + 1) * VT_ROWS, pl.ds(off, UNROLL * CH)]
                pv = jnp.dot(vgt, p_ref[s], preferred_element_type=F32)
                for e in range(2):
                    h = 2 * s + e
                    cols = slice(e * TQ, (e + 1) * TQ)
                    acc_ref[s, :, cols] = jnp.exp2(m_before[h] - m_shift[h]) * acc_ref[s, :, cols] + pv[:, cols]
            return carry
        lax.fori_loop(0, jnp.minimum(n_span, 1), body, 0)

    def middle_span(i, carry):
        m_before, m_run = carry
        m_loc = span_step(i * UNROLL, (i - 1) * UNROLL, UNROLL, m_run, new=True, old=True)
        pv_span((i - 1) * UNROLL, m_before, m_run)
        return m_run, advance_max(m_run, m_loc)

    acc_ref[...] = jnp.zeros(acc_ref.shape, F32)
    m_none = tuple(jnp.full((1, TQ), -jnp.inf, F32) for _ in range(ATT_HEADS))
    m_first = advance_max(m_none, span_step(0, 0, UNROLL, None, new=True, old=False))
    m_before, m_fin = lax.fori_loop(1, n_span, middle_span, (m_none, m_first))
    span_step(0, (n_span - 1) * UNROLL, UNROLL, m_fin, new=False, old=True)
    pv_span((n_span - 1) * UNROLL, m_before, m_fin)

    for s in range(n_pairs):
        o_t = jnp.concatenate(
            [acc_ref[s, :HEAD_DIM, e * TQ:(e + 1) * TQ] / acc_ref[s, HEAD_DIM:HEAD_DIM + 1, e * TQ:(e + 1) * TQ]
             for e in range(2)], axis=0)
        o_ref[0, :, s * LANES:(s + 1) * LANES] = o_t.T.astype(o_ref.dtype)


def _dsa_attention(qi, wit, q, ki2, kd, vt, topk):
    b, s, _ = q.shape
    lk = kd.shape[1]
    assert s % TQ == 0 and lk % (UNROLL * CH) == 0 and lk >= N_META + s and topk <= CH
    qblk = lambda bi, j: (bi, j, 0)
    kblk = lambda bi, j: (bi, 0, 0)
    return pl.pallas_call(
        functools.partial(_attn_kernel, topk),
        grid=(b, s // TQ),
        in_specs=[
            pl.BlockSpec((1, TQ, IDX_HEADS * IDX_DIM), qblk),
            pl.BlockSpec((1, IDX_HEADS, TQ), lambda bi, j: (bi, 0, j)),
            pl.BlockSpec((1, TQ, ATT_W), qblk),
            pl.BlockSpec((1, lk, LANES), kblk),
            pl.BlockSpec((1, lk, 2 * KV_W), kblk),
            pl.BlockSpec((1, ATT_KV_HEADS * VT_ROWS, lk), kblk),
        ],
        out_specs=pl.BlockSpec((1, TQ, ATT_W), qblk),
        out_shape=jax.ShapeDtypeStruct((b, s, ATT_W), BF16),
        scratch_shapes=[
            pltpu.VMEM((lk, TQ), I32),
            pltpu.VMEM((CH, TQ), I32),
            pltpu.VMEM((ATT_HEADS // 2, UNROLL * CH, 2 * TQ), F32),
            pltpu.VMEM((ATT_HEADS // 2, UNROLL * CH, 2 * TQ), BF16),
            pltpu.VMEM((IDX_HEADS // 2, LANES, 2 * TQ), BF16),
            pltpu.VMEM((ATT_HEADS // 2, LANES, 2 * TQ), BF16),
            pltpu.VMEM((SUBLANES, TQ), I32),
            pltpu.VMEM((ATT_HEADS // 2, VT_ROWS, 2 * TQ), F32),
        ],
        compiler_params=pltpu.CompilerParams(
            dimension_semantics=("arbitrary", "arbitrary"), vmem_limit_bytes=VMEM_LIMIT),
        name="dsa_attn",
    )(qi, wit, q, ki2, kd, vt)


def _merge_ln_kernel(att_ref, cu_ref, halo_ref, gb_ref, ga_ref, gc_ref, h_ref, cw_ref,
                     wa_ref, wc_ref, wo_ref, g_ref, b_ref, o_ref):
    tm = cu_ref.shape[1]
    cu = cu_ref[0]
    halo = halo_ref[0, 0]
    row = lax.broadcasted_iota(I32, (tm, CONV_W), 0)
    prev1 = jnp.where(row == 0, halo[7:8], pltpu.roll(cu, 1, 0))
    prev2 = jnp.where(row == 0, halo[6:7], jnp.where(row == 1, halo[7:8], pltpu.roll(cu, 2, 0)))
    cw = cw_ref[...]
    conv = cw[0:1] * prev2 + cw[1:2] * prev1 + cw[2:3] * cu
    y_conv = jnp.dot((gb_ref[0] * conv).astype(BF16), wc_ref[...], preferred_element_type=F32)
    y_att = jnp.dot(att_ref[0], wa_ref[...], preferred_element_type=F32)
    merged = jax.nn.sigmoid(ga_ref[0]) * y_att + jax.nn.sigmoid(gc_ref[0]) * y_conv
    mix = jnp.dot(merged.astype(BF16), wo_ref[...], preferred_element_type=F32)
    o_ref[0] = _layer_norm(DEEPNORM_ALPHA * h_ref[0] + mix, g_ref[...], b_ref[...])


def _merge_ln(att, cu, halo, gb, ga, gc, h1, conv_w8, wa, wc, wo, g, b):
    bsz, s, _ = cu.shape
    tm = s // halo.shape[1]
    blk = lambda bi, i: (bi, i, 0)
    const = lambda bi, i: (0, 0)
    return pl.pallas_call(
        _merge_ln_kernel,
        grid=(bsz, s // tm),
        in_specs=[
            pl.BlockSpec((1, tm, ATT_W), blk),
            pl.BlockSpec((1, tm, CONV_W), blk),
            pl.BlockSpec((1, 1, 8, CONV_W), lambda bi, i: (bi, i, 0, 0)),
            pl.BlockSpec((1, tm, CONV_W), blk),
            pl.BlockSpec((1, tm, D_MODEL), blk),
            pl.BlockSpec((1, tm, D_MODEL), blk),
            pl.BlockSpec((1, tm, D_MODEL), blk),
            pl.BlockSpec((8, CONV_W), const),
            pl.BlockSpec((ATT_W, D_MODEL), const),
            pl.BlockSpec((CONV_W, D_MODEL), const),
            pl.BlockSpec((D_MODEL, D_MODEL), const),
            pl.BlockSpec((1, D_MODEL), const),
            pl.BlockSpec((1, D_MODEL), const),
        ],
        out_specs=pl.BlockSpec((1, tm, D_MODEL), blk),
        out_shape=jax.ShapeDtypeStruct((bsz, s, D_MODEL), F32),
        compiler_params=pltpu.CompilerParams(
            dimension_semantics=("arbitrary", "arbitrary"), vmem_limit_bytes=VMEM_LIMIT),
        name="merge_ln",
    )(att, cu, halo, gb, ga, gc, h1, conv_w8, wa, wc, wo, g, b)


def _rope_tables(pos):
    half = HEAD_DIM // 2
    inv_freq = ROPE_THETA ** (-jnp.arange(half, dtype=F32) / half)
    ang = pos.astype(F32)[:, None] * inv_freq[None, :]
    cos, sin = jnp.cos(ang), jnp.sin(ang)
    return jnp.tile(cos, (1, LANES // half)), jnp.tile(jnp.concatenate([-sin, sin], axis=1), (1, LANES // HEAD_DIM))


def _split_w_in(w_in):
    widths = (ATT_W, KV_W, KV_W, IDX_HEADS * IDX_DIM, IDX_DIM, IDX_HEADS,
              CONV_W, CONV_W, CONV_W, D_MODEL, D_MODEL)
    offs = np.cumsum(widths)[:-1].tolist()
    wq, wk, wv, wqi, wki, wwi, wu, wgb, wgc, wga, wgv = jnp.split(w_in, offs, axis=1)
    dup = lambda w: jnp.concatenate([w[:, :HEAD_DIM], w[:, :HEAD_DIM], w[:, HEAD_DIM:], w[:, HEAD_DIM:]], axis=1)
    w_a = jnp.concatenate(
        [wq, wqi, dup(wk), wki, wki, dup(wv), wwi, jnp.zeros((D_MODEL, LANES - IDX_HEADS), w_in.dtype)], axis=1)
    w_b = jnp.concatenate([wu, wgc, wgb, wga, wgv], axis=1)
    return w_a.astype(BF16), w_b.astype(BF16)


def kernel(x, meta_tokens, ffn1_w_gate, ffn1_w_up, ffn1_w_down, ln1_g, ln1_b, w_in, conv_w, w_att_out, w_conv_out, w_o, ln2_g, ln2_b, ffn2_w_gate, ffn2_w_up, ffn2_w_down, ln3_g, ln3_b):
    bsz, seq, _ = x.shape
    l_tot = N_META + seq
    topk = min(TOPK_MAX, l_tot // 4)
    lk = -(-l_tot // (UNROLL * CH)) * (UNROLL * CH)
    lyr = 0
    bf = lambda w: w.astype(BF16)
    vec = lambda v: v[lyr][None, :]

    w1 = (bf(ffn1_w_gate[lyr]), bf(ffn1_w_up[lyr]), bf(ffn1_w_down[lyr]), vec(ln1_g), vec(ln1_b))
    w_a, w_b = _split_w_in(w_in[lyr])
    pos = jnp.arange(l_tot, dtype=I32)
    cos, sin = _rope_tables(pos)

    hm = _ffn_ln(meta_tokens.astype(F32), *w1)
    _, _, kd_m, ki_m, vd_m, _ = _proj_rope(hm, w_a, cos[:N_META], sin[:N_META], N_META)
    cu_m = _proj_gate(hm, w_b)[0]

    h1 = _ffn_ln(x.reshape(bsz * seq, D_MODEL), *w1)
    q, qi, kd, ki2, vd, wi = _proj_rope(h1, w_a, cos[N_META:], sin[N_META:], seq)
    cu, gb, ga, gc = _proj_gate(h1, w_b)

    def with_meta(meta_rows, real):
        wdt = real.shape[-1]
        return jnp.concatenate(
            [jnp.broadcast_to(meta_rows[None], (bsz, N_META, wdt)), real.reshape(bsz, seq, wdt),
             jnp.zeros((bsz, lk - l_tot, wdt), real.dtype)], axis=1)

    per_seq = lambda a: a.reshape(bsz, seq, a.shape[-1])
    v_keys = with_meta(vd_m, vd)
    ones_pad = jnp.zeros((bsz, lk, VT_ROWS - HEAD_DIM), BF16).at[:, :, 0].set(1.0)
    vt = jnp.swapaxes(jnp.concatenate(
        [v_keys[:, :, :HEAD_DIM], ones_pad, v_keys[:, :, 2 * HEAD_DIM:3 * HEAD_DIM], ones_pad], axis=2), 1, 2)
    wit = jnp.swapaxes(per_seq(wi)[:, :, :IDX_HEADS], 1, 2)
    att = _dsa_attention(per_seq(qi), wit, per_seq(q), with_meta(ki_m, ki2), with_meta(kd_m, kd), vt, topk)

    tm = min(seq, 512)
    cu3 = per_seq(cu)
    tails = cu3.reshape(bsz, seq // tm, tm, CONV_W)[:, :-1, tm - 8:, :]
    halo = jnp.concatenate([jnp.broadcast_to(cu_m[None, None, N_META - 8:], (bsz, 1, 8, CONV_W)), tails], axis=1)
    conv_w8 = jnp.concatenate([conv_w[lyr].astype(F32), jnp.zeros((8 - CONV_K, CONV_W), F32)], axis=0)
    h2 = _merge_ln(att, cu3, halo, per_seq(gb), per_seq(ga), per_seq(gc), per_seq(h1), conv_w8,
                   bf(w_att_out[lyr]), bf(w_conv_out[lyr]), bf(w_o[lyr]), vec(ln2_g), vec(ln2_b))

    h3 = _ffn_ln(h2.reshape(bsz * seq, D_MODEL), bf(ffn2_w_gate[lyr]), bf(ffn2_w_up[lyr]), bf(ffn2_w_down[lyr]),
                 vec(ln3_g), vec(ln3_b))
    return h3.reshape(bsz, seq, D_MODEL)
```

```python
import functools
import math

import numpy as np
import jax
import jax.numpy as jnp
from jax import lax
from jax.experimental import pallas as pl
from jax.experimental.pallas import tpu as pltpu

F32 = jnp.float32
BF16 = jnp.bfloat16
I32 = jnp.int32

D_MODEL = 1024
N_META = 16
ATT_HEADS = 8
ATT_KV_HEADS = 2
HEAD_DIM = 64
ATT_W = ATT_HEADS * HEAD_DIM
KV_W = ATT_KV_HEADS * HEAD_DIM
IDX_HEADS = 8
IDX_DIM = 64
TOPK_MAX = 256
ROPE_THETA = 10000.0
CONV_W = D_MODEL // 2
CONV_K = 3
D_FF = 2816
LN_EPS = 1e-5
DEPTH = 1
DEEPNORM_ALPHA = (2.0 * DEPTH) ** 0.25

LANES = 128
SUBLANES = 8
TQ = 256
CH = 256
VT_ROWS = 80
UNROLL = 4
STEPS_PER_EXIT_TEST = 2
UNTESTED_STEPS = 12
INTERP_STEPS = 10
KEY_STEP_PERIOD = 4
MAX_SEARCH_STEPS = 160
INT_MAX = 2 ** 31 - 1
LOG2E = 1.4426950408889634
INT_MIN = -(2 ** 31)
NEG_BIG = -1e30
VMEM_LIMIT = 52 * 1024 * 1024


def _layer_norm(y, g, b):
    mu = jnp.mean(y, axis=-1, keepdims=True)
    yc = y - mu
    var = jnp.mean(yc * yc, axis=-1, keepdims=True)
    return yc * lax.rsqrt(var + LN_EPS) * g + b


def _ffn_ln_kernel(x_ref, wg_ref, wu_ref, wd_ref, g_ref, b_ref, o_ref, acc_ref, xb_ref):
    f = pl.program_id(1)

    @pl.when(f == 0)
    def _():
        acc_ref[...] = jnp.zeros_like(acc_ref)
        xb_ref[...] = x_ref[...].astype(BF16)

    xb = xb_ref[...]
    gt = jnp.dot(xb, wg_ref[...], preferred_element_type=F32)
    up = jnp.dot(xb, wu_ref[...], preferred_element_type=F32)
    a = (gt * jax.nn.sigmoid(gt)) * up
    acc_ref[...] += jnp.dot(a.astype(BF16), wd_ref[...], preferred_element_type=F32)

    @pl.when(f == pl.num_programs(1) - 1)
    def _():
        y = DEEPNORM_ALPHA * x_ref[...] + 0.5 * acc_ref[...]
        o_ref[...] = _layer_norm(y, g_ref[...], b_ref[...])


def _ffn_ln(x2d, wg, wu, wd, g, b):
    n = x2d.shape[0]
    tm = min(n, 1024)
    tf = 256
    assert n % tm == 0 and D_FF % tf == 0
    return pl.pallas_call(
        _ffn_ln_kernel,
        grid=(n // tm, D_FF // tf),
        in_specs=[
            pl.BlockSpec((tm, D_MODEL), lambda i, f: (i, 0)),
            pl.BlockSpec((D_MODEL, tf), lambda i, f: (0, f)),
            pl.BlockSpec((D_MODEL, tf), lambda i, f: (0, f)),
            pl.BlockSpec((tf, D_MODEL), lambda i, f: (f, 0)),
            pl.BlockSpec((1, D_MODEL), lambda i, f: (0, 0)),
            pl.BlockSpec((1, D_MODEL), lambda i, f: (0, 0)),
        ],
        out_specs=pl.BlockSpec((tm, D_MODEL), lambda i, f: (i, 0)),
        out_shape=jax.ShapeDtypeStruct((n, D_MODEL), F32),
        scratch_shapes=[pltpu.VMEM((tm, D_MODEL), F32), pltpu.VMEM((tm, D_MODEL), BF16)],
        compiler_params=pltpu.CompilerParams(
            dimension_semantics=("arbitrary", "arbitrary"), vmem_limit_bytes=VMEM_LIMIT),
        name="ffn_ln",
    )(x2d, wg, wu, wd, g, b)


_PROJ_A_W = ATT_W + IDX_HEADS * IDX_DIM + 2 * KV_W + LANES + 2 * KV_W + LANES


def _proj_rope_kernel(x_ref, w_ref, cos_ref, sin_ref, q_ref, qi_ref, kd_ref, ki_ref, vd_ref, wi_ref):
    tm = x_ref.shape[0]
    xb = x_ref[...].astype(BF16)
    cos = cos_ref[...]
    sin = sin_ref[...]
    lane = lax.broadcasted_iota(I32, (tm, LANES), 1)
    first_half = (lane & (HEAD_DIM // 2)) == 0

    def rope(s):
        partner = jnp.where(first_half, pltpu.roll(s, LANES - HEAD_DIM // 2, 1),
                            pltpu.roll(s, HEAD_DIM // 2, 1))
        return s * cos + partner * sin

    def dot_cols(c0, width):
        return jnp.dot(xb, w_ref[:, c0:c0 + width], preferred_element_type=F32)

    outs = ((q_ref, ATT_W, HEAD_DIM ** -0.5 * LOG2E), (qi_ref, IDX_HEADS * IDX_DIM, 1.0), (kd_ref, 2 * KV_W, 1.0))
    c0 = 0
    for ref, width, scale in outs:
        for j in range(0, width, 2 * LANES):
            p = dot_cols(c0 + j, 2 * LANES)
            for s in range(2):
                r = rope(p[:, s * LANES:(s + 1) * LANES])
                if scale != 1.0:
                    r = r * scale
                ref[:, j + s * LANES:j + (s + 1) * LANES] = r.astype(ref.dtype)
        c0 += width
    p = dot_cols(c0, LANES)
    ki_ref[...] = rope(p).astype(ki_ref.dtype)
    c0 += LANES
    vd_ref[...] = dot_cols(c0, 2 * KV_W).astype(vd_ref.dtype)
    c0 += 2 * KV_W
    wi_ref[...] = dot_cols(c0, LANES) * ((IDX_HEADS * IDX_DIM) ** -0.5)


def _proj_rope(h2d, w_a, cos, sin, rows_per_seq):
    n = h2d.shape[0]
    tm = min(rows_per_seq, 512)
    tiles_per_seq = rows_per_seq // tm
    assert n % tm == 0 and rows_per_seq % tm == 0
    row = lambda i: (i, 0)
    tab = lambda i: (i % tiles_per_seq, 0)
    widths = (ATT_W, IDX_HEADS * IDX_DIM, 2 * KV_W, LANES, 2 * KV_W, LANES)
    dtypes = (BF16, BF16, BF16, BF16, BF16, F32)
    return pl.pallas_call(
        _proj_rope_kernel,
        grid=(n // tm,),
        in_specs=[
            pl.BlockSpec((tm, D_MODEL), row),
            pl.BlockSpec((D_MODEL, _PROJ_A_W), lambda i: (0, 0)),
            pl.BlockSpec((tm, LANES), tab),
            pl.BlockSpec((tm, LANES), tab),
        ],
        out_specs=[pl.BlockSpec((tm, w), row) for w in widths],
        out_shape=[jax.ShapeDtypeStruct((n, w), dt) for w, dt in zip(widths, dtypes)],
        compiler_params=pltpu.CompilerParams(
            dimension_semantics=("arbitrary",), vmem_limit_bytes=VMEM_LIMIT),
        name="proj_rope",
    )(h2d, w_a, cos, sin)


_PROJ_B_W = 3 * CONV_W + 2 * D_MODEL


def _proj_gate_kernel(x_ref, w_ref, cu_ref, gb_ref, ga_ref, gc_ref):
    xb = x_ref[...].astype(BF16)

    def dot_cols(c0, width):
        return jnp.dot(xb, w_ref[:, c0:c0 + width], preferred_element_type=F32)

    cu_ref[...] = dot_cols(0, CONV_W) * dot_cols(CONV_W, CONV_W)
    gb_ref[...] = dot_cols(2 * CONV_W, CONV_W)
    ga_ref[...] = dot_cols(3 * CONV_W, D_MODEL)
    gc_ref[...] = dot_cols(3 * CONV_W + D_MODEL, D_MODEL)


def _proj_gate(h2d, w_b):
    n = h2d.shape[0]
    tm = min(n, 512)
    assert n % tm == 0
    row = lambda i: (i, 0)
    widths = (CONV_W, CONV_W, D_MODEL, D_MODEL)
    return pl.pallas_call(
        _proj_gate_kernel,
        grid=(n // tm,),
        in_specs=[pl.BlockSpec((tm, D_MODEL), row), pl.BlockSpec((D_MODEL, _PROJ_B_W), lambda i: (0, 0))],
        out_specs=[pl.BlockSpec((tm, w), row) for w in widths],
        out_shape=[jax.ShapeDtypeStruct((n, w), F32) for w in widths],
        compiler_params=pltpu.CompilerParams(
            dimension_semantics=("arbitrary",), vmem_limit_bytes=VMEM_LIMIT),
        name="proj_gate",
    )(h2d, w_b)


def _fold_rows(x, op):
    parts = [x[i:i + SUBLANES] for i in range(0, x.shape[0], SUBLANES)]
    while len(parts) > 1:
        parts = [op(parts[i], parts[i + 1]) for i in range(0, len(parts), 2)]
    return parts[0]


def _key_to_val(key):
    return pltpu.bitcast(jnp.where(key < 0, key ^ 0x7FFFFFFF, key), F32)


def _val_to_key(val):
    bits = pltpu.bitcast(jnp.where(val == 0.0, 0.0, val), I32)
    return jnp.where(bits < 0, bits ^ 0x7FFFFFFF, bits)


def _chunk_loop(nch, body, init):
    n_main = nch // UNROLL

    def main(i, carry):
        for u in range(UNROLL):
            carry = body(i * UNROLL + u, carry)
        return carry

    carry = lax.fori_loop(0, n_main, main, init)
    return lax.fori_loop(n_main * UNROLL, nch, body, carry)


def _attn_kernel(topk, qi_ref, wit_ref, q_ref, ki_ref, kd_ref, vt_ref, o_ref,
                 keys_ref, kmax_ref, stage_ref, p_ref, qit_ref, qt_ref, j_ref, acc_ref):
    jblk = pl.program_id(1)
    qpos0 = N_META + jblk * TQ
    nch = (qpos0 + TQ - 1) // CH + 1
    kf = float(topk)
    n_pairs = ATT_HEADS // 2

    lane = lax.broadcasted_iota(I32, (TQ, LANES), 1)
    lo_half = lane < HEAD_DIM
    qpos_row = qpos0 + lax.broadcasted_iota(I32, (1, TQ), 1)
    searchable = (qpos_row + 1).astype(F32) > kf

    for s in range(n_pairs):
        for src, dst in ((qi_ref, qit_ref), (q_ref, qt_ref)):
            slab = src[0, :, s * LANES:(s + 1) * LANES].astype(F32)
            even = jnp.where(lo_half, slab, 0.0).T
            odd = jnp.where(lo_half, 0.0, slab).T
            dst[s] = jnp.concatenate([even, odd], axis=1).astype(BF16)

    kpos_iota = lax.broadcasted_iota(I32, (CH, TQ), 0)
    qpos = qpos0 + lax.broadcasted_iota(I32, (CH, TQ), 1)

    def score_chunk(c, carry):
        off = pl.multiple_of(c * CH, CH)
        kic = ki_ref[0, pl.ds(off, CH), :]
        acc = jnp.zeros((CH, TQ), F32)
        for s in range(IDX_HEADS // 2):
            d = jnp.dot(kic, qit_ref[s], preferred_element_type=F32)
            for e in range(2):
                w = wit_ref[0, 2 * s + e:2 * s + e + 1, :]
                acc = acc + jnp.maximum(d[:, e * TQ:(e + 1) * TQ], 0.0) * w
        key = jnp.where(off + kpos_iota <= qpos, _val_to_key(acc), INT_MIN)
        keys_ref[pl.ds(off, CH), :] = key
        kmax_ref[...] = jnp.maximum(kmax_ref[...], key)
        return carry

    n_span = (nch + UNROLL - 1) // UNROLL

    def score_span(i, carry):
        for u in range(UNROLL):
            carry = score_chunk(i * UNROLL + u, carry)
        return carry

    kmax_ref[...] = jnp.full(kmax_ref.shape, INT_MIN, I32)
    lax.fori_loop(0, n_span, score_span, 0)

    def count(pred):
        def body(c, acc):
            off = pl.multiple_of(c * CH, CH)
            hit = jnp.where(pred(keys_ref[pl.ds(off, CH), :], off + kpos_iota), 1.0, 0.0)
            return acc + _fold_rows(hit, jnp.add)
        acc = _chunk_loop(nch, body, jnp.zeros((SUBLANES, TQ), F32))
        return jnp.sum(acc, axis=0, keepdims=True)

    kmax = kmax_ref[...]
    lo0 = jnp.min(_fold_rows(kmax, jnp.minimum), axis=0, keepdims=True)
    hi0 = jnp.minimum(jnp.max(_fold_rows(kmax, jnp.maximum), axis=0, keepdims=True), INT_MAX - 1) + 1

    def live(lo, hi, c_lo):
        return searchable & (c_lo != kf) & (hi - lo != 1)

    def n_live(lo, hi, c_lo):
        return jnp.max(jnp.where(live(lo, hi, c_lo), 1.0, 0.0))

    def pick(it, lo, hi, c_lo, c_hi):
        top = hi - 1
        v_lo, v_top = _key_to_val(lo), _key_to_val(top)
        log_lo = jnp.log(c_lo)
        frac = (log_lo - math.log(kf)) / (log_lo - jnp.log(jnp.maximum(c_hi, 0.5)))
        interpolate = (it >= 1) & (it <= INTERP_STEPS) & (c_lo < 1e8)
        frac = jnp.where(interpolate, jnp.minimum(jnp.maximum(frac, 0.1), 0.9), 0.5)
        mid_val = _val_to_key(v_lo + (v_top - v_lo) * frac)
        mid_key = lo + lax.shift_right_logical(hi - lo, 1)
        key_step = (it > INTERP_STEPS) & (it % KEY_STEP_PERIOD == KEY_STEP_PERIOD - 1)
        cand = jnp.where(key_step, mid_key, mid_val)
        cand = jnp.where((lo < 0) & (top >= 0), 0, cand)
        cand = jnp.where((lo == 0) & (top >= 1), 1, cand)
        cand = jnp.minimum(jnp.maximum(cand, lo + 1), top)
        return jnp.where(live(lo, hi, c_lo), cand, lo)

    def search_steps(st, n_steps):
        it, lo, hi, c_lo, c_hi = st
        for _ in range(n_steps):
            cand = pick(it, lo, hi, c_lo, c_hi)
            cnt = count(lambda k, pos: k >= cand)
            ok = cnt >= kf
            lo = jnp.where(ok, cand, lo)
            c_lo = jnp.where(ok, cnt, c_lo)
            hi = jnp.where(ok, hi, cand)
            c_hi = jnp.where(ok, c_hi, cnt)
            it = it + 1
        return it, lo, hi, c_lo, c_hi

    def bis_cond(st):
        return (st[0] < MAX_SEARCH_STEPS) & (st[5] > 0.0)

    def bis_body(st):
        st = search_steps(st[:5], STEPS_PER_EXIT_TEST)
        return st + (n_live(st[1], st[2], st[3]),)

    c_unknown = jnp.full((1, TQ), 1e9, F32)
    st = (jnp.int32(0), lo0, hi0, c_unknown, jnp.zeros((1, TQ), F32))
    st = lax.fori_loop(0, UNTESTED_STEPS // STEPS_PER_EXIT_TEST,
                       lambda _, st: search_steps(st, STEPS_PER_EXIT_TEST), st)
    _, t_fin, _, ct_fin, c_above, _ = lax.while_loop(bis_cond, bis_body, st + (n_live(st[1], st[2], st[3]),))
    t_fin = jnp.where(searchable, t_fin, INT_MIN)
    j_ref[...] = jnp.broadcast_to(jnp.where(searchable, 2 ** 30, -1).astype(I32), j_ref.shape)
    open_fin = jnp.max(jnp.where(searchable & (ct_fin != kf), 1.0, 0.0))

    @pl.when(open_fin > 0.0)
    def _():
        tie_q = searchable & (ct_fin != kf)
        need = kf - c_above
        tri = jnp.where(lax.broadcasted_iota(I32, (CH, CH), 0) >= lax.broadcasted_iota(I32, (CH, CH), 1),
                        1.0, 0.0).astype(BF16)

        def body(c, carry):
            seen, jmax = carry
            off = pl.multiple_of(c * CH, CH)
            is_tie = keys_ref[pl.ds(off, CH), :] == t_fin
            rank = jnp.dot(tri, jnp.where(is_tie, 1.0, 0.0).astype(BF16), preferred_element_type=F32) + seen
            kept_pos = jnp.where(is_tie & (rank <= need), off + kpos_iota, -1)
            return rank[CH - 1:CH, :], jnp.maximum(jmax, _fold_rows(kept_pos, jnp.maximum))

        _, jmax = _chunk_loop(nch, body, (jnp.zeros((1, TQ), F32), jnp.full((SUBLANES, TQ), -1, I32)))
        j_tie = jnp.max(jmax, axis=0, keepdims=True)
        j_ref[...] = jnp.where(tie_q, jnp.broadcast_to(j_tie, j_ref.shape), j_ref[...])

    j_fin = j_ref[0:1, :]

    def logits(c, s):
        g = (2 * s) // (ATT_HEADS // ATT_KV_HEADS)
        kg = kd_ref[0, pl.ds(pl.multiple_of(c * CH, CH), CH), g * LANES:(g + 1) * LANES]
        return jnp.dot(kg, qt_ref[s], preferred_element_type=F32)

    def span_step(c_new, c_old, n_sub, m_old, new, old):
        m_loc = [jnp.full((SUBLANES, TQ), -jnp.inf, F32) for _ in range(ATT_HEADS)]
        for u in range(n_sub):
            rows = slice(u * CH, (u + 1) * CH)
            if new:
                off_new = pl.multiple_of((c_new + u) * CH, CH)
                k = keys_ref[pl.ds(off_new, CH), :]
                sel = (k > t_fin) | ((k == t_fin) & (off_new + kpos_iota <= j_fin))
                bias = jnp.where(sel, 0.0, NEG_BIG)
            for s in range(n_pairs):
                if old:
                    ps = [jnp.exp2(stage_ref[s, rows, e * TQ:(e + 1) * TQ] - m_old[2 * s + e]).astype(BF16)
                          for e in range(2)]
                    p_ref[s, rows, :] = jnp.concatenate(ps, axis=1)
                if new:
                    lt2 = logits(c_new + u, s)
                    for e in range(2):
                        h = 2 * s + e
                        lt = lt2[:, e * TQ:(e + 1) * TQ] + bias
                        stage_ref[s, rows, e * TQ:(e + 1) * TQ] = lt
                        m_loc[h] = jnp.maximum(m_loc[h], _fold_rows(lt, jnp.maximum))
        return m_loc

    def advance_max(m_run, m_loc):
        return tuple(jnp.maximum(m_run[h], jnp.max(m_loc[h], axis=0, keepdims=True)) for h in range(ATT_HEADS))

    def pv_span(c0, m_before, m_shift):
        def body(_, carry):
            off = pl.multiple_of(c0 * CH, UNROLL * CH)
            for s in range(n_pairs):
                g = (2 * s) // (ATT_HEADS // ATT_KV_HEADS)
                vgt = vt_ref[0, g * VT_ROWS:(g + 1) * VT_ROWS, pl.ds(off, UNROLL * CH)]
                pv = jnp.dot(vgt, p_ref[s], preferred_element_type=F32)
                for e in range(2):
                    h = 2 * s + e
                    cols = slice(e * TQ, (e + 1) * TQ)
                    acc_ref[s, :, cols] = jnp.exp2(m_before[h] - m_shift[h]) * acc_ref[s, :, cols] + pv[:, cols]
            return carry
        lax.fori_loop(0, jnp.minimum(n_span, 1), body, 0)

    def middle_span(i, carry):
        m_before, m_run = carry
        m_loc = span_step(i * UNROLL, (i - 1) * UNROLL, UNROLL, m_run, new=True, old=True)
        pv_span((i - 1) * UNROLL, m_before, m_run)
        return m_run, advance_max(m_run, m_loc)

    acc_ref[...] = jnp.zeros(acc_ref.shape, F32)
    m_none = tuple(jnp.full((1, TQ), -jnp.inf, F32) for _ in range(ATT_HEADS))
    m_first = advance_max(m_none, span_step(0, 0, UNROLL, None, new=True, old=False))
    m_before, m_fin = lax.fori_loop(1, n_span, middle_span, (m_none, m_first))
    span_step(0, (n_span - 1) * UNROLL, UNROLL, m_fin, new=False, old=True)
    pv_span((n_span - 1) * UNROLL, m_before, m_fin)

    for s in range(n_pairs):
        o_t = jnp.concatenate(
            [acc_ref[s, :HEAD_DIM, e * TQ:(e + 1) * TQ] / acc_ref[s, HEAD_DIM:HEAD_DIM + 1, e * TQ:(e + 1) * TQ]
             for e in range(2)], axis=0)
        o_ref[0, :, s * LANES:(s + 1) * LANES] = o_t.T.astype(o_ref.dtype)


def _dsa_attention(qi, wit, q, ki2, kd, vt, topk):
    b, s, _ = q.shape
    lk = kd.shape[1]
    assert s % TQ == 0 and lk % (UNROLL * CH) == 0 and lk >= N_META + s and topk <= CH
    qblk = lambda bi, j: (bi, j, 0)
    kblk = lambda bi, j: (bi, 0, 0)
    return pl.pallas_call(
        functools.partial(_attn_kernel, topk),
        grid=(b, s // TQ),
        in_specs=[
            pl.BlockSpec((1, TQ, IDX_HEADS * IDX_DIM), qblk),
            pl.BlockSpec((1, IDX_HEADS, TQ), lambda bi, j: (bi, 0, j)),
            pl.BlockSpec((1, TQ, ATT_W), qblk),
            pl.BlockSpec((1, lk, LANES), kblk),
            pl.BlockSpec((1, lk, 2 * KV_W), kblk),
            pl.BlockSpec((1, ATT_KV_HEADS * VT_ROWS, lk), kblk),
        ],
        out_specs=pl.BlockSpec((1, TQ, ATT_W), qblk),
        out_shape=jax.ShapeDtypeStruct((b, s, ATT_W), BF16),
        scratch_shapes=[
            pltpu.VMEM((lk, TQ), I32),
            pltpu.VMEM((CH, TQ), I32),
            pltpu.VMEM((ATT_HEADS // 2, UNROLL * CH, 2 * TQ), F32),
            pltpu.VMEM((ATT_HEADS // 2, UNROLL * CH, 2 * TQ), BF16),
            pltpu.VMEM((IDX_HEADS // 2, LANES, 2 * TQ), BF16),
            pltpu.VMEM((ATT_HEADS // 2, LANES, 2 * TQ), BF16),
            pltpu.VMEM((SUBLANES, TQ), I32),
            pltpu.VMEM((ATT_HEADS // 2, VT_ROWS, 2 * TQ), F32),
        ],
        compiler_params=pltpu.CompilerParams(
            dimension_semantics=("arbitrary", "arbitrary"), vmem_limit_bytes=VMEM_LIMIT),
        name="dsa_attn",
    )(qi, wit, q, ki2, kd, vt)


def _merge_ln_kernel(att_ref, cu_ref, halo_ref, gb_ref, ga_ref, gc_ref, h_ref, cw_ref,
                     wa_ref, wc_ref, wo_ref, g_ref, b_ref, o_ref):
    tm = cu_ref.shape[1]
    cu = cu_ref[0]
    halo = halo_ref[0, 0]
    row = lax.broadcasted_iota(I32, (tm, CONV_W), 0)
    prev1 = jnp.where(row == 0, halo[7:8], pltpu.roll(cu, 1, 0))
    prev2 = jnp.where(row == 0, halo[6:7], jnp.where(row == 1, halo[7:8], pltpu.roll(cu, 2, 0)))
    cw = cw_ref[...]
    conv = cw[0:1] * prev2 + cw[1:2] * prev1 + cw[2:3] * cu
    y_conv = jnp.dot((gb_ref[0] * conv).astype(BF16), wc_ref[...], preferred_element_type=F32)
    y_att = jnp.dot(att_ref[0], wa_ref[...], preferred_element_type=F32)
    merged = jax.nn.sigmoid(ga_ref[0]) * y_att + jax.nn.sigmoid(gc_ref[0]) * y_conv
    mix = jnp.dot(merged.astype(BF16), wo_ref[...], preferred_element_type=F32)
    o_ref[0] = _layer_norm(DEEPNORM_ALPHA * h_ref[0] + mix, g_ref[...], b_ref[...])


def _merge_ln(att, cu, halo, gb, ga, gc, h1, conv_w8, wa, wc, wo, g, b):
    bsz, s, _ = cu.shape
    tm = s // halo.shape[1]
    blk = lambda bi, i: (bi, i, 0)
    const = lambda bi, i: (0, 0)
    return pl.pallas_call(
        _merge_ln_kernel,
        grid=(bsz, s // tm),
        in_specs=[
            pl.BlockSpec((1, tm, ATT_W), blk),
            pl.BlockSpec((1, tm, CONV_W), blk),
            pl.BlockSpec((1, 1, 8, CONV_W), lambda bi, i: (bi, i, 0, 0)),
            pl.BlockSpec((1, tm, CONV_W), blk),
            pl.BlockSpec((1, tm, D_MODEL), blk),
            pl.BlockSpec((1, tm, D_MODEL), blk),
            pl.BlockSpec((1, tm, D_MODEL), blk),
            pl.BlockSpec((8, CONV_W), const),
            pl.BlockSpec((ATT_W, D_MODEL), const),
            pl.BlockSpec((CONV_W, D_MODEL), const),
            pl.BlockSpec((D_MODEL, D_MODEL), const),
            pl.BlockSpec((1, D_MODEL), const),
            pl.BlockSpec((1, D_MODEL), const),
        ],
        out_specs=pl.BlockSpec((1, tm, D_MODEL), blk),
        out_shape=jax.ShapeDtypeStruct((bsz, s, D_MODEL), F32),
        compiler_params=pltpu.CompilerParams(
            dimension_semantics=("arbitrary", "arbitrary"), vmem_limit_bytes=VMEM_LIMIT),
        name="merge_ln",
    )(att, cu, halo, gb, ga, gc, h1, conv_w8, wa, wc, wo, g, b)


def _rope_tables(pos):
    half = HEAD_DIM // 2
    inv_freq = ROPE_THETA ** (-jnp.arange(half, dtype=F32) / half)
    ang = pos.astype(F32)[:, None] * inv_freq[None, :]
    cos, sin = jnp.cos(ang), jnp.sin(ang)
    return jnp.tile(cos, (1, LANES // half)), jnp.tile(jnp.concatenate([-sin, sin], axis=1), (1, LANES // HEAD_DIM))


def _split_w_in(w_in):
    widths = (ATT_W, KV_W, KV_W, IDX_HEADS * IDX_DIM, IDX_DIM, IDX_HEADS,
              CONV_W, CONV_W, CONV_W, D_MODEL, D_MODEL)
    offs = np.cumsum(widths)[:-1].tolist()
    wq, wk, wv, wqi, wki, wwi, wu, wgb, wgc, wga, wgv = jnp.split(w_in, offs, axis=1)
    dup = lambda w: jnp.concatenate([w[:, :HEAD_DIM], w[:, :HEAD_DIM], w[:, HEAD_DIM:], w[:, HEAD_DIM:]], axis=1)
    w_a = jnp.concatenate(
        [wq, wqi, dup(wk), wki, wki, dup(wv), wwi, jnp.zeros((D_MODEL, LANES - IDX_HEADS), w_in.dtype)], axis=1)
    w_b = jnp.concatenate([wu, wgc, wgb, wga, wgv], axis=1)
    return w_a.astype(BF16), w_b.astype(BF16)


def kernel(x, meta_tokens, ffn1_w_gate, ffn1_w_up, ffn1_w_down, ln1_g, ln1_b, w_in, conv_w, w_att_out, w_conv_out, w_o, ln2_g, ln2_b, ffn2_w_gate, ffn2_w_up, ffn2_w_down, ln3_g, ln3_b):
    bsz, seq, _ = x.shape
    l_tot = N_META + seq
    topk = min(TOPK_MAX, l_tot // 4)
    lk = -(-l_tot // (UNROLL * CH)) * (UNROLL * CH)
    lyr = 0
    bf = lambda w: w.astype(BF16)
    vec = lambda v: v[lyr][None, :]

    w1 = (bf(ffn1_w_gate[lyr]), bf(ffn1_w_up[lyr]), bf(ffn1_w_down[lyr]), vec(ln1_g), vec(ln1_b))
    w_a, w_b = _split_w_in(w_in[lyr])
    pos = jnp.arange(l_tot, dtype=I32)
    cos, sin = _rope_tables(pos)

    hm = _ffn_ln(meta_tokens.astype(F32), *w1)
    _, _, kd_m, ki_m, vd_m, _ = _proj_rope(hm, w_a, cos[:N_META], sin[:N_META], N_META)
    cu_m = _proj_gate(hm, w_b)[0]

    h1 = _ffn_ln(x.reshape(bsz * seq, D_MODEL), *w1)
    q, qi, kd, ki2, vd, wi = _proj_rope(h1, w_a, cos[N_META:], sin[N_META:], seq)
    cu, gb, ga, gc = _proj_gate(h1, w_b)

    def with_meta(meta_rows, real):
        wdt = real.shape[-1]
        return jnp.concatenate(
            [jnp.broadcast_to(meta_rows[None], (bsz, N_META, wdt)), real.reshape(bsz, seq, wdt),
             jnp.zeros((bsz, lk - l_tot, wdt), real.dtype)], axis=1)

    per_seq = lambda a: a.reshape(bsz, seq, a.shape[-1])
    v_keys = with_meta(vd_m, vd)
    ones_pad = jnp.zeros((bsz, lk, VT_ROWS - HEAD_DIM), BF16).at[:, :, 0].set(1.0)
    vt = jnp.swapaxes(jnp.concatenate(
        [v_keys[:, :, :HEAD_DIM], ones_pad, v_keys[:, :, 2 * HEAD_DIM:3 * HEAD_DIM], ones_pad], axis=2), 1, 2)
    wit = jnp.swapaxes(per_seq(wi)[:, :, :IDX_HEADS], 1, 2)
    att = _dsa_attention(per_seq(qi), wit, per_seq(q), with_meta(ki_m, ki2), with_meta(kd_m, kd), vt, topk)

    tm = min(seq, 512)
    cu3 = per_seq(cu)
    tails = cu3.reshape(bsz, seq // tm, tm, CONV_W)[:, :-1, tm - 8:, :]
    halo = jnp.concatenate([jnp.broadcast_to(cu_m[None, None, N_META - 8:], (bsz, 1, 8, CONV_W)), tails], axis=1)
    conv_w8 = jnp.concatenate([conv_w[lyr].astype(F32), jnp.zeros((8 - CONV_K, CONV_W), F32)], axis=0)
    h2 = _merge_ln(att, cu3, halo, per_seq(gb), per_seq(ga), per_seq(gc), per_seq(h1), conv_w8,
                   bf(w_att_out[lyr]), bf(w_conv_out[lyr]), bf(w_o[lyr]), vec(ln2_g), vec(ln2_b))

    h3 = _ffn_ln(h2.reshape(bsz * seq, D_MODEL), bf(ffn2_w_gate[lyr]), bf(ffn2_w_up[lyr]), bf(ffn2_w_down[lyr]),
                 vec(ln3_g), vec(ln3_b))
    return h3.reshape(bsz, seq, D_MODEL)
```

```python
import functools
import math

import numpy as np
import jax
import jax.numpy as jnp
from jax import lax
from jax.experimental import pallas as pl
from jax.experimental.pallas import tpu as pltpu

F32 = jnp.float32
BF16 = jnp.bfloat16
I32 = jnp.int32

D_MODEL = 1024
N_META = 16
ATT_HEADS = 8
ATT_KV_HEADS = 2
HEAD_DIM = 64
ATT_W = ATT_HEADS * HEAD_DIM
KV_W = ATT_KV_HEADS * HEAD_DIM
IDX_HEADS = 8
IDX_DIM = 64
TOPK_MAX = 256
ROPE_THETA = 10000.0
CONV_W = D_MODEL // 2
CONV_K = 3
D_FF = 2816
LN_EPS = 1e-5
DEPTH = 1
DEEPNORM_ALPHA = (2.0 * DEPTH) ** 0.25

LANES = 128
SUBLANES = 8
TQ = 256
CH = 256
VT_ROWS = 80
UNROLL = 4
STEPS_PER_EXIT_TEST = 2
UNTESTED_STEPS = 12
INTERP_STEPS = 10
KEY_STEP_PERIOD = 4
MAX_SEARCH_STEPS = 160
INT_MAX = 2 ** 31 - 1
LOG2E = 1.4426950408889634
INT_MIN = -(2 ** 31)
NEG_BIG = -1e30
VMEM_LIMIT = 52 * 1024 * 1024


def _layer_norm(y, g, b):
    mu = jnp.mean(y, axis=-1, keepdims=True)
    yc = y - mu
    var = jnp.mean(yc * yc, axis=-1, keepdims=True)
    return yc * lax.rsqrt(var + LN_EPS) * g + b


def _ffn_ln_kernel(x_ref, wg_ref, wu_ref, wd_ref, g_ref, b_ref, o_ref, xb_ref):
    f = pl.program_id(1)

    @pl.when(f == 0)
    def _():
        o_ref[...] = jnp.zeros_like(o_ref)
        xb_ref[...] = x_ref[...].astype(BF16)

    xb = xb_ref[...]
    gt = jnp.dot(xb, wg_ref[...], preferred_element_type=F32)
    up = jnp.dot(xb, wu_ref[...], preferred_element_type=F32)
    a = (gt * jax.nn.sigmoid(gt)) * up
    o_ref[...] += jnp.dot(a.astype(BF16), wd_ref[...], preferred_element_type=F32)

    @pl.when(f == pl.num_programs(1) - 1)
    def _():
        tr = min(x_ref.shape[0], 256)

        def norm_rows(r, carry):
            rows = pl.ds(pl.multiple_of(r * tr, tr), tr)
            y = DEEPNORM_ALPHA * x_ref[rows, :] + 0.5 * o_ref[rows, :]
            o_ref[rows, :] = _layer_norm(y, g_ref[...], b_ref[...])
            return carry

        lax.fori_loop(0, x_ref.shape[0] // tr, norm_rows, 0)


def _ffn_ln(x2d, wg, wu, wd, g, b):
    n = x2d.shape[0]
    tm = min(n, 2048)
    tf = 256
    assert n % tm == 0 and D_FF % tf == 0
    return pl.pallas_call(
        _ffn_ln_kernel,
        grid=(n // tm, D_FF // tf),
        in_specs=[
            pl.BlockSpec((tm, D_MODEL), lambda i, f: (i, 0)),
            pl.BlockSpec((D_MODEL, tf), lambda i, f: (0, f)),
            pl.BlockSpec((D_MODEL, tf), lambda i, f: (0, f)),
            pl.BlockSpec((tf, D_MODEL), lambda i, f: (f, 0)),
            pl.BlockSpec((1, D_MODEL), lambda i, f: (0, 0)),
            pl.BlockSpec((1, D_MODEL), lambda i, f: (0, 0)),
        ],
        out_specs=pl.BlockSpec((tm, D_MODEL), lambda i, f: (i, 0)),
        out_shape=jax.ShapeDtypeStruct((n, D_MODEL), F32),
        scratch_shapes=[pltpu.VMEM((tm, D_MODEL), BF16)],
        compiler_params=pltpu.CompilerParams(
            dimension_semantics=("arbitrary", "arbitrary"), vmem_limit_bytes=VMEM_LIMIT),
        name="ffn_ln",
    )(x2d, wg, wu, wd, g, b)


_PROJ_A_W = ATT_W + IDX_HEADS * IDX_DIM + 2 * KV_W + LANES + 2 * KV_W + LANES


def _proj_rope_kernel(x_ref, w_ref, cos_ref, sin_ref, q_ref, qi_ref, kd_ref, ki_ref, vd_ref, wi_ref):
    tm = x_ref.shape[0]
    xb = x_ref[...].astype(BF16)
    cos = cos_ref[...]
    sin = sin_ref[...]
    lane = lax.broadcasted_iota(I32, (tm, LANES), 1)
    first_half = (lane & (HEAD_DIM // 2)) == 0

    def rope(s):
        partner = jnp.where(first_half, pltpu.roll(s, LANES - HEAD_DIM // 2, 1),
                            pltpu.roll(s, HEAD_DIM // 2, 1))
        return s * cos + partner * sin

    def dot_cols(c0, width):
        return jnp.dot(xb, w_ref[:, c0:c0 + width], preferred_element_type=F32)

    outs = ((q_ref, ATT_W, HEAD_DIM ** -0.5 * LOG2E), (qi_ref, IDX_HEADS * IDX_DIM, 1.0), (kd_ref, 2 * KV_W, 1.0))
    c0 = 0
    for ref, width, scale in outs:
        for j in range(0, width, 2 * LANES):
            p = dot_cols(c0 + j, 2 * LANES)
            for s in range(2):
                r = rope(p[:, s * LANES:(s + 1) * LANES])
                if scale != 1.0:
                    r = r * scale
                ref[:, j + s * LANES:j + (s + 1) * LANES] = r.astype(ref.dtype)
        c0 += width
    p = dot_cols(c0, LANES)
    ki_ref[...] = rope(p).astype(ki_ref.dtype)
    c0 += LANES
    vd_ref[...] = dot_cols(c0, 2 * KV_W).astype(vd_ref.dtype)
    c0 += 2 * KV_W
    wi_ref[...] = dot_cols(c0, LANES) * ((IDX_HEADS * IDX_DIM) ** -0.5)


def _proj_rope(h2d, w_a, cos, sin, rows_per_seq):
    n = h2d.shape[0]
    tm = min(rows_per_seq, 512)
    tiles_per_seq = rows_per_seq // tm
    assert n % tm == 0 and rows_per_seq % tm == 0
    row = lambda i: (i, 0)
    tab = lambda i: (i % tiles_per_seq, 0)
    widths = (ATT_W, IDX_HEADS * IDX_DIM, 2 * KV_W, LANES, 2 * KV_W, LANES)
    dtypes = (BF16, BF16, BF16, BF16, BF16, F32)
    return pl.pallas_call(
        _proj_rope_kernel,
        grid=(n // tm,),
        in_specs=[
            pl.BlockSpec((tm, D_MODEL), row),
            pl.BlockSpec((D_MODEL, _PROJ_A_W), lambda i: (0, 0)),
            pl.BlockSpec((tm, LANES), tab),
            pl.BlockSpec((tm, LANES), tab),
        ],
        out_specs=[pl.BlockSpec((tm, w), row) for w in widths],
        out_shape=[jax.ShapeDtypeStruct((n, w), dt) for w, dt in zip(widths, dtypes)],
        compiler_params=pltpu.CompilerParams(
            dimension_semantics=("arbitrary",), vmem_limit_bytes=VMEM_LIMIT),
        name="proj_rope",
    )(h2d, w_a, cos, sin)


_PROJ_B_W = 3 * CONV_W + 2 * D_MODEL


def _proj_gate_kernel(x_ref, w_ref, cu_ref, gb_ref, ga_ref, gc_ref):
    xb = x_ref[...].astype(BF16)

    def dot_cols(c0, width):
        return jnp.dot(xb, w_ref[:, c0:c0 + width], preferred_element_type=F32)

    cu_ref[...] = dot_cols(0, CONV_W) * dot_cols(CONV_W, CONV_W)
    gb_ref[...] = dot_cols(2 * CONV_W, CONV_W)
    ga_ref[...] = dot_cols(3 * CONV_W, D_MODEL)
    gc_ref[...] = dot_cols(3 * CONV_W + D_MODEL, D_MODEL)


def _proj_gate(h2d, w_b):
    n = h2d.shape[0]
    tm = min(n, 512)
    assert n % tm == 0
    row = lambda i: (i, 0)
    widths = (CONV_W, CONV_W, D_MODEL, D_MODEL)
    return pl.pallas_call(
        _proj_gate_kernel,
        grid=(n // tm,),
        in_specs=[pl.BlockSpec((tm, D_MODEL), row), pl.BlockSpec((D_MODEL, _PROJ_B_W), lambda i: (0, 0))],
        out_specs=[pl.BlockSpec((tm, w), row) for w in widths],
        out_shape=[jax.ShapeDtypeStruct((n, w), F32) for w in widths],
        compiler_params=pltpu.CompilerParams(
            dimension_semantics=("arbitrary",), vmem_limit_bytes=VMEM_LIMIT),
        name="proj_gate",
    )(h2d, w_b)


def _fold_rows(x, op):
    parts = [x[i:i + SUBLANES] for i in range(0, x.shape[0], SUBLANES)]
    while len(parts) > 1:
        parts = [op(parts[i], parts[i + 1]) for i in range(0, len(parts), 2)]
    return parts[0]


def _key_to_val(key):
    return pltpu.bitcast(jnp.where(key < 0, key ^ 0x7FFFFFFF, key), F32)


def _val_to_key(val):
    bits = pltpu.bitcast(jnp.where(val == 0.0, 0.0, val), I32)
    return jnp.where(bits < 0, bits ^ 0x7FFFFFFF, bits)


def _chunk_loop(nch, body, init):
    n_main = nch // UNROLL

    def main(i, carry):
        for u in range(UNROLL):
            carry = body(i * UNROLL + u, carry)
        return carry

    carry = lax.fori_loop(0, n_main, main, init)
    return lax.fori_loop(n_main * UNROLL, nch, body, carry)


def _attn_kernel(topk, qi_ref, wit_ref, q_ref, ki_ref, kd_ref, vt_ref, o_ref,
                 keys_ref, kmax_ref, stage_ref, p_ref, qit_ref, qt_ref, j_ref, acc_ref):
    jblk = pl.program_id(1)
    qpos0 = N_META + jblk * TQ
    nch = (qpos0 + TQ - 1) // CH + 1
    kf = float(topk)
    n_pairs = ATT_HEADS // 2

    lane = lax.broadcasted_iota(I32, (TQ, LANES), 1)
    lo_half = lane < HEAD_DIM
    qpos_row = qpos0 + lax.broadcasted_iota(I32, (1, TQ), 1)
    searchable = (qpos_row + 1).astype(F32) > kf

    for s in range(n_pairs):
        for src, dst in ((qi_ref, qit_ref), (q_ref, qt_ref)):
            slab = src[0, :, s * LANES:(s + 1) * LANES].astype(F32)
            even = jnp.where(lo_half, slab, 0.0).T
            odd = jnp.where(lo_half, 0.0, slab).T
            dst[s] = jnp.concatenate([even, odd], axis=1).astype(BF16)

    kpos_iota = lax.broadcasted_iota(I32, (CH, TQ), 0)
    qpos = qpos0 + lax.broadcasted_iota(I32, (CH, TQ), 1)

    def score_chunk(c, carry):
        off = pl.multiple_of(c * CH, CH)
        kic = ki_ref[0, pl.ds(off, CH), :]
        acc = jnp.zeros((CH, TQ), F32)
        for s in range(IDX_HEADS // 2):
            d = jnp.dot(kic, qit_ref[s], preferred_element_type=F32)
            for e in range(2):
                w = wit_ref[0, 2 * s + e:2 * s + e + 1, :]
                acc = acc + jnp.maximum(d[:, e * TQ:(e + 1) * TQ], 0.0) * w
        key = jnp.where(off + kpos_iota <= qpos, _val_to_key(acc), INT_MIN)
        keys_ref[pl.ds(off, CH), :] = key
        kmax_ref[...] = jnp.maximum(kmax_ref[...], key)
        return carry

    n_span = (nch + UNROLL - 1) // UNROLL

    def score_span(i, carry):
        for u in range(UNROLL):
            carry = score_chunk(i * UNROLL + u, carry)
        return carry

    kmax_ref[...] = jnp.full(kmax_ref.shape, INT_MIN, I32)
    lax.fori_loop(0, n_span, score_span, 0)

    def count(pred):
        def body(c, acc):
            off = pl.multiple_of(c * CH, CH)
            hit = jnp.where(pred(keys_ref[pl.ds(off, CH), :], off + kpos_iota), 1.0, 0.0)
            return acc + _fold_rows(hit, jnp.add)
        acc = _chunk_loop(nch, body, jnp.zeros((SUBLANES, TQ), F32))
        return jnp.sum(acc, axis=0, keepdims=True)

    kmax = kmax_ref[...]
    lo0 = jnp.min(_fold_rows(kmax, jnp.minimum), axis=0, keepdims=True)
    hi0 = jnp.minimum(jnp.max(_fold_rows(kmax, jnp.maximum), axis=0, keepdims=True), INT_MAX - 1) + 1

    def live(lo, hi, c_lo):
        return searchable & (c_lo != kf) & (hi - lo != 1)

    def n_live(lo, hi, c_lo):
        return jnp.max(jnp.where(live(lo, hi, c_lo), 1.0, 0.0))

    def pick(it, lo, hi, c_lo, c_hi):
        top = hi - 1
        v_lo, v_top = _key_to_val(lo), _key_to_val(top)
        log_lo = jnp.log(c_lo)
        frac = (log_lo - math.log(kf)) / (log_lo - jnp.log(jnp.maximum(c_hi, 0.5)))
        interpolate = (it >= 1) & (it <= INTERP_STEPS) & (c_lo < 1e8)
        frac = jnp.where(interpolate, jnp.minimum(jnp.maximum(frac, 0.1), 0.9), 0.5)
        mid_val = _val_to_key(v_lo + (v_top - v_lo) * frac)
        mid_key = lo + lax.shift_right_logical(hi - lo, 1)
        key_step = (it > INTERP_STEPS) & (it % KEY_STEP_PERIOD == KEY_STEP_PERIOD - 1)
        cand = jnp.where(key_step, mid_key, mid_val)
        cand = jnp.where((lo < 0) & (top >= 0), 0, cand)
        cand = jnp.where((lo == 0) & (top >= 1), 1, cand)
        cand = jnp.minimum(jnp.maximum(cand, lo + 1), top)
        return jnp.where(live(lo, hi, c_lo), cand, lo)

    def search_steps(st, n_steps):
        it, lo, hi, c_lo, c_hi = st
        for _ in range(n_steps):
            cand = pick(it, lo, hi, c_lo, c_hi)
            cnt = count(lambda k, pos: k >= cand)
            ok = cnt >= kf
            lo = jnp.where(ok, cand, lo)
            c_lo = jnp.where(ok, cnt, c_lo)
            hi = jnp.where(ok, hi, cand)
            c_hi = jnp.where(ok, c_hi, cnt)
            it = it + 1
        return it, lo, hi, c_lo, c_hi

    def bis_cond(st):
        return (st[0] < MAX_SEARCH_STEPS) & (st[5] > 0.0)

    def bis_body(st):
        st = search_steps(st[:5], STEPS_PER_EXIT_TEST)
        return st + (n_live(st[1], st[2], st[3]),)

    c_unknown = jnp.full((1, TQ), 1e9, F32)
    st = (jnp.int32(0), lo0, hi0, c_unknown, jnp.zeros((1, TQ), F32))
    st = lax.fori_loop(0, UNTESTED_STEPS // STEPS_PER_EXIT_TEST,
                       lambda _, st: search_steps(st, STEPS_PER_EXIT_TEST), st)
    _, t_fin, _, ct_fin, c_above, _ = lax.while_loop(bis_cond, bis_body, st + (n_live(st[1], st[2], st[3]),))
    t_fin = jnp.where(searchable, t_fin, INT_MIN)
    j_ref[...] = jnp.broadcast_to(jnp.where(searchable, 2 ** 30, -1).astype(I32), j_ref.shape)
    open_fin = jnp.max(jnp.where(searchable & (ct_fin != kf), 1.0, 0.0))

    @pl.when(open_fin > 0.0)
    def _():
        tie_q = searchable & (ct_fin != kf)
        need = kf - c_above
        tri = jnp.where(lax.broadcasted_iota(I32, (CH, CH), 0) >= lax.broadcasted_iota(I32, (CH, CH), 1),
                        1.0, 0.0).astype(BF16)

        def body(c, carry):
            seen, jmax = carry
            off = pl.multiple_of(c * CH, CH)
            is_tie = keys_ref[pl.ds(off, CH), :] == t_fin
            rank = jnp.dot(tri, jnp.where(is_tie, 1.0, 0.0).astype(BF16), preferred_element_type=F32) + seen
            kept_pos = jnp.where(is_tie & (rank <= need), off + kpos_iota, -1)
            return rank[CH - 1:CH, :], jnp.maximum(jmax, _fold_rows(kept_pos, jnp.maximum))

        _, jmax = _chunk_loop(nch, body, (jnp.zeros((1, TQ), F32), jnp.full((SUBLANES, TQ), -1, I32)))
        j_tie = jnp.max(jmax, axis=0, keepdims=True)
        j_ref[...] = jnp.where(tie_q, jnp.broadcast_to(j_tie, j_ref.shape), j_ref[...])

    j_fin = j_ref[0:1, :]

    def logits(c, s):
        g = (2 * s) // (ATT_HEADS // ATT_KV_HEADS)
        kg = kd_ref[0, pl.ds(pl.multiple_of(c * CH, CH), CH), g * LANES:(g + 1) * LANES]
        return jnp.dot(kg, qt_ref[s], preferred_element_type=F32)

    def span_step(c_new, c_old, n_sub, m_old, new, old):
        m_loc = [jnp.full((SUBLANES, TQ), -jnp.inf, F32) for _ in range(ATT_HEADS)]
        for u in range(n_sub):
            rows = slice(u * CH, (u + 1) * CH)
            if new:
                off_new = pl.multiple_of((c_new + u) * CH, CH)
                k = keys_ref[pl.ds(off_new, CH), :]
                sel = (k > t_fin) | ((k == t_fin) & (off_new + kpos_iota <= j_fin))
                bias = jnp.where(sel, 0.0, NEG_BIG)
            for s in range(n_pairs):
                if old:
                    ps = [jnp.exp2(stage_ref[s, rows, e * TQ:(e + 1) * TQ] - m_old[2 * s + e]).astype(BF16)
                          for e in range(2)]
                    p_ref[s, rows, :] = jnp.concatenate(ps, axis=1)
                if new:
                    lt2 = logits(c_new + u, s)
                    for e in range(2):
                        h = 2 * s + e
                        lt = lt2[:, e * TQ:(e + 1) * TQ] + bias
                        stage_ref[s, rows, e * TQ:(e + 1) * TQ] = lt
                        m_loc[h] = jnp.maximum(m_loc[h], _fold_rows(lt, jnp.maximum))
        return m_loc

    def advance_max(m_run, m_loc):
        return tuple(jnp.maximum(m_run[h], jnp.max(m_loc[h], axis=0, keepdims=True)) for h in range(ATT_HEADS))

    def pv_span(c0, m_before, m_shift):
        def body(_, carry):
            off = pl.multiple_of(c0 * CH, UNROLL * CH)
            for s in range(n_pairs):
                g = (2 * s) // (ATT_HEADS // ATT_KV_HEADS)
                vgt = vt_ref[0, g * VT_ROWS:(g + 1) * VT_ROWS, pl.ds(off, UNROLL * CH)]
                pv = jnp.dot(vgt, p_ref[s], preferred_element_type=F32)
                for e in range(2):
                    h = 2 * s + e
                    cols = slice(e * TQ, (e + 1) * TQ)
                    acc_ref[s, :, cols] = jnp.exp2(m_before[h] - m_shift[h]) * acc_ref[s, :, cols] + pv[:, cols]
            return carry
        lax.fori_loop(0, jnp.minimum(n_span, 1), body, 0)

    def middle_span(i, carry):
        m_before, m_run = carry
        m_loc = span_step(i * UNROLL, (i - 1) * UNROLL, UNROLL, m_run, new=True, old=True)
        pv_span((i - 1) * UNROLL, m_before, m_run)
        return m_run, advance_max(m_run, m_loc)

    acc_ref[...] = jnp.zeros(acc_ref.shape, F32)
    m_none = tuple(jnp.full((1, TQ), -jnp.inf, F32) for _ in range(ATT_HEADS))
    m_first = advance_max(m_none, span_step(0, 0, UNROLL, None, new=True, old=False))
    m_before, m_fin = lax.fori_loop(1, n_span, middle_span, (m_none, m_first))
    span_step(0, (n_span - 1) * UNROLL, UNROLL, m_fin, new=False, old=True)
    pv_span((n_span - 1) * UNROLL, m_before, m_fin)

    for s in range(n_pairs):
        o_t = jnp.concatenate(
            [acc_ref[s, :HEAD_DIM, e * TQ:(e + 1) * TQ] / acc_ref[s, HEAD_DIM:HEAD_DIM + 1, e * TQ:(e + 1) * TQ]
             for e in range(2)], axis=0)
        o_ref[0, :, s * LANES:(s + 1) * LANES] = o_t.T.astype(o_ref.dtype)


def _dsa_attention(qi, wit, q, ki2, kd, vt, topk):
    b, s, _ = q.shape
    lk = kd.shape[1]
    assert s % TQ == 0 and lk % (UNROLL * CH) == 0 and lk >= N_META + s and topk <= CH
    qblk = lambda bi, j: (bi, j, 0)
    kblk = lambda bi, j: (bi, 0, 0)
    return pl.pallas_call(
        functools.partial(_attn_kernel, topk),
        grid=(b, s // TQ),
        in_specs=[
            pl.BlockSpec((1, TQ, IDX_HEADS * IDX_DIM), qblk),
            pl.BlockSpec((1, IDX_HEADS, TQ), lambda bi, j: (bi, 0, j)),
            pl.BlockSpec((1, TQ, ATT_W), qblk),
            pl.BlockSpec((1, lk, LANES), kblk),
            pl.BlockSpec((1, lk, 2 * KV_W), kblk),
            pl.BlockSpec((1, ATT_KV_HEADS * VT_ROWS, lk), kblk),
        ],
        out_specs=pl.BlockSpec((1, TQ, ATT_W), qblk),
        out_shape=jax.ShapeDtypeStruct((b, s, ATT_W), BF16),
        scratch_shapes=[
            pltpu.VMEM((lk, TQ), I32),
            pltpu.VMEM((CH, TQ), I32),
            pltpu.VMEM((ATT_HEADS // 2, UNROLL * CH, 2 * TQ), F32),
            pltpu.VMEM((ATT_HEADS // 2, UNROLL * CH, 2 * TQ), BF16),
            pltpu.VMEM((IDX_HEADS // 2, LANES, 2 * TQ), BF16),
            pltpu.VMEM((ATT_HEADS // 2, LANES, 2 * TQ), BF16),
            pltpu.VMEM((SUBLANES, TQ), I32),
            pltpu.VMEM((ATT_HEADS // 2, VT_ROWS, 2 * TQ), F32),
        ],
        compiler_params=pltpu.CompilerParams(
            dimension_semantics=("arbitrary", "arbitrary"), vmem_limit_bytes=VMEM_LIMIT),
        name="dsa_attn",
    )(qi, wit, q, ki2, kd, vt)


def _merge_ln_kernel(att_ref, cu_ref, halo_ref, gb_ref, ga_ref, gc_ref, h_ref, cw_ref,
                     wa_ref, wc_ref, wo_ref, g_ref, b_ref, o_ref):
    tm = cu_ref.shape[1]
    cu = cu_ref[0]
    halo = halo_ref[0, 0]
    row = lax.broadcasted_iota(I32, (tm, CONV_W), 0)
    prev1 = jnp.where(row == 0, halo[7:8], pltpu.roll(cu, 1, 0))
    prev2 = jnp.where(row == 0, halo[6:7], jnp.where(row == 1, halo[7:8], pltpu.roll(cu, 2, 0)))
    cw = cw_ref[...]
    conv = cw[0:1] * prev2 + cw[1:2] * prev1 + cw[2:3] * cu
    y_conv = jnp.dot((gb_ref[0] * conv).astype(BF16), wc_ref[...], preferred_element_type=F32)
    y_att = jnp.dot(att_ref[0], wa_ref[...], preferred_element_type=F32)
    merged = jax.nn.sigmoid(ga_ref[0]) * y_att + jax.nn.sigmoid(gc_ref[0]) * y_conv
    mix = jnp.dot(merged.astype(BF16), wo_ref[...], preferred_element_type=F32)
    o_ref[0] = _layer_norm(DEEPNORM_ALPHA * h_ref[0] + mix, g_ref[...], b_ref[...])


def _merge_ln(att, cu, halo, gb, ga, gc, h1, conv_w8, wa, wc, wo, g, b):
    bsz, s, _ = cu.shape
    tm = s // halo.shape[1]
    blk = lambda bi, i: (bi, i, 0)
    const = lambda bi, i: (0, 0)
    return pl.pallas_call(
        _merge_ln_kernel,
        grid=(bsz, s // tm),
        in_specs=[
            pl.BlockSpec((1, tm, ATT_W), blk),
            pl.BlockSpec((1, tm, CONV_W), blk),
            pl.BlockSpec((1, 1, 8, CONV_W), lambda bi, i: (bi, i, 0, 0)),
            pl.BlockSpec((1, tm, CONV_W), blk),
            pl.BlockSpec((1, tm, D_MODEL), blk),
            pl.BlockSpec((1, tm, D_MODEL), blk),
            pl.BlockSpec((1, tm, D_MODEL), blk),
            pl.BlockSpec((8, CONV_W), const),
            pl.BlockSpec((ATT_W, D_MODEL), const),
            pl.BlockSpec((CONV_W, D_MODEL), const),
            pl.BlockSpec((D_MODEL, D_MODEL), const),
            pl.BlockSpec((1, D_MODEL), const),
            pl.BlockSpec((1, D_MODEL), const),
        ],
        out_specs=pl.BlockSpec((1, tm, D_MODEL), blk),
        out_shape=jax.ShapeDtypeStruct((bsz, s, D_MODEL), F32),
        compiler_params=pltpu.CompilerParams(
            dimension_semantics=("arbitrary", "arbitrary"), vmem_limit_bytes=VMEM_LIMIT),
        name="merge_ln",
    )(att, cu, halo, gb, ga, gc, h1, conv_w8, wa, wc, wo, g, b)


def _rope_tables(pos):
    half = HEAD_DIM // 2
    inv_freq = ROPE_THETA ** (-jnp.arange(half, dtype=F32) / half)
    ang = pos.astype(F32)[:, None] * inv_freq[None, :]
    cos, sin = jnp.cos(ang), jnp.sin(ang)
    return jnp.tile(cos, (1, LANES // half)), jnp.tile(jnp.concatenate([-sin, sin], axis=1), (1, LANES // HEAD_DIM))


def _split_w_in(w_in):
    widths = (ATT_W, KV_W, KV_W, IDX_HEADS * IDX_DIM, IDX_DIM, IDX_HEADS,
              CONV_W, CONV_W, CONV_W, D_MODEL, D_MODEL)
    offs = np.cumsum(widths)[:-1].tolist()
    wq, wk, wv, wqi, wki, wwi, wu, wgb, wgc, wga, wgv = jnp.split(w_in, offs, axis=1)
    dup = lambda w: jnp.concatenate([w[:, :HEAD_DIM], w[:, :HEAD_DIM], w[:, HEAD_DIM:], w[:, HEAD_DIM:]], axis=1)
    w_a = jnp.concatenate(
        [wq, wqi, dup(wk), wki, wki, dup(wv), wwi, jnp.zeros((D_MODEL, LANES - IDX_HEADS), w_in.dtype)], axis=1)
    w_b = jnp.concatenate([wu, wgc, wgb, wga, wgv], axis=1)
    return w_a.astype(BF16), w_b.astype(BF16)


def kernel(x, meta_tokens, ffn1_w_gate, ffn1_w_up, ffn1_w_down, ln1_g, ln1_b, w_in, conv_w, w_att_out, w_conv_out, w_o, ln2_g, ln2_b, ffn2_w_gate, ffn2_w_up, ffn2_w_down, ln3_g, ln3_b):
    bsz, seq, _ = x.shape
    l_tot = N_META + seq
    topk = min(TOPK_MAX, l_tot // 4)
    lk = -(-l_tot // (UNROLL * CH)) * (UNROLL * CH)
    lyr = 0
    bf = lambda w: w.astype(BF16)
    vec = lambda v: v[lyr][None, :]

    w1 = (bf(ffn1_w_gate[lyr]), bf(ffn1_w_up[lyr]), bf(ffn1_w_down[lyr]), vec(ln1_g), vec(ln1_b))
    w_a, w_b = _split_w_in(w_in[lyr])
    pos = jnp.arange(l_tot, dtype=I32)
    cos, sin = _rope_tables(pos)

    hm = _ffn_ln(meta_tokens.astype(F32), *w1)
    _, _, kd_m, ki_m, vd_m, _ = _proj_rope(hm, w_a, cos[:N_META], sin[:N_META], N_META)
    cu_m = _proj_gate(hm, w_b)[0]

    h1 = _ffn_ln(x.reshape(bsz * seq, D_MODEL), *w1)
    q, qi, kd, ki2, vd, wi = _proj_rope(h1, w_a, cos[N_META:], sin[N_META:], seq)
    cu, gb, ga, gc = _proj_gate(h1, w_b)

    def with_meta(meta_rows, real):
        wdt = real.shape[-1]
        return jnp.concatenate(
            [jnp.broadcast_to(meta_rows[None], (bsz, N_META, wdt)), real.reshape(bsz, seq, wdt),
             jnp.zeros((bsz, lk - l_tot, wdt), real.dtype)], axis=1)

    per_seq = lambda a: a.reshape(bsz, seq, a.shape[-1])
    v_keys = with_meta(vd_m, vd)
    ones_pad = jnp.zeros((bsz, lk, VT_ROWS - HEAD_DIM), BF16).at[:, :, 0].set(1.0)
    vt = jnp.swapaxes(jnp.concatenate(
        [v_keys[:, :, :HEAD_DIM], ones_pad, v_keys[:, :, 2 * HEAD_DIM:3 * HEAD_DIM], ones_pad], axis=2), 1, 2)
    wit = jnp.swapaxes(per_seq(wi)[:, :, :IDX_HEADS], 1, 2)
    att = _dsa_attention(per_seq(qi), wit, per_seq(q), with_meta(ki_m, ki2), with_meta(kd_m, kd), vt, topk)

    tm = min(seq, 512)
    cu3 = per_seq(cu)
    tails = cu3.reshape(bsz, seq // tm, tm, CONV_W)[:, :-1, tm - 8:, :]
    halo = jnp.concatenate([jnp.broadcast_to(cu_m[None, None, N_META - 8:], (bsz, 1, 8, CONV_W)), tails], axis=1)
    conv_w8 = jnp.concatenate([conv_w[lyr].astype(F32), jnp.zeros((8 - CONV_K, CONV_W), F32)], axis=0)
    h2 = _merge_ln(att, cu3, halo, per_seq(gb), per_seq(ga), per_seq(gc), per_seq(h1), conv_w8,
                   bf(w_att_out[lyr]), bf(w_conv_out[lyr]), bf(w_o[lyr]), vec(ln2_g), vec(ln2_b))

    h3 = _ffn_ln(h2.reshape(bsz * seq, D_MODEL), bf(ffn2_w_gate[lyr]), bf(ffn2_w_up[lyr]), bf(ffn2_w_down[lyr]),
                 vec(ln3_g), vec(ln3_b))
    return h3.reshape(bsz, seq, D_MODEL)
```

```python
import functools
import math

import numpy as np
import jax
import jax.numpy as jnp
from jax import lax
from jax.experimental import pallas as pl
from jax.experimental.pallas import tpu as pltpu

F32 = jnp.float32
BF16 = jnp.bfloat16
I32 = jnp.int32

D_MODEL = 1024
N_META = 16
ATT_HEADS = 8
ATT_KV_HEADS = 2
HEAD_DIM = 64
ATT_W = ATT_HEADS * HEAD_DIM
KV_W = ATT_KV_HEADS * HEAD_DIM
IDX_HEADS = 8
IDX_DIM = 64
TOPK_MAX = 256
ROPE_THETA = 10000.0
CONV_W = D_MODEL // 2
CONV_K = 3
D_FF = 2816
LN_EPS = 1e-5
DEPTH = 1
DEEPNORM_ALPHA = (2.0 * DEPTH) ** 0.25

LANES = 128
SUBLANES = 8
TQ = 256
CH = 256
VT_ROWS = 80
UNROLL = 4
STEPS_PER_EXIT_TEST = 2
UNTESTED_STEPS = 12
INTERP_STEPS = 10
KEY_STEP_PERIOD = 4
MAX_SEARCH_STEPS = 160
INT_MAX = 2 ** 31 - 1
INT_MIN = -(2 ** 31)
LOG2E = 1.4426950408889634
NEG_BIG = -1e30
VMEM_LIMIT = 52 * 1024 * 1024


def _layer_norm(y, g, b):
    mu = jnp.mean(y, axis=-1, keepdims=True)
    yc = y - mu
    var = jnp.mean(yc * yc, axis=-1, keepdims=True)
    return yc * lax.rsqrt(var + LN_EPS) * g + b


def _ffn_ln_kernel(x_ref, wg_ref, wu_ref, wd_ref, g_ref, b_ref, o_ref, xb_ref):
    f = pl.program_id(1)

    @pl.when(f == 0)
    def _():
        o_ref[...] = jnp.zeros_like(o_ref)
        xb_ref[...] = x_ref[...].astype(BF16)

    xb = xb_ref[...]
    gt = jnp.dot(xb, wg_ref[...], preferred_element_type=F32)
    up = jnp.dot(xb, wu_ref[...], preferred_element_type=F32)
    a = (gt * jax.nn.sigmoid(gt)) * up
    o_ref[...] += jnp.dot(a.astype(BF16), wd_ref[...], preferred_element_type=F32)

    @pl.when(f == pl.num_programs(1) - 1)
    def _():
        tr = min(x_ref.shape[0], 256)

        def norm_rows(r, carry):
            rows = pl.ds(pl.multiple_of(r * tr, tr), tr)
            y = DEEPNORM_ALPHA * x_ref[rows, :] + 0.5 * o_ref[rows, :]
            o_ref[rows, :] = _layer_norm(y, g_ref[...], b_ref[...])
            return carry

        lax.fori_loop(0, x_ref.shape[0] // tr, norm_rows, 0)


def _ffn_ln(x2d, wg, wu, wd, g, b):
    n = x2d.shape[0]
    tm = min(n, 2048)
    tf = 256
    assert n % tm == 0 and D_FF % tf == 0
    return pl.pallas_call(
        _ffn_ln_kernel,
        grid=(n // tm, D_FF // tf),
        in_specs=[
            pl.BlockSpec((tm, D_MODEL), lambda i, f: (i, 0)),
            pl.BlockSpec((D_MODEL, tf), lambda i, f: (0, f)),
            pl.BlockSpec((D_MODEL, tf), lambda i, f: (0, f)),
            pl.BlockSpec((tf, D_MODEL), lambda i, f: (f, 0)),
            pl.BlockSpec((1, D_MODEL), lambda i, f: (0, 0)),
            pl.BlockSpec((1, D_MODEL), lambda i, f: (0, 0)),
        ],
        out_specs=pl.BlockSpec((tm, D_MODEL), lambda i, f: (i, 0)),
        out_shape=jax.ShapeDtypeStruct((n, D_MODEL), F32),
        scratch_shapes=[pltpu.VMEM((tm, D_MODEL), BF16)],
        compiler_params=pltpu.CompilerParams(
            dimension_semantics=("arbitrary", "arbitrary"), vmem_limit_bytes=VMEM_LIMIT),
        name="ffn_ln",
    )(x2d, wg, wu, wd, g, b)


_PROJ_A_W = ATT_W + IDX_HEADS * IDX_DIM + 2 * KV_W + LANES + 2 * KV_W + LANES


def _proj_rope_kernel(x_ref, w_ref, cos_ref, sin_ref, q_ref, qi_ref, kd_ref, ki_ref, vd_ref, wi_ref):
    tm = x_ref.shape[0]
    xb = x_ref[...].astype(BF16)
    cos = cos_ref[...]
    sin = sin_ref[...]
    lane = lax.broadcasted_iota(I32, (tm, LANES), 1)
    first_half = (lane & (HEAD_DIM // 2)) == 0

    def rope(s):
        partner = jnp.where(first_half, pltpu.roll(s, LANES - HEAD_DIM // 2, 1),
                            pltpu.roll(s, HEAD_DIM // 2, 1))
        return s * cos + partner * sin

    def dot_cols(c0, width):
        return jnp.dot(xb, w_ref[:, c0:c0 + width], preferred_element_type=F32)

    outs = ((q_ref, ATT_W, HEAD_DIM ** -0.5 * LOG2E), (qi_ref, IDX_HEADS * IDX_DIM, 1.0), (kd_ref, 2 * KV_W, 1.0))
    c0 = 0
    for ref, width, scale in outs:
        for j in range(0, width, 2 * LANES):
            p = dot_cols(c0 + j, 2 * LANES)
            for s in range(2):
                r = rope(p[:, s * LANES:(s + 1) * LANES])
                if scale != 1.0:
                    r = r * scale
                ref[:, j + s * LANES:j + (s + 1) * LANES] = r.astype(ref.dtype)
        c0 += width
    p = dot_cols(c0, LANES)
    ki_ref[...] = rope(p).astype(ki_ref.dtype)
    c0 += LANES
    vd_ref[...] = dot_cols(c0, 2 * KV_W).astype(vd_ref.dtype)
    c0 += 2 * KV_W
    wi_ref[...] = dot_cols(c0, LANES) * ((IDX_HEADS * IDX_DIM) ** -0.5)


def _proj_rope(h2d, w_a, cos, sin, rows_per_seq):
    n = h2d.shape[0]
    tm = min(rows_per_seq, 512)
    tiles_per_seq = rows_per_seq // tm
    assert n % tm == 0 and rows_per_seq % tm == 0
    row = lambda i: (i, 0)
    tab = lambda i: (i % tiles_per_seq, 0)
    widths = (ATT_W, IDX_HEADS * IDX_DIM, 2 * KV_W, LANES, 2 * KV_W, LANES)
    dtypes = (BF16, BF16, BF16, BF16, BF16, F32)
    return pl.pallas_call(
        _proj_rope_kernel,
        grid=(n // tm,),
        in_specs=[
            pl.BlockSpec((tm, D_MODEL), row),
            pl.BlockSpec((D_MODEL, _PROJ_A_W), lambda i: (0, 0)),
            pl.BlockSpec((tm, LANES), tab),
            pl.BlockSpec((tm, LANES), tab),
        ],
        out_specs=[pl.BlockSpec((tm, w), row) for w in widths],
        out_shape=[jax.ShapeDtypeStruct((n, w), dt) for w, dt in zip(widths, dtypes)],
        compiler_params=pltpu.CompilerParams(
            dimension_semantics=("arbitrary",), vmem_limit_bytes=VMEM_LIMIT),
        name="proj_rope",
    )(h2d, w_a, cos, sin)


_PROJ_B_W = 3 * CONV_W


def _proj_gate_kernel(x_ref, w_ref, cu_ref, gb_ref):
    xb = x_ref[...].astype(BF16)

    def dot_cols(c0, width):
        return jnp.dot(xb, w_ref[:, c0:c0 + width], preferred_element_type=F32)

    cu_ref[...] = dot_cols(0, CONV_W) * dot_cols(CONV_W, CONV_W)
    gb_ref[...] = dot_cols(2 * CONV_W, CONV_W)


def _proj_gate(h2d, w_b):
    n = h2d.shape[0]
    tm = min(n, 512)
    assert n % tm == 0
    row = lambda i: (i, 0)
    widths = (CONV_W, CONV_W)
    return pl.pallas_call(
        _proj_gate_kernel,
        grid=(n // tm,),
        in_specs=[pl.BlockSpec((tm, D_MODEL), row), pl.BlockSpec((D_MODEL, _PROJ_B_W), lambda i: (0, 0))],
        out_specs=[pl.BlockSpec((tm, w), row) for w in widths],
        out_shape=[jax.ShapeDtypeStruct((n, w), F32) for w in widths],
        compiler_params=pltpu.CompilerParams(
            dimension_semantics=("arbitrary",), vmem_limit_bytes=VMEM_LIMIT),
        name="proj_gate",
    )(h2d, w_b)


def _fold_rows(x, op):
    parts = [x[i:i + SUBLANES] for i in range(0, x.shape[0], SUBLANES)]
    while len(parts) > 1:
        parts = [op(parts[i], parts[i + 1]) for i in range(0, len(parts), 2)]
    return parts[0]


def _key_to_val(key):
    return pltpu.bitcast(jnp.where(key < 0, key ^ 0x7FFFFFFF, key), F32)


def _val_to_key(val):
    bits = pltpu.bitcast(jnp.where(val == 0.0, 0.0, val), I32)
    return jnp.where(bits < 0, bits ^ 0x7FFFFFFF, bits)


def _chunk_loop(nch, body, init):
    n_main = nch // UNROLL

    def main(i, carry):
        for u in range(UNROLL):
            carry = body(i * UNROLL + u, carry)
        return carry

    carry = lax.fori_loop(0, n_main, main, init)
    return lax.fori_loop(n_main * UNROLL, nch, body, carry)


def _attn_kernel(topk, qi_ref, wit_ref, q_ref, ki_ref, kd_ref, vt_ref, o_ref,
                 keys_ref, kmax_ref, stage_ref, p_ref, qit_ref, qt_ref, j_ref, acc_ref):
    jblk = pl.program_id(1)
    qpos0 = N_META + jblk * TQ
    nch = (qpos0 + TQ - 1) // CH + 1
    kf = float(topk)
    n_pairs = ATT_HEADS // 2

    lane = lax.broadcasted_iota(I32, (TQ, LANES), 1)
    lo_half = lane < HEAD_DIM
    qpos_row = qpos0 + lax.broadcasted_iota(I32, (1, TQ), 1)
    searchable = (qpos_row + 1).astype(F32) > kf

    for s in range(n_pairs):
        for src, dst in ((qi_ref, qit_ref), (q_ref, qt_ref)):
            slab = src[0, :, s * LANES:(s + 1) * LANES].astype(F32)
            even = jnp.where(lo_half, slab, 0.0).T
            odd = jnp.where(lo_half, 0.0, slab).T
            dst[s] = jnp.concatenate([even, odd], axis=1).astype(BF16)

    kpos_iota = lax.broadcasted_iota(I32, (CH, TQ), 0)
    qpos = qpos0 + lax.broadcasted_iota(I32, (CH, TQ), 1)

    def score_chunk(c, carry):
        off = pl.multiple_of(c * CH, CH)
        kic = ki_ref[0, pl.ds(off, CH), :]
        acc = jnp.zeros((CH, TQ), F32)
        for s in range(IDX_HEADS // 2):
            d = jnp.dot(kic, qit_ref[s], preferred_element_type=F32)
            for e in range(2):
                w = wit_ref[0, 2 * s + e:2 * s + e + 1, :]
                acc = acc + jnp.maximum(d[:, e * TQ:(e + 1) * TQ], 0.0) * w
        key = jnp.where(off + kpos_iota <= qpos, _val_to_key(acc), INT_MIN)
        keys_ref[pl.ds(off, CH), :] = key
        kmax_ref[...] = jnp.maximum(kmax_ref[...], key)
        return carry

    n_span = (nch + UNROLL - 1) // UNROLL

    def score_span(i, carry):
        for u in range(UNROLL):
            carry = score_chunk(i * UNROLL + u, carry)
        return carry

    kmax_ref[...] = jnp.full(kmax_ref.shape, INT_MIN, I32)
    lax.fori_loop(0, n_span, score_span, 0)

    def count(pred):
        def body(c, acc):
            off = pl.multiple_of(c * CH, CH)
            hit = jnp.where(pred(keys_ref[pl.ds(off, CH), :], off + kpos_iota), 1.0, 0.0)
            return acc + _fold_rows(hit, jnp.add)
        acc = _chunk_loop(nch, body, jnp.zeros((SUBLANES, TQ), F32))
        return jnp.sum(acc, axis=0, keepdims=True)

    kmax = kmax_ref[...]
    lo0 = jnp.min(_fold_rows(kmax, jnp.minimum), axis=0, keepdims=True)
    hi0 = jnp.minimum(jnp.max(_fold_rows(kmax, jnp.maximum), axis=0, keepdims=True), INT_MAX - 1) + 1

    def live(lo, hi, c_lo):
        return searchable & (c_lo != kf) & (hi - lo != 1)

    def n_live(lo, hi, c_lo):
        return jnp.max(jnp.where(live(lo, hi, c_lo), 1.0, 0.0))

    def pick(it, lo, hi, c_lo, c_hi):
        top = hi - 1
        v_lo, v_top = _key_to_val(lo), _key_to_val(top)
        log_lo = jnp.log(c_lo)
        frac = (log_lo - math.log(kf)) / (log_lo - jnp.log(jnp.maximum(c_hi, 0.5)))
        interpolate = (it >= 1) & (it <= INTERP_STEPS) & (c_lo < 1e8)
        frac = jnp.where(interpolate, jnp.minimum(jnp.maximum(frac, 0.1), 0.9), 0.5)
        mid_val = _val_to_key(v_lo + (v_top - v_lo) * frac)
        mid_key = lo + lax.shift_right_logical(hi - lo, 1)
        key_step = (it > INTERP_STEPS) & (it % KEY_STEP_PERIOD == KEY_STEP_PERIOD - 1)
        cand = jnp.where(key_step, mid_key, mid_val)
        cand = jnp.where((lo < 0) & (top >= 0), 0, cand)
        cand = jnp.where((lo == 0) & (top >= 1), 1, cand)
        cand = jnp.minimum(jnp.maximum(cand, lo + 1), top)
        return jnp.where(live(lo, hi, c_lo), cand, lo)

    def search_steps(st, n_steps):
        it, lo, hi, c_lo, c_hi = st
        for _ in range(n_steps):
            cand = pick(it, lo, hi, c_lo, c_hi)
            cnt = count(lambda k, pos: k >= cand)
            ok = cnt >= kf
            lo = jnp.where(ok, cand, lo)
            c_lo = jnp.where(ok, cnt, c_lo)
            hi = jnp.where(ok, hi, cand)
            c_hi = jnp.where(ok, c_hi, cnt)
            it = it + 1
        return it, lo, hi, c_lo, c_hi

    def bis_cond(st):
        return (st[0] < MAX_SEARCH_STEPS) & (st[5] > 0.0)

    def bis_body(st):
        st = search_steps(st[:5], STEPS_PER_EXIT_TEST)
        return st + (n_live(st[1], st[2], st[3]),)

    c_unknown = jnp.full((1, TQ), 1e9, F32)
    st = (jnp.int32(0), lo0, hi0, c_unknown, jnp.zeros((1, TQ), F32))
    st = lax.fori_loop(0, UNTESTED_STEPS // STEPS_PER_EXIT_TEST,
                       lambda _, st: search_steps(st, STEPS_PER_EXIT_TEST), st)
    _, t_fin, _, ct_fin, c_above, _ = lax.while_loop(bis_cond, bis_body, st + (n_live(st[1], st[2], st[3]),))
    t_fin = jnp.where(searchable, t_fin, INT_MIN)
    j_ref[...] = jnp.broadcast_to(jnp.where(searchable, 2 ** 30, -1).astype(I32), j_ref.shape)
    open_fin = jnp.max(jnp.where(searchable & (ct_fin != kf), 1.0, 0.0))

    @pl.when(open_fin > 0.0)
    def _():
        tie_q = searchable & (ct_fin != kf)
        need = kf - c_above
        tri = jnp.where(lax.broadcasted_iota(I32, (CH, CH), 0) >= lax.broadcasted_iota(I32, (CH, CH), 1),
                        1.0, 0.0).astype(BF16)

        def body(c, carry):
            seen, jmax = carry
            off = pl.multiple_of(c * CH, CH)
            is_tie = keys_ref[pl.ds(off, CH), :] == t_fin
            rank = jnp.dot(tri, jnp.where(is_tie, 1.0, 0.0).astype(BF16), preferred_element_type=F32) + seen
            kept_pos = jnp.where(is_tie & (rank <= need), off + kpos_iota, -1)
            return rank[CH - 1:CH, :], jnp.maximum(jmax, _fold_rows(kept_pos, jnp.maximum))

        _, jmax = _chunk_loop(nch, body, (jnp.zeros((1, TQ), F32), jnp.full((SUBLANES, TQ), -1, I32)))
        j_tie = jnp.max(jmax, axis=0, keepdims=True)
        j_ref[...] = jnp.where(tie_q, jnp.broadcast_to(j_tie, j_ref.shape), j_ref[...])

    j_fin = j_ref[0:1, :]

    def logits(c, s):
        g = (2 * s) // (ATT_HEADS // ATT_KV_HEADS)
        kg = kd_ref[0, pl.ds(pl.multiple_of(c * CH, CH), CH), g * LANES:(g + 1) * LANES]
        return jnp.dot(kg, qt_ref[s], preferred_element_type=F32)

    def span_step(c_new, c_old, n_sub, m_old, new, old):
        m_loc = [jnp.full((SUBLANES, TQ), -jnp.inf, F32) for _ in range(ATT_HEADS)]
        for u in range(n_sub):
            rows = slice(u * CH, (u + 1) * CH)
            if new:
                off_new = pl.multiple_of((c_new + u) * CH, CH)
                k = keys_ref[pl.ds(off_new, CH), :]
                sel = (k > t_fin) | ((k == t_fin) & (off_new + kpos_iota <= j_fin))
                bias = jnp.where(sel, 0.0, NEG_BIG)
            for s in range(n_pairs):
                if old:
                    ps = [jnp.exp2(stage_ref[s, rows, e * TQ:(e + 1) * TQ] - m_old[2 * s + e]).astype(BF16)
                          for e in range(2)]
                    p_ref[s, rows, :] = jnp.concatenate(ps, axis=1)
                if new:
                    lt2 = logits(c_new + u, s)
                    for e in range(2):
                        h = 2 * s + e
                        lt = lt2[:, e * TQ:(e + 1) * TQ] + bias
                        stage_ref[s, rows, e * TQ:(e + 1) * TQ] = lt
                        m_loc[h] = jnp.maximum(m_loc[h], _fold_rows(lt, jnp.maximum))
        return m_loc

    def advance_max(m_run, m_loc):
        return tuple(jnp.maximum(m_run[h], jnp.max(m_loc[h], axis=0, keepdims=True)) for h in range(ATT_HEADS))

    def pv_span(c0, m_before, m_shift):
        def body(_, carry):
            off = pl.multiple_of(c0 * CH, UNROLL * CH)
            for s in range(n_pairs):
                g = (2 * s) // (ATT_HEADS // ATT_KV_HEADS)
                vgt = vt_ref[0, g * VT_ROWS:(g + 1) * VT_ROWS, pl.ds(off, UNROLL * CH)]
                pv = jnp.dot(vgt, p_ref[s], preferred_element_type=F32)
                for e in range(2):
                    h = 2 * s + e
                    cols = slice(e * TQ, (e + 1) * TQ)
                    acc_ref[s, :, cols] = jnp.exp2(m_before[h] - m_shift[h]) * acc_ref[s, :, cols] + pv[:, cols]
            return carry
        lax.fori_loop(0, jnp.minimum(n_span, 1), body, 0)

    def middle_span(i, carry):
        m_before, m_run = carry
        m_loc = span_step(i * UNROLL, (i - 1) * UNROLL, UNROLL, m_run, new=True, old=True)
        pv_span((i - 1) * UNROLL, m_before, m_run)
        return m_run, advance_max(m_run, m_loc)

    acc_ref[...] = jnp.zeros(acc_ref.shape, F32)
    m_none = tuple(jnp.full((1, TQ), -jnp.inf, F32) for _ in range(ATT_HEADS))
    m_first = advance_max(m_none, span_step(0, 0, UNROLL, None, new=True, old=False))
    m_before, m_fin = lax.fori_loop(1, n_span, middle_span, (m_none, m_first))
    span_step(0, (n_span - 1) * UNROLL, UNROLL, m_fin, new=False, old=True)
    pv_span((n_span - 1) * UNROLL, m_before, m_fin)

    for s in range(n_pairs):
        o_t = jnp.concatenate(
            [acc_ref[s, :HEAD_DIM, e * TQ:(e + 1) * TQ] / acc_ref[s, HEAD_DIM:HEAD_DIM + 1, e * TQ:(e + 1) * TQ]
             for e in range(2)], axis=0)
        o_ref[0, :, s * LANES:(s + 1) * LANES] = o_t.T.astype(o_ref.dtype)


def _dsa_attention(qi, wit, q, ki2, kd, vt, topk):
    b, s, _ = q.shape
    lk = kd.shape[1]
    assert s % TQ == 0 and lk % (UNROLL * CH) == 0 and lk >= N_META + s and topk <= CH
    qblk = lambda bi, j: (bi, j, 0)
    kblk = lambda bi, j: (bi, 0, 0)
    return pl.pallas_call(
        functools.partial(_attn_kernel, topk),
        grid=(b, s // TQ),
        in_specs=[
            pl.BlockSpec((1, TQ, IDX_HEADS * IDX_DIM), qblk),
            pl.BlockSpec((1, IDX_HEADS, TQ), lambda bi, j: (bi, 0, j)),
            pl.BlockSpec((1, TQ, ATT_W), qblk),
            pl.BlockSpec((1, lk, LANES), kblk),
            pl.BlockSpec((1, lk, 2 * KV_W), kblk),
            pl.BlockSpec((1, ATT_KV_HEADS * VT_ROWS, lk), kblk),
        ],
        out_specs=pl.BlockSpec((1, TQ, ATT_W), qblk),
        out_shape=jax.ShapeDtypeStruct((b, s, ATT_W), BF16),
        scratch_shapes=[
            pltpu.VMEM((lk, TQ), I32),
            pltpu.VMEM((CH, TQ), I32),
            pltpu.VMEM((ATT_HEADS // 2, UNROLL * CH, 2 * TQ), F32),
            pltpu.VMEM((ATT_HEADS // 2, UNROLL * CH, 2 * TQ), BF16),
            pltpu.VMEM((IDX_HEADS // 2, LANES, 2 * TQ), BF16),
            pltpu.VMEM((ATT_HEADS // 2, LANES, 2 * TQ), BF16),
            pltpu.VMEM((SUBLANES, TQ), I32),
            pltpu.VMEM((ATT_HEADS // 2, VT_ROWS, 2 * TQ), F32),
        ],
        compiler_params=pltpu.CompilerParams(
            dimension_semantics=("arbitrary", "arbitrary"), vmem_limit_bytes=VMEM_LIMIT),
        name="dsa_attn",
    )(qi, wit, q, ki2, kd, vt)


def _merge_ln_kernel(att_ref, cu_ref, halo_ref, gb_ref, h_ref, cw_ref,
                     wg_ref, wa_ref, wc_ref, wo_ref, g_ref, b_ref, o_ref):
    tm = cu_ref.shape[1]
    hb = h_ref[0].astype(BF16)
    g_att = jnp.dot(hb, wg_ref[:, :D_MODEL], preferred_element_type=F32)
    g_conv = jnp.dot(hb, wg_ref[:, D_MODEL:], preferred_element_type=F32)
    cu = cu_ref[0]
    halo = halo_ref[0, 0]
    row = lax.broadcasted_iota(I32, (tm, CONV_W), 0)
    prev1 = jnp.where(row == 0, halo[7:8], pltpu.roll(cu, 1, 0))
    prev2 = jnp.where(row == 0, halo[6:7], jnp.where(row == 1, halo[7:8], pltpu.roll(cu, 2, 0)))
    cw = cw_ref[...]
    conv = cw[0:1] * prev2 + cw[1:2] * prev1 + cw[2:3] * cu
    y_conv = jnp.dot((gb_ref[0] * conv).astype(BF16), wc_ref[...], preferred_element_type=F32)
    y_att = jnp.dot(att_ref[0], wa_ref[...], preferred_element_type=F32)
    merged = jax.nn.sigmoid(g_att) * y_att + jax.nn.sigmoid(g_conv) * y_conv
    mix = jnp.dot(merged.astype(BF16), wo_ref[...], preferred_element_type=F32)
    o_ref[0] = _layer_norm(DEEPNORM_ALPHA * h_ref[0] + mix, g_ref[...], b_ref[...])


def _merge_ln(att, cu, halo, gb, h1, conv_w8, wg, wa, wc, wo, g, b):
    bsz, s, _ = cu.shape
    tm = s // halo.shape[1]
    blk = lambda bi, i: (bi, i, 0)
    const = lambda bi, i: (0, 0)
    return pl.pallas_call(
        _merge_ln_kernel,
        grid=(bsz, s // tm),
        in_specs=[
            pl.BlockSpec((1, tm, ATT_W), blk),
            pl.BlockSpec((1, tm, CONV_W), blk),
            pl.BlockSpec((1, 1, 8, CONV_W), lambda bi, i: (bi, i, 0, 0)),
            pl.BlockSpec((1, tm, CONV_W), blk),
            pl.BlockSpec((1, tm, D_MODEL), blk),
            pl.BlockSpec((8, CONV_W), const),
            pl.BlockSpec((D_MODEL, 2 * D_MODEL), const),
            pl.BlockSpec((ATT_W, D_MODEL), const),
            pl.BlockSpec((CONV_W, D_MODEL), const),
            pl.BlockSpec((D_MODEL, D_MODEL), const),
            pl.BlockSpec((1, D_MODEL), const),
            pl.BlockSpec((1, D_MODEL), const),
        ],
        out_specs=pl.BlockSpec((1, tm, D_MODEL), blk),
        out_shape=jax.ShapeDtypeStruct((bsz, s, D_MODEL), F32),
        compiler_params=pltpu.CompilerParams(
            dimension_semantics=("arbitrary", "arbitrary"), vmem_limit_bytes=VMEM_LIMIT),
        name="merge_ln",
    )(att, cu, halo, gb, h1, conv_w8, wg, wa, wc, wo, g, b)


def _rope_tables(pos):
    half = HEAD_DIM // 2
    inv_freq = ROPE_THETA ** (-jnp.arange(half, dtype=F32) / half)
    ang = pos.astype(F32)[:, None] * inv_freq[None, :]
    cos, sin = jnp.cos(ang), jnp.sin(ang)
    return jnp.tile(cos, (1, LANES // half)), jnp.tile(jnp.concatenate([-sin, sin], axis=1), (1, LANES // HEAD_DIM))


def _split_w_in(w_in):
    widths = (ATT_W, KV_W, KV_W, IDX_HEADS * IDX_DIM, IDX_DIM, IDX_HEADS,
              CONV_W, CONV_W, CONV_W, D_MODEL, D_MODEL)
    offs = np.cumsum(widths)[:-1].tolist()
    wq, wk, wv, wqi, wki, wwi, wu, wgb, wgc, wga, wgv = jnp.split(w_in, offs, axis=1)
    dup = lambda w: jnp.concatenate([w[:, :HEAD_DIM], w[:, :HEAD_DIM], w[:, HEAD_DIM:], w[:, HEAD_DIM:]], axis=1)
    w_a = jnp.concatenate(
        [wq, wqi, dup(wk), wki, wki, dup(wv), wwi, jnp.zeros((D_MODEL, LANES - IDX_HEADS), w_in.dtype)], axis=1)
    w_b = jnp.concatenate([wu, wgc, wgb], axis=1)
    w_g = jnp.concatenate([wga, wgv], axis=1)
    return w_a.astype(BF16), w_b.astype(BF16), w_g.astype(BF16)


def kernel(x, meta_tokens, ffn1_w_gate, ffn1_w_up, ffn1_w_down, ln1_g, ln1_b, w_in, conv_w, w_att_out, w_conv_out, w_o, ln2_g, ln2_b, ffn2_w_gate, ffn2_w_up, ffn2_w_down, ln3_g, ln3_b):
    bsz, seq, _ = x.shape
    l_tot = N_META + seq
    topk = min(TOPK_MAX, l_tot // 4)
    lk = -(-l_tot // (UNROLL * CH)) * (UNROLL * CH)
    lyr = 0
    bf = lambda w: w.astype(BF16)
    vec = lambda v: v[lyr][None, :]

    w1 = (bf(ffn1_w_gate[lyr]), bf(ffn1_w_up[lyr]), bf(ffn1_w_down[lyr]), vec(ln1_g), vec(ln1_b))
    w_a, w_b, w_g = _split_w_in(w_in[lyr])
    pos = jnp.arange(l_tot, dtype=I32)
    cos, sin = _rope_tables(pos)

    hm = _ffn_ln(meta_tokens.astype(F32), *w1)
    _, _, kd_m, ki_m, vd_m, _ = _proj_rope(hm, w_a, cos[:N_META], sin[:N_META], N_META)
    cu_m = _proj_gate(hm, w_b)[0]

    h1 = _ffn_ln(x.reshape(bsz * seq, D_MODEL), *w1)
    q, qi, kd, ki2, vd, wi = _proj_rope(h1, w_a, cos[N_META:], sin[N_META:], seq)
    cu, gb = _proj_gate(h1, w_b)

    def with_meta(meta_rows, real):
        wdt = real.shape[-1]
        return jnp.concatenate(
            [jnp.broadcast_to(meta_rows[None], (bsz, N_META, wdt)), real.reshape(bsz, seq, wdt),
             jnp.zeros((bsz, lk - l_tot, wdt), real.dtype)], axis=1)

    per_seq = lambda a: a.reshape(bsz, seq, a.shape[-1])
    v_keys = with_meta(vd_m, vd)
    ones_pad = jnp.zeros((bsz, lk, VT_ROWS - HEAD_DIM), BF16).at[:, :, 0].set(1.0)
    vt = jnp.swapaxes(jnp.concatenate(
        [v_keys[:, :, :HEAD_DIM], ones_pad, v_keys[:, :, 2 * HEAD_DIM:3 * HEAD_DIM], ones_pad], axis=2), 1, 2)
    wit = jnp.swapaxes(per_seq(wi)[:, :, :IDX_HEADS], 1, 2)
    att = _dsa_attention(per_seq(qi), wit, per_seq(q), with_meta(ki_m, ki2), with_meta(kd_m, kd), vt, topk)

    tm = min(seq, 512)
    cu3 = per_seq(cu)
    tails = cu3.reshape(bsz, seq // tm, tm, CONV_W)[:, :-1, tm - 8:, :]
    halo = jnp.concatenate([jnp.broadcast_to(cu_m[None, None, N_META - 8:], (bsz, 1, 8, CONV_W)), tails], axis=1)
    conv_w8 = jnp.concatenate([conv_w[lyr].astype(F32), jnp.zeros((8 - CONV_K, CONV_W), F32)], axis=0)
    h2 = _merge_ln(att, cu3, halo, per_seq(gb), per_seq(h1), conv_w8, w_g,
                   bf(w_att_out[lyr]), bf(w_conv_out[lyr]), bf(w_o[lyr]), vec(ln2_g), vec(ln2_b))

    h3 = _ffn_ln(h2.reshape(bsz * seq, D_MODEL), bf(ffn2_w_gate[lyr]), bf(ffn2_w_up[lyr]), bf(ffn2_w_down[lyr]),
                 vec(ln3_g), vec(ln3_b))
    return h3.reshape(bsz, seq, D_MODEL)
```

```python
import functools
import math

import numpy as np
import jax
import jax.numpy as jnp
from jax import lax
from jax.experimental import pallas as pl
from jax.experimental.pallas import tpu as pltpu

F32 = jnp.float32
BF16 = jnp.bfloat16
I32 = jnp.int32

D_MODEL = 1024
N_META = 16
ATT_HEADS = 8
ATT_KV_HEADS = 2
HEAD_DIM = 64
ATT_W = ATT_HEADS * HEAD_DIM
KV_W = ATT_KV_HEADS * HEAD_DIM
IDX_HEADS = 8
IDX_DIM = 64
TOPK_MAX = 256
ROPE_THETA = 10000.0
CONV_W = D_MODEL // 2
CONV_K = 3
D_FF = 2816
LN_EPS = 1e-5
DEPTH = 1
DEEPNORM_ALPHA = (2.0 * DEPTH) ** 0.25

LANES = 128
SUBLANES = 8
TQ = 256
CH = 256
VT_ROWS = 80
UNROLL = 4
HALF = UNROLL // 2
STEPS_PER_EXIT_TEST = 2
UNTESTED_STEPS = 12
INTERP_STEPS = 10
KEY_STEP_PERIOD = 4
MAX_SEARCH_STEPS = 160
INT_MAX = 2 ** 31 - 1
INT_MIN = -(2 ** 31)
LOG2E = 1.4426950408889634
NEG_BIG = -1e30
VMEM_LIMIT = 52 * 1024 * 1024


def _layer_norm(y, g, b):
    mu = jnp.mean(y, axis=-1, keepdims=True)
    yc = y - mu
    var = jnp.mean(yc * yc, axis=-1, keepdims=True)
    return yc * lax.rsqrt(var + LN_EPS) * g + b


def _ffn_ln_kernel(x_ref, wg_ref, wu_ref, wd_ref, g_ref, b_ref, o_ref, xb_ref):
    f = pl.program_id(1)

    @pl.when(f == 0)
    def _():
        o_ref[...] = jnp.zeros_like(o_ref)
        xb_ref[...] = x_ref[...].astype(BF16)

    xb = xb_ref[...]
    gt = jnp.dot(xb, wg_ref[...], preferred_element_type=F32)
    up = jnp.dot(xb, wu_ref[...], preferred_element_type=F32)
    a = (gt * jax.nn.sigmoid(gt)) * up
    o_ref[...] += jnp.dot(a.astype(BF16), wd_ref[...], preferred_element_type=F32)

    @pl.when(f == pl.num_programs(1) - 1)
    def _():
        tr = min(x_ref.shape[0], 256)

        def norm_rows(r, carry):
            rows = pl.ds(pl.multiple_of(r * tr, tr), tr)
            y = DEEPNORM_ALPHA * x_ref[rows, :] + 0.5 * o_ref[rows, :]
            o_ref[rows, :] = _layer_norm(y, g_ref[...], b_ref[...])
            return carry

        lax.fori_loop(0, x_ref.shape[0] // tr, norm_rows, 0)


def _ffn_ln(x2d, wg, wu, wd, g, b):
    n = x2d.shape[0]
    tm = min(n, 2048)
    tf = 256
    assert n % tm == 0 and D_FF % tf == 0
    return pl.pallas_call(
        _ffn_ln_kernel,
        grid=(n // tm, D_FF // tf),
        in_specs=[
            pl.BlockSpec((tm, D_MODEL), lambda i, f: (i, 0)),
            pl.BlockSpec((D_MODEL, tf), lambda i, f: (0, f)),
            pl.BlockSpec((D_MODEL, tf), lambda i, f: (0, f)),
            pl.BlockSpec((tf, D_MODEL), lambda i, f: (f, 0)),
            pl.BlockSpec((1, D_MODEL), lambda i, f: (0, 0)),
            pl.BlockSpec((1, D_MODEL), lambda i, f: (0, 0)),
        ],
        out_specs=pl.BlockSpec((tm, D_MODEL), lambda i, f: (i, 0)),
        out_shape=jax.ShapeDtypeStruct((n, D_MODEL), F32),
        scratch_shapes=[pltpu.VMEM((tm, D_MODEL), BF16)],
        compiler_params=pltpu.CompilerParams(
            dimension_semantics=("arbitrary", "arbitrary"), vmem_limit_bytes=VMEM_LIMIT),
        name="ffn_ln",
    )(x2d, wg, wu, wd, g, b)


_PROJ_A_W = ATT_W + IDX_HEADS * IDX_DIM + 2 * KV_W + LANES + 2 * KV_W + LANES


def _proj_rope_kernel(x_ref, w_ref, cos_ref, sin_ref, q_ref, qi_ref, kd_ref, ki_ref, vd_ref, wi_ref):
    tm = x_ref.shape[0]
    xb = x_ref[...].astype(BF16)
    cos = cos_ref[...]
    sin = sin_ref[...]
    lane = lax.broadcasted_iota(I32, (tm, LANES), 1)
    first_half = (lane & (HEAD_DIM // 2)) == 0

    def rope(s):
        partner = jnp.where(first_half, pltpu.roll(s, LANES - HEAD_DIM // 2, 1),
                            pltpu.roll(s, HEAD_DIM // 2, 1))
        return s * cos + partner * sin

    def dot_cols(c0, width):
        return jnp.dot(xb, w_ref[:, c0:c0 + width], preferred_element_type=F32)

    outs = ((q_ref, ATT_W, HEAD_DIM ** -0.5 * LOG2E), (qi_ref, IDX_HEADS * IDX_DIM, 1.0), (kd_ref, 2 * KV_W, 1.0))
    c0 = 0
    for ref, width, scale in outs:
        for j in range(0, width, 2 * LANES):
            p = dot_cols(c0 + j, 2 * LANES)
            for s in range(2):
                r = rope(p[:, s * LANES:(s + 1) * LANES])
                if scale != 1.0:
                    r = r * scale
                ref[:, j + s * LANES:j + (s + 1) * LANES] = r.astype(ref.dtype)
        c0 += width
    p = dot_cols(c0, LANES)
    ki_ref[...] = rope(p).astype(ki_ref.dtype)
    c0 += LANES
    vd_ref[...] = dot_cols(c0, 2 * KV_W).astype(vd_ref.dtype)
    c0 += 2 * KV_W
    wi_ref[...] = dot_cols(c0, LANES) * ((IDX_HEADS * IDX_DIM) ** -0.5)


def _proj_rope(h2d, w_a, cos, sin, rows_per_seq):
    n = h2d.shape[0]
    tm = min(rows_per_seq, 512)
    tiles_per_seq = rows_per_seq // tm
    assert n % tm == 0 and rows_per_seq % tm == 0
    row = lambda i: (i, 0)
    tab = lambda i: (i % tiles_per_seq, 0)
    widths = (ATT_W, IDX_HEADS * IDX_DIM, 2 * KV_W, LANES, 2 * KV_W, LANES)
    dtypes = (BF16, BF16, BF16, BF16, BF16, F32)
    return pl.pallas_call(
        _proj_rope_kernel,
        grid=(n // tm,),
        in_specs=[
            pl.BlockSpec((tm, D_MODEL), row),
            pl.BlockSpec((D_MODEL, _PROJ_A_W), lambda i: (0, 0)),
            pl.BlockSpec((tm, LANES), tab),
            pl.BlockSpec((tm, LANES), tab),
        ],
        out_specs=[pl.BlockSpec((tm, w), row) for w in widths],
        out_shape=[jax.ShapeDtypeStruct((n, w), dt) for w, dt in zip(widths, dtypes)],
        compiler_params=pltpu.CompilerParams(
            dimension_semantics=("arbitrary",), vmem_limit_bytes=VMEM_LIMIT),
        name="proj_rope",
    )(h2d, w_a, cos, sin)


_PROJ_B_W = 3 * CONV_W


def _proj_gate_kernel(x_ref, w_ref, cu_ref, gb_ref):
    xb = x_ref[...].astype(BF16)

    def dot_cols(c0, width):
        return jnp.dot(xb, w_ref[:, c0:c0 + width], preferred_element_type=F32)

    cu_ref[...] = dot_cols(0, CONV_W) * dot_cols(CONV_W, CONV_W)
    gb_ref[...] = dot_cols(2 * CONV_W, CONV_W)


def _proj_gate(h2d, w_b):
    n = h2d.shape[0]
    tm = min(n, 512)
    assert n % tm == 0
    row = lambda i: (i, 0)
    widths = (CONV_W, CONV_W)
    return pl.pallas_call(
        _proj_gate_kernel,
        grid=(n // tm,),
        in_specs=[pl.BlockSpec((tm, D_MODEL), row), pl.BlockSpec((D_MODEL, _PROJ_B_W), lambda i: (0, 0))],
        out_specs=[pl.BlockSpec((tm, w), row) for w in widths],
        out_shape=[jax.ShapeDtypeStruct((n, w), F32) for w in widths],
        compiler_params=pltpu.CompilerParams(
            dimension_semantics=("arbitrary",), vmem_limit_bytes=VMEM_LIMIT),
        name="proj_gate",
    )(h2d, w_b)


def _fold_rows(x, op):
    parts = [x[i:i + SUBLANES] for i in range(0, x.shape[0], SUBLANES)]
    while len(parts) > 1:
        parts = [op(parts[i], parts[i + 1]) for i in range(0, len(parts), 2)]
    return parts[0]


def _key_to_val(key):
    return pltpu.bitcast(jnp.where(key < 0, key ^ 0x7FFFFFFF, key), F32)


def _val_to_key(val):
    bits = pltpu.bitcast(jnp.where(val == 0.0, 0.0, val), I32)
    return jnp.where(bits < 0, bits ^ 0x7FFFFFFF, bits)


def _chunk_loop(nch, body, init):
    n_main = nch // UNROLL

    def main(i, carry):
        for u in range(UNROLL):
            carry = body(i * UNROLL + u, carry)
        return carry

    carry = lax.fori_loop(0, n_main, main, init)
    return lax.fori_loop(n_main * UNROLL, nch, body, carry)


def _attn_kernel(topk, qi_ref, wit_ref, q_ref, ki_ref, kd_ref, vt_ref, o_ref,
                 keys_ref, kmax_ref, stage_ref, p_ref, qit_ref, qt_ref, j_ref, acc_ref):
    jblk = pl.program_id(1)
    qpos0 = N_META + jblk * TQ
    nch = (qpos0 + TQ - 1) // CH + 1
    kf = float(topk)
    n_pairs = ATT_HEADS // 2

    lane = lax.broadcasted_iota(I32, (TQ, LANES), 1)
    lo_half = lane < HEAD_DIM
    qpos_row = qpos0 + lax.broadcasted_iota(I32, (1, TQ), 1)
    searchable = (qpos_row + 1).astype(F32) > kf

    for s in range(n_pairs):
        for src, dst in ((qi_ref, qit_ref), (q_ref, qt_ref)):
            slab = src[0, :, s * LANES:(s + 1) * LANES].astype(F32)
            even = jnp.where(lo_half, slab, 0.0).T
            odd = jnp.where(lo_half, 0.0, slab).T
            dst[s] = jnp.concatenate([even, odd], axis=1).astype(BF16)

    kpos_iota = lax.broadcasted_iota(I32, (CH, TQ), 0)
    qpos = qpos0 + lax.broadcasted_iota(I32, (CH, TQ), 1)

    def score_chunk(c, carry):
        off = pl.multiple_of(c * CH, CH)
        kic = ki_ref[0, pl.ds(off, CH), :]
        acc = jnp.zeros((CH, TQ), F32)
        for s in range(IDX_HEADS // 2):
            d = jnp.dot(kic, qit_ref[s], preferred_element_type=F32)
            for e in range(2):
                w = wit_ref[0, 2 * s + e:2 * s + e + 1, :]
                acc = acc + jnp.maximum(d[:, e * TQ:(e + 1) * TQ], 0.0) * w
        key = jnp.where(off + kpos_iota <= qpos, _val_to_key(acc), INT_MIN)
        keys_ref[pl.ds(off, CH), :] = key
        kmax_ref[...] = jnp.maximum(kmax_ref[...], key)
        return carry

    n_full = (nch + 1) // UNROLL
    has_half = ((nch + 1) % UNROLL) // HALF
    c_half = n_full * UNROLL

    def score_span(c0, n_sub):
        for u in range(n_sub):
            score_chunk(c0 + u, 0)
        return 0

    kmax_ref[...] = jnp.full(kmax_ref.shape, INT_MIN, I32)
    lax.fori_loop(0, n_full, lambda i, c: score_span(i * UNROLL, UNROLL), 0)
    lax.fori_loop(0, has_half, lambda i, c: score_span(c_half, HALF), 0)

    def count(pred):
        def body(c, acc):
            off = pl.multiple_of(c * CH, CH)
            hit = jnp.where(pred(keys_ref[pl.ds(off, CH), :], off + kpos_iota), 1.0, 0.0)
            return acc + _fold_rows(hit, jnp.add)
        acc = _chunk_loop(nch, body, jnp.zeros((SUBLANES, TQ), F32))
        return jnp.sum(acc, axis=0, keepdims=True)

    kmax = kmax_ref[...]
    lo0 = jnp.min(_fold_rows(kmax, jnp.minimum), axis=0, keepdims=True)
    hi0 = jnp.minimum(jnp.max(_fold_rows(kmax, jnp.maximum), axis=0, keepdims=True), INT_MAX - 1) + 1

    def live(lo, hi, c_lo):
        return searchable & (c_lo != kf) & (hi - lo != 1)

    def n_live(lo, hi, c_lo):
        return jnp.max(jnp.where(live(lo, hi, c_lo), 1.0, 0.0))

    def pick(it, lo, hi, c_lo, c_hi):
        top = hi - 1
        v_lo, v_top = _key_to_val(lo), _key_to_val(top)
        log_lo = jnp.log(c_lo)
        frac = (log_lo - math.log(kf)) / (log_lo - jnp.log(jnp.maximum(c_hi, 0.5)))
        interpolate = (it >= 1) & (it <= INTERP_STEPS) & (c_lo < 1e8)
        frac = jnp.where(interpolate, jnp.minimum(jnp.maximum(frac, 0.1), 0.9), 0.5)
        mid_val = _val_to_key(v_lo + (v_top - v_lo) * frac)
        mid_key = lo + lax.shift_right_logical(hi - lo, 1)
        key_step = (it > INTERP_STEPS) & (it % KEY_STEP_PERIOD == KEY_STEP_PERIOD - 1)
        cand = jnp.where(key_step, mid_key, mid_val)
        cand = jnp.where((lo < 0) & (top >= 0), 0, cand)
        cand = jnp.where((lo == 0) & (top >= 1), 1, cand)
        cand = jnp.minimum(jnp.maximum(cand, lo + 1), top)
        return jnp.where(live(lo, hi, c_lo), cand, lo)

    def search_steps(st, n_steps):
        it, lo, hi, c_lo, c_hi = st
        for _ in range(n_steps):
            cand = pick(it, lo, hi, c_lo, c_hi)
            cnt = count(lambda k, pos: k >= cand)
            ok = cnt >= kf
            lo = jnp.where(ok, cand, lo)
            c_lo = jnp.where(ok, cnt, c_lo)
            hi = jnp.where(ok, hi, cand)
            c_hi = jnp.where(ok, c_hi, cnt)
            it = it + 1
        return it, lo, hi, c_lo, c_hi

    def bis_cond(st):
        return (st[0] < MAX_SEARCH_STEPS) & (st[5] > 0.0)

    def bis_body(st):
        st = search_steps(st[:5], STEPS_PER_EXIT_TEST)
        return st + (n_live(st[1], st[2], st[3]),)

    c_unknown = jnp.full((1, TQ), 1e9, F32)
    st = (jnp.int32(0), lo0, hi0, c_unknown, jnp.zeros((1, TQ), F32))
    st = lax.fori_loop(0, UNTESTED_STEPS // STEPS_PER_EXIT_TEST,
                       lambda _, st: search_steps(st, STEPS_PER_EXIT_TEST), st)
    _, t_fin, _, ct_fin, c_above, _ = lax.while_loop(bis_cond, bis_body, st + (n_live(st[1], st[2], st[3]),))
    t_fin = jnp.where(searchable, t_fin, INT_MIN)
    j_ref[...] = jnp.broadcast_to(jnp.where(searchable, 2 ** 30, -1).astype(I32), j_ref.shape)
    open_fin = jnp.max(jnp.where(searchable & (ct_fin != kf), 1.0, 0.0))

    @pl.when(open_fin > 0.0)
    def _():
        tie_q = searchable & (ct_fin != kf)
        need = kf - c_above
        tri = jnp.where(lax.broadcasted_iota(I32, (CH, CH), 0) >= lax.broadcasted_iota(I32, (CH, CH), 1),
                        1.0, 0.0).astype(BF16)

        def body(c, carry):
            seen, jmax = carry
            off = pl.multiple_of(c * CH, CH)
            is_tie = keys_ref[pl.ds(off, CH), :] == t_fin
            rank = jnp.dot(tri, jnp.where(is_tie, 1.0, 0.0).astype(BF16), preferred_element_type=F32) + seen
            kept_pos = jnp.where(is_tie & (rank <= need), off + kpos_iota, -1)
            return rank[CH - 1:CH, :], jnp.maximum(jmax, _fold_rows(kept_pos, jnp.maximum))

        _, jmax = _chunk_loop(nch, body, (jnp.zeros((1, TQ), F32), jnp.full((SUBLANES, TQ), -1, I32)))
        j_tie = jnp.max(jmax, axis=0, keepdims=True)
        j_ref[...] = jnp.where(tie_q, jnp.broadcast_to(j_tie, j_ref.shape), j_ref[...])

    j_fin = j_ref[0:1, :]

    def logits(c, s):
        g = (2 * s) // (ATT_HEADS // ATT_KV_HEADS)
        kg = kd_ref[0, pl.ds(pl.multiple_of(c * CH, CH), CH), g * LANES:(g + 1) * LANES]
        return jnp.dot(kg, qt_ref[s], preferred_element_type=F32)

    def span_step(c_new, n_new, n_old, m_old):
        m_loc = [jnp.full((SUBLANES, TQ), -jnp.inf, F32) for _ in range(ATT_HEADS)]
        for u in range(max(n_new, n_old)):
            rows = slice(u * CH, (u + 1) * CH)
            if u < n_new:
                off_new = pl.multiple_of((c_new + u) * CH, CH)
                k = keys_ref[pl.ds(off_new, CH), :]
                sel = (k > t_fin) | ((k == t_fin) & (off_new + kpos_iota <= j_fin))
                bias = jnp.where(sel, 0.0, NEG_BIG)
            for s in range(n_pairs):
                if u < n_old:
                    ps = [jnp.exp2(stage_ref[s, rows, e * TQ:(e + 1) * TQ] - m_old[2 * s + e]).astype(BF16)
                          for e in range(2)]
                    p_ref[s, rows, :] = jnp.concatenate(ps, axis=1)
                if u < n_new:
                    lt2 = logits(c_new + u, s)
                    for e in range(2):
                        h = 2 * s + e
                        lt = lt2[:, e * TQ:(e + 1) * TQ] + bias
                        stage_ref[s, rows, e * TQ:(e + 1) * TQ] = lt
                        m_loc[h] = jnp.maximum(m_loc[h], _fold_rows(lt, jnp.maximum))
        return m_loc

    def advance_max(m_run, m_loc):
        return tuple(jnp.maximum(m_run[h], jnp.max(m_loc[h], axis=0, keepdims=True)) for h in range(ATT_HEADS))

    def pv_span(c0, n_sub, m_acc, m_shift):
        def body(_, carry):
            off = pl.multiple_of(c0 * CH, HALF * CH)
            for s in range(n_pairs):
                g = (2 * s) // (ATT_HEADS // ATT_KV_HEADS)
                vgt = vt_ref[0, g * VT_ROWS:(g + 1) * VT_ROWS, pl.ds(off, n_sub * CH)]
                pv = jnp.dot(vgt, p_ref[s, :n_sub * CH, :], preferred_element_type=F32)
                for e in range(2):
                    h = 2 * s + e
                    cols = slice(e * TQ, (e + 1) * TQ)
                    acc_ref[s, :, cols] = jnp.exp2(m_acc[h] - m_shift[h]) * acc_ref[s, :, cols] + pv[:, cols]
            return carry
        lax.fori_loop(0, jnp.minimum(nch, 1), body, 0)

    def first_full(i, carry):
        return carry[0], advance_max(carry[1], span_step(0, UNROLL, 0, None))

    def next_full(i, carry):
        m_acc, m_run = carry
        m_loc = span_step(i * UNROLL, UNROLL, UNROLL, m_run)
        pv_span((i - 1) * UNROLL, UNROLL, m_acc, m_run)
        return m_run, advance_max(m_run, m_loc)

    def half_after_full(i, carry):
        m_acc, m_run = carry
        m_loc = span_step(c_half, HALF, UNROLL, m_run)
        pv_span(c_half - UNROLL, UNROLL, m_acc, m_run)
        return m_run, advance_max(m_run, m_loc)

    def half_alone(i, carry):
        return carry[0], advance_max(carry[1], span_step(c_half, HALF, 0, None))

    def drain_full(i, carry):
        m_acc, m_run = carry
        span_step(0, 0, UNROLL, m_run)
        pv_span(c_half - UNROLL, UNROLL, m_acc, m_run)
        return m_run, m_run

    def drain_half(i, carry):
        m_acc, m_run = carry
        span_step(0, 0, HALF, m_run)
        pv_span(c_half, HALF, m_acc, m_run)
        return m_run, m_run

    acc_ref[...] = jnp.zeros(acc_ref.shape, F32)
    m_none = tuple(jnp.full((1, TQ), -jnp.inf, F32) for _ in range(ATT_HEADS))
    has_full = jnp.minimum(n_full, 1)
    carry = lax.fori_loop(0, has_full, first_full, (m_none, m_none))
    carry = lax.fori_loop(1, n_full, next_full, carry)
    carry = lax.fori_loop(0, has_half * has_full, half_after_full, carry)
    carry = lax.fori_loop(0, has_half * (1 - has_full), half_alone, carry)
    carry = lax.fori_loop(0, (1 - has_half) * has_full, drain_full, carry)
    lax.fori_loop(0, has_half, drain_half, carry)

    for s in range(n_pairs):
        o_t = jnp.concatenate(
            [acc_ref[s, :HEAD_DIM, e * TQ:(e + 1) * TQ] / acc_ref[s, HEAD_DIM:HEAD_DIM + 1, e * TQ:(e + 1) * TQ]
             for e in range(2)], axis=0)
        o_ref[0, :, s * LANES:(s + 1) * LANES] = o_t.T.astype(o_ref.dtype)


def _dsa_attention(qi, wit, q, ki2, kd, vt, topk):
    b, s, _ = q.shape
    lk = kd.shape[1]
    assert s % TQ == 0 and lk % (UNROLL * CH) == 0 and lk >= N_META + s and topk <= CH
    qblk = lambda bi, j: (bi, j, 0)
    kblk = lambda bi, j: (bi, 0, 0)
    return pl.pallas_call(
        functools.partial(_attn_kernel, topk),
        grid=(b, s // TQ),
        in_specs=[
            pl.BlockSpec((1, TQ, IDX_HEADS * IDX_DIM), qblk),
            pl.BlockSpec((1, IDX_HEADS, TQ), lambda bi, j: (bi, 0, j)),
            pl.BlockSpec((1, TQ, ATT_W), qblk),
            pl.BlockSpec((1, lk, LANES), kblk),
            pl.BlockSpec((1, lk, 2 * KV_W), kblk),
            pl.BlockSpec((1, ATT_KV_HEADS * VT_ROWS, lk), kblk),
        ],
        out_specs=pl.BlockSpec((1, TQ, ATT_W), qblk),
        out_shape=jax.ShapeDtypeStruct((b, s, ATT_W), BF16),
        scratch_shapes=[
            pltpu.VMEM((lk, TQ), I32),
            pltpu.VMEM((CH, TQ), I32),
            pltpu.VMEM((ATT_HEADS // 2, UNROLL * CH, 2 * TQ), F32),
            pltpu.VMEM((ATT_HEADS // 2, UNROLL * CH, 2 * TQ), BF16),
            pltpu.VMEM((IDX_HEADS // 2, LANES, 2 * TQ), BF16),
            pltpu.VMEM((ATT_HEADS // 2, LANES, 2 * TQ), BF16),
            pltpu.VMEM((SUBLANES, TQ), I32),
            pltpu.VMEM((ATT_HEADS // 2, VT_ROWS, 2 * TQ), F32),
        ],
        compiler_params=pltpu.CompilerParams(
            dimension_semantics=("arbitrary", "arbitrary"), vmem_limit_bytes=VMEM_LIMIT),
        name="dsa_attn",
    )(qi, wit, q, ki2, kd, vt)


def _merge_ln_kernel(att_ref, cu_ref, halo_ref, gb_ref, h_ref, cw_ref,
                     wg_ref, wa_ref, wc_ref, wo_ref, g_ref, b_ref, o_ref):
    tm = cu_ref.shape[1]
    hb = h_ref[0].astype(BF16)
    g_att = jnp.dot(hb, wg_ref[:, :D_MODEL], preferred_element_type=F32)
    g_conv = jnp.dot(hb, wg_ref[:, D_MODEL:], preferred_element_type=F32)
    cu = cu_ref[0]
    halo = halo_ref[0, 0]
    row = lax.broadcasted_iota(I32, (tm, CONV_W), 0)
    prev1 = jnp.where(row == 0, halo[7:8], pltpu.roll(cu, 1, 0))
    prev2 = jnp.where(row == 0, halo[6:7], jnp.where(row == 1, halo[7:8], pltpu.roll(cu, 2, 0)))
    cw = cw_ref[...]
    conv = cw[0:1] * prev2 + cw[1:2] * prev1 + cw[2:3] * cu
    y_conv = jnp.dot((gb_ref[0] * conv).astype(BF16), wc_ref[...], preferred_element_type=F32)
    y_att = jnp.dot(att_ref[0], wa_ref[...], preferred_element_type=F32)
    merged = jax.nn.sigmoid(g_att) * y_att + jax.nn.sigmoid(g_conv) * y_conv
    mix = jnp.dot(merged.astype(BF16), wo_ref[...], preferred_element_type=F32)
    o_ref[0] = _layer_norm(DEEPNORM_ALPHA * h_ref[0] + mix, g_ref[...], b_ref[...])


def _merge_ln(att, cu, halo, gb, h1, conv_w8, wg, wa, wc, wo, g, b):
    bsz, s, _ = cu.shape
    tm = s // halo.shape[1]
    blk = lambda bi, i: (bi, i, 0)
    const = lambda bi, i: (0, 0)
    return pl.pallas_call(
        _merge_ln_kernel,
        grid=(bsz, s // tm),
        in_specs=[
            pl.BlockSpec((1, tm, ATT_W), blk),
            pl.BlockSpec((1, tm, CONV_W), blk),
            pl.BlockSpec((1, 1, 8, CONV_W), lambda bi, i: (bi, i, 0, 0)),
            pl.BlockSpec((1, tm, CONV_W), blk),
            pl.BlockSpec((1, tm, D_MODEL), blk),
            pl.BlockSpec((8, CONV_W), const),
            pl.BlockSpec((D_MODEL, 2 * D_MODEL), const),
            pl.BlockSpec((ATT_W, D_MODEL), const),
            pl.BlockSpec((CONV_W, D_MODEL), const),
            pl.BlockSpec((D_MODEL, D_MODEL), const),
            pl.BlockSpec((1, D_MODEL), const),
            pl.BlockSpec((1, D_MODEL), const),
        ],
        out_specs=pl.BlockSpec((1, tm, D_MODEL), blk),
        out_shape=jax.ShapeDtypeStruct((bsz, s, D_MODEL), F32),
        compiler_params=pltpu.CompilerParams(
            dimension_semantics=("arbitrary", "arbitrary"), vmem_limit_bytes=VMEM_LIMIT),
        name="merge_ln",
    )(att, cu, halo, gb, h1, conv_w8, wg, wa, wc, wo, g, b)


def _rope_tables(pos):
    half = HEAD_DIM // 2
    inv_freq = ROPE_THETA ** (-jnp.arange(half, dtype=F32) / half)
    ang = pos.astype(F32)[:, None] * inv_freq[None, :]
    cos, sin = jnp.cos(ang), jnp.sin(ang)
    return jnp.tile(cos, (1, LANES // half)), jnp.tile(jnp.concatenate([-sin, sin], axis=1), (1, LANES // HEAD_DIM))


def _split_w_in(w_in):
    widths = (ATT_W, KV_W, KV_W, IDX_HEADS * IDX_DIM, IDX_DIM, IDX_HEADS,
              CONV_W, CONV_W, CONV_W, D_MODEL, D_MODEL)
    offs = np.cumsum(widths)[:-1].tolist()
    wq, wk, wv, wqi, wki, wwi, wu, wgb, wgc, wga, wgv = jnp.split(w_in, offs, axis=1)
    dup = lambda w: jnp.concatenate([w[:, :HEAD_DIM], w[:, :HEAD_DIM], w[:, HEAD_DIM:], w[:, HEAD_DIM:]], axis=1)
    w_a = jnp.concatenate(
        [wq, wqi, dup(wk), wki, wki, dup(wv), wwi, jnp.zeros((D_MODEL, LANES - IDX_HEADS), w_in.dtype)], axis=1)
    w_b = jnp.concatenate([wu, wgc, wgb], axis=1)
    w_g = jnp.concatenate([wga, wgv], axis=1)
    return w_a.astype(BF16), w_b.astype(BF16), w_g.astype(BF16)


def kernel(x, meta_tokens, ffn1_w_gate, ffn1_w_up, ffn1_w_down, ln1_g, ln1_b, w_in, conv_w, w_att_out, w_conv_out, w_o, ln2_g, ln2_b, ffn2_w_gate, ffn2_w_up, ffn2_w_down, ln3_g, ln3_b):
    bsz, seq, _ = x.shape
    l_tot = N_META + seq
    topk = min(TOPK_MAX, l_tot // 4)
    lk = -(-l_tot // (UNROLL * CH)) * (UNROLL * CH)
    lyr = 0
    bf = lambda w: w.astype(BF16)
    vec = lambda v: v[lyr][None, :]

    w1 = (bf(ffn1_w_gate[lyr]), bf(ffn1_w_up[lyr]), bf(ffn1_w_down[lyr]), vec(ln1_g), vec(ln1_b))
    w_a, w_b, w_g = _split_w_in(w_in[lyr])
    pos = jnp.arange(l_tot, dtype=I32)
    cos, sin = _rope_tables(pos)

    hm = _ffn_ln(meta_tokens.astype(F32), *w1)
    _, _, kd_m, ki_m, vd_m, _ = _proj_rope(hm, w_a, cos[:N_META], sin[:N_META], N_META)
    cu_m = _proj_gate(hm, w_b)[0]

    h1 = _ffn_ln(x.reshape(bsz * seq, D_MODEL), *w1)
    q, qi, kd, ki2, vd, wi = _proj_rope(h1, w_a, cos[N_META:], sin[N_META:], seq)
    cu, gb = _proj_gate(h1, w_b)

    def with_meta(meta_rows, real):
        wdt = real.shape[-1]
        return jnp.concatenate(
            [jnp.broadcast_to(meta_rows[None], (bsz, N_META, wdt)), real.reshape(bsz, seq, wdt),
             jnp.zeros((bsz, lk - l_tot, wdt), real.dtype)], axis=1)

    per_seq = lambda a: a.reshape(bsz, seq, a.shape[-1])
    v_keys = with_meta(vd_m, vd)
    ones_pad = jnp.zeros((bsz, lk, VT_ROWS - HEAD_DIM), BF16).at[:, :, 0].set(1.0)
    vt = jnp.swapaxes(jnp.concatenate(
        [v_keys[:, :, :HEAD_DIM], ones_pad, v_keys[:, :, 2 * HEAD_DIM:3 * HEAD_DIM], ones_pad], axis=2), 1, 2)
    wit = jnp.swapaxes(per_seq(wi)[:, :, :IDX_HEADS], 1, 2)
    att = _dsa_attention(per_seq(qi), wit, per_seq(q), with_meta(ki_m, ki2), with_meta(kd_m, kd), vt, topk)

    tm = min(seq, 512)
    cu3 = per_seq(cu)
    tails = cu3.reshape(bsz, seq // tm, tm, CONV_W)[:, :-1, tm - 8:, :]
    halo = jnp.concatenate([jnp.broadcast_to(cu_m[None, None, N_META - 8:], (bsz, 1, 8, CONV_W)), tails], axis=1)
    conv_w8 = jnp.concatenate([conv_w[lyr].astype(F32), jnp.zeros((8 - CONV_K, CONV_W), F32)], axis=0)
    h2 = _merge_ln(att, cu3, halo, per_seq(gb), per_seq(h1), conv_w8, w_g,
                   bf(w_att_out[lyr]), bf(w_conv_out[lyr]), bf(w_o[lyr]), vec(ln2_g), vec(ln2_b))

    h3 = _ffn_ln(h2.reshape(bsz * seq, D_MODEL), bf(ffn2_w_gate[lyr]), bf(ffn2_w_up[lyr]), bf(ffn2_w_down[lyr]),
                 vec(ln3_g), vec(ln3_b))
    return h3.reshape(bsz, seq, D_MODEL)
```

```python
import functools
import math

import numpy as np
import jax
import jax.numpy as jnp
from jax import lax
from jax.experimental import pallas as pl
from jax.experimental.pallas import tpu as pltpu

F32 = jnp.float32
BF16 = jnp.bfloat16
I32 = jnp.int32

D_MODEL = 1024
N_META = 16
ATT_HEADS = 8
ATT_KV_HEADS = 2
HEAD_DIM = 64
ATT_W = ATT_HEADS * HEAD_DIM
KV_W = ATT_KV_HEADS * HEAD_DIM
IDX_HEADS = 8
IDX_DIM = 64
TOPK_MAX = 256
ROPE_THETA = 10000.0
CONV_W = D_MODEL // 2
CONV_K = 3
D_FF = 2816
LN_EPS = 1e-5
DEPTH = 1
DEEPNORM_ALPHA = (2.0 * DEPTH) ** 0.25

LANES = 128
SUBLANES = 8
TQ = 256
CH = 256
VT_ROWS = 80
UNROLL = 4
HALF = UNROLL // 2
STEPS_PER_EXIT_TEST = 2
UNTESTED_STEPS = 14
INTERP_STEPS = 10
KEY_STEP_PERIOD = 4
MAX_SEARCH_STEPS = 160
INT_MAX = 2 ** 31 - 1
INT_MIN = -(2 ** 31)
LOG2E = 1.4426950408889634
NEG_BIG = -1e30
VMEM_LIMIT = 52 * 1024 * 1024


def _layer_norm(y, g, b):
    mu = jnp.mean(y, axis=-1, keepdims=True)
    yc = y - mu
    var = jnp.mean(yc * yc, axis=-1, keepdims=True)
    return yc * lax.rsqrt(var + LN_EPS) * g + b


def _ffn_ln_kernel(x_ref, wg_ref, wu_ref, wd_ref, g_ref, b_ref, o_ref, xb_ref):
    f = pl.program_id(1)

    @pl.when(f == 0)
    def _():
        o_ref[...] = jnp.zeros_like(o_ref)
        xb_ref[...] = x_ref[...].astype(BF16)

    xb = xb_ref[...]
    gt = jnp.dot(xb, wg_ref[...], preferred_element_type=F32)
    up = jnp.dot(xb, wu_ref[...], preferred_element_type=F32)
    a = (gt * jax.nn.sigmoid(gt)) * up
    o_ref[...] += jnp.dot(a.astype(BF16), wd_ref[...], preferred_element_type=F32)

    @pl.when(f == pl.num_programs(1) - 1)
    def _():
        tr = min(x_ref.shape[0], 256)

        def norm_rows(r, carry):
            rows = pl.ds(pl.multiple_of(r * tr, tr), tr)
            y = DEEPNORM_ALPHA * x_ref[rows, :] + 0.5 * o_ref[rows, :]
            o_ref[rows, :] = _layer_norm(y, g_ref[...], b_ref[...])
            return carry

        lax.fori_loop(0, x_ref.shape[0] // tr, norm_rows, 0)


def _ffn_ln(x2d, wg, wu, wd, g, b):
    n = x2d.shape[0]
    tm = min(n, 2048)
    tf = 256 if tm > 256 else D_FF
    assert n % tm == 0 and D_FF % tf == 0
    return pl.pallas_call(
        _ffn_ln_kernel,
        grid=(n // tm, D_FF // tf),
        in_specs=[
            pl.BlockSpec((tm, D_MODEL), lambda i, f: (i, 0)),
            pl.BlockSpec((D_MODEL, tf), lambda i, f: (0, f)),
            pl.BlockSpec((D_MODEL, tf), lambda i, f: (0, f)),
            pl.BlockSpec((tf, D_MODEL), lambda i, f: (f, 0)),
            pl.BlockSpec((1, D_MODEL), lambda i, f: (0, 0)),
            pl.BlockSpec((1, D_MODEL), lambda i, f: (0, 0)),
        ],
        out_specs=pl.BlockSpec((tm, D_MODEL), lambda i, f: (i, 0)),
        out_shape=jax.ShapeDtypeStruct((n, D_MODEL), F32),
        scratch_shapes=[pltpu.VMEM((tm, D_MODEL), BF16)],
        compiler_params=pltpu.CompilerParams(
            dimension_semantics=("arbitrary", "arbitrary"), vmem_limit_bytes=VMEM_LIMIT),
        name="ffn_ln",
    )(x2d, wg, wu, wd, g, b)


_PROJ_A_W = ATT_W + IDX_HEADS * IDX_DIM + 2 * KV_W + LANES + 2 * KV_W + LANES


def _proj_rope_kernel(x_ref, w_ref, cos_ref, sin_ref, q_ref, qi_ref, kd_ref, ki_ref, vd_ref, wi_ref):
    tm = x_ref.shape[0]
    xb = x_ref[...].astype(BF16)
    cos = cos_ref[...]
    sin = sin_ref[...]
    lane = lax.broadcasted_iota(I32, (tm, LANES), 1)
    first_half = (lane & (HEAD_DIM // 2)) == 0

    def rope(s):
        partner = jnp.where(first_half, pltpu.roll(s, LANES - HEAD_DIM // 2, 1),
                            pltpu.roll(s, HEAD_DIM // 2, 1))
        return s * cos + partner * sin

    def dot_cols(c0, width):
        return jnp.dot(xb, w_ref[:, c0:c0 + width], preferred_element_type=F32)

    outs = ((q_ref, ATT_W, HEAD_DIM ** -0.5 * LOG2E), (qi_ref, IDX_HEADS * IDX_DIM, 1.0), (kd_ref, 2 * KV_W, 1.0))
    c0 = 0
    for ref, width, scale in outs:
        for j in range(0, width, 2 * LANES):
            p = dot_cols(c0 + j, 2 * LANES)
            for s in range(2):
                r = rope(p[:, s * LANES:(s + 1) * LANES])
                if scale != 1.0:
                    r = r * scale
                ref[:, j + s * LANES:j + (s + 1) * LANES] = r.astype(ref.dtype)
        c0 += width
    p = dot_cols(c0, LANES)
    ki_ref[...] = rope(p).astype(ki_ref.dtype)
    c0 += LANES
    vd_ref[...] = dot_cols(c0, 2 * KV_W).astype(vd_ref.dtype)
    c0 += 2 * KV_W
    wi_ref[...] = dot_cols(c0, LANES) * ((IDX_HEADS * IDX_DIM) ** -0.5)


def _proj_rope(h2d, w_a, cos, sin, rows_per_seq):
    n = h2d.shape[0]
    tm = min(rows_per_seq, 512)
    tiles_per_seq = rows_per_seq // tm
    assert n % tm == 0 and rows_per_seq % tm == 0
    row = lambda i: (i, 0)
    tab = lambda i: (i % tiles_per_seq, 0)
    widths = (ATT_W, IDX_HEADS * IDX_DIM, 2 * KV_W, LANES, 2 * KV_W, LANES)
    dtypes = (BF16, BF16, BF16, BF16, BF16, F32)
    return pl.pallas_call(
        _proj_rope_kernel,
        grid=(n // tm,),
        in_specs=[
            pl.BlockSpec((tm, D_MODEL), row),
            pl.BlockSpec((D_MODEL, _PROJ_A_W), lambda i: (0, 0)),
            pl.BlockSpec((tm, LANES), tab),
            pl.BlockSpec((tm, LANES), tab),
        ],
        out_specs=[pl.BlockSpec((tm, w), row) for w in widths],
        out_shape=[jax.ShapeDtypeStruct((n, w), dt) for w, dt in zip(widths, dtypes)],
        compiler_params=pltpu.CompilerParams(
            dimension_semantics=("arbitrary",), vmem_limit_bytes=VMEM_LIMIT),
        name="proj_rope",
    )(h2d, w_a, cos, sin)


_PROJ_B_W = 3 * CONV_W


def _proj_gate_kernel(x_ref, w_ref, cu_ref, gb_ref):
    xb = x_ref[...].astype(BF16)

    def dot_cols(c0, width):
        return jnp.dot(xb, w_ref[:, c0:c0 + width], preferred_element_type=F32)

    cu_ref[...] = dot_cols(0, CONV_W) * dot_cols(CONV_W, CONV_W)
    gb_ref[...] = dot_cols(2 * CONV_W, CONV_W)


def _proj_gate(h2d, w_b):
    n = h2d.shape[0]
    tm = min(n, 512)
    assert n % tm == 0
    row = lambda i: (i, 0)
    widths = (CONV_W, CONV_W)
    return pl.pallas_call(
        _proj_gate_kernel,
        grid=(n // tm,),
        in_specs=[pl.BlockSpec((tm, D_MODEL), row), pl.BlockSpec((D_MODEL, _PROJ_B_W), lambda i: (0, 0))],
        out_specs=[pl.BlockSpec((tm, w), row) for w in widths],
        out_shape=[jax.ShapeDtypeStruct((n, w), F32) for w in widths],
        compiler_params=pltpu.CompilerParams(
            dimension_semantics=("arbitrary",), vmem_limit_bytes=VMEM_LIMIT),
        name="proj_gate",
    )(h2d, w_b)


def _fold_rows(x, op):
    parts = [x[i:i + SUBLANES] for i in range(0, x.shape[0], SUBLANES)]
    while len(parts) > 1:
        parts = [op(parts[i], parts[i + 1]) for i in range(0, len(parts), 2)]
    return parts[0]


def _key_to_val(key):
    return pltpu.bitcast(jnp.where(key < 0, key ^ 0x7FFFFFFF, key), F32)


def _val_to_key(val):
    bits = pltpu.bitcast(jnp.where(val == 0.0, 0.0, val), I32)
    return jnp.where(bits < 0, bits ^ 0x7FFFFFFF, bits)


def _chunk_loop(nch, body, init):
    n_main = nch // UNROLL

    def main(i, carry):
        for u in range(UNROLL):
            carry = body(i * UNROLL + u, carry)
        return carry

    carry = lax.fori_loop(0, n_main, main, init)
    return lax.fori_loop(n_main * UNROLL, nch, body, carry)


def _attn_kernel(topk, qi_ref, wit_ref, q_ref, ki_ref, kd_ref, vt_ref, o_ref,
                 keys_ref, kmax_ref, stage_ref, p_ref, qit_ref, qt_ref, j_ref, acc_ref):
    jblk = pl.program_id(1)
    qpos0 = N_META + jblk * TQ
    nch = (qpos0 + TQ - 1) // CH + 1
    kf = float(topk)
    n_pairs = ATT_HEADS // 2

    lane = lax.broadcasted_iota(I32, (TQ, LANES), 1)
    lo_half = lane < HEAD_DIM
    qpos_row = qpos0 + lax.broadcasted_iota(I32, (1, TQ), 1)
    searchable = (qpos_row + 1).astype(F32) > kf

    for s in range(n_pairs):
        for src, dst in ((qi_ref, qit_ref), (q_ref, qt_ref)):
            slab = src[0, :, s * LANES:(s + 1) * LANES].astype(F32)
            even = jnp.where(lo_half, slab, 0.0).T
            odd = jnp.where(lo_half, 0.0, slab).T
            dst[s] = jnp.concatenate([even, odd], axis=1).astype(BF16)

    kpos_iota = lax.broadcasted_iota(I32, (CH, TQ), 0)
    qpos = qpos0 + lax.broadcasted_iota(I32, (CH, TQ), 1)

    def score_chunk(c, carry):
        off = pl.multiple_of(c * CH, CH)
        kic = ki_ref[0, pl.ds(off, CH), :]
        acc = jnp.zeros((CH, TQ), F32)
        for s in range(IDX_HEADS // 2):
            d = jnp.dot(kic, qit_ref[s], preferred_element_type=F32)
            for e in range(2):
                w = wit_ref[0, 2 * s + e:2 * s + e + 1, :]
                acc = acc + jnp.maximum(d[:, e * TQ:(e + 1) * TQ], 0.0) * w
        key = jnp.where(off + kpos_iota <= qpos, _val_to_key(acc), INT_MIN)
        keys_ref[pl.ds(off, CH), :] = key
        kmax_ref[...] = jnp.maximum(kmax_ref[...], key)
        return carry

    n_full = (nch + 1) // UNROLL
    has_half = ((nch + 1) % UNROLL) // HALF
    c_half = n_full * UNROLL

    def score_span(c0, n_sub):
        for u in range(n_sub):
            score_chunk(c0 + u, 0)
        return 0

    kmax_ref[...] = jnp.full(kmax_ref.shape, INT_MIN, I32)
    lax.fori_loop(0, n_full, lambda i, c: score_span(i * UNROLL, UNROLL), 0)
    lax.fori_loop(0, has_half, lambda i, c: score_span(c_half, HALF), 0)

    def count(pred):
        def body(c, acc):
            off = pl.multiple_of(c * CH, CH)
            hit = jnp.where(pred(keys_ref[pl.ds(off, CH), :], off + kpos_iota), 1.0, 0.0)
            return acc + _fold_rows(hit, jnp.add)
        acc = _chunk_loop(nch, body, jnp.zeros((SUBLANES, TQ), F32))
        return jnp.sum(acc, axis=0, keepdims=True)

    kmax = kmax_ref[...]
    lo0 = jnp.min(_fold_rows(kmax, jnp.minimum), axis=0, keepdims=True)
    hi0 = jnp.minimum(jnp.max(_fold_rows(kmax, jnp.maximum), axis=0, keepdims=True), INT_MAX - 1) + 1

    def live(lo, hi, c_lo):
        return searchable & (c_lo != kf) & (hi - lo != 1)

    def n_live(lo, hi, c_lo):
        return jnp.max(jnp.where(live(lo, hi, c_lo), 1.0, 0.0))

    def pick(it, lo, hi, c_lo, c_hi):
        top = hi - 1
        v_lo, v_top = _key_to_val(lo), _key_to_val(top)
        log_lo = jnp.log(c_lo)
        frac = (log_lo - math.log(kf)) / (log_lo - jnp.log(jnp.maximum(c_hi, 0.5)))
        interpolate = (it >= 1) & (it <= INTERP_STEPS) & (c_lo < 1e8)
        frac = jnp.where(interpolate, jnp.minimum(jnp.maximum(frac, 0.1), 0.9), 0.5)
        mid_val = _val_to_key(v_lo + (v_top - v_lo) * frac)
        mid_key = lo + lax.shift_right_logical(hi - lo, 1)
        key_step = (it > INTERP_STEPS) & (it % KEY_STEP_PERIOD == KEY_STEP_PERIOD - 1)
        cand = jnp.where(key_step, mid_key, mid_val)
        cand = jnp.where((lo < 0) & (top >= 0), 0, cand)
        cand = jnp.where((lo == 0) & (top >= 1), 1, cand)
        cand = jnp.minimum(jnp.maximum(cand, lo + 1), top)
        return jnp.where(live(lo, hi, c_lo), cand, lo)

    def search_steps(st, n_steps):
        it, lo, hi, c_lo, c_hi = st
        for _ in range(n_steps):
            cand = pick(it, lo, hi, c_lo, c_hi)
            cnt = count(lambda k, pos: k >= cand)
            ok = cnt >= kf
            lo = jnp.where(ok, cand, lo)
            c_lo = jnp.where(ok, cnt, c_lo)
            hi = jnp.where(ok, hi, cand)
            c_hi = jnp.where(ok, c_hi, cnt)
            it = it + 1
        return it, lo, hi, c_lo, c_hi

    def bis_cond(st):
        return (st[0] < MAX_SEARCH_STEPS) & (st[5] > 0.0)

    def bis_body(st):
        st = search_steps(st[:5], STEPS_PER_EXIT_TEST)
        return st + (n_live(st[1], st[2], st[3]),)

    c_unknown = jnp.full((1, TQ), 1e9, F32)
    st = (jnp.int32(0), lo0, hi0, c_unknown, jnp.zeros((1, TQ), F32))
    st = lax.fori_loop(0, UNTESTED_STEPS // STEPS_PER_EXIT_TEST,
                       lambda _, st: search_steps(st, STEPS_PER_EXIT_TEST), st)
    _, t_fin, _, ct_fin, c_above, _ = lax.while_loop(bis_cond, bis_body, st + (n_live(st[1], st[2], st[3]),))
    t_fin = jnp.where(searchable, t_fin, INT_MIN)
    j_ref[...] = jnp.broadcast_to(jnp.where(searchable, 2 ** 30, -1).astype(I32), j_ref.shape)
    open_fin = jnp.max(jnp.where(searchable & (ct_fin != kf), 1.0, 0.0))

    @pl.when(open_fin > 0.0)
    def _():
        tie_q = searchable & (ct_fin != kf)
        need = kf - c_above
        tri = jnp.where(lax.broadcasted_iota(I32, (CH, CH), 0) >= lax.broadcasted_iota(I32, (CH, CH), 1),
                        1.0, 0.0).astype(BF16)

        def body(c, carry):
            seen, jmax = carry
            off = pl.multiple_of(c * CH, CH)
            is_tie = keys_ref[pl.ds(off, CH), :] == t_fin
            rank = jnp.dot(tri, jnp.where(is_tie, 1.0, 0.0).astype(BF16), preferred_element_type=F32) + seen
            kept_pos = jnp.where(is_tie & (rank <= need), off + kpos_iota, -1)
            return rank[CH - 1:CH, :], jnp.maximum(jmax, _fold_rows(kept_pos, jnp.maximum))

        _, jmax = _chunk_loop(nch, body, (jnp.zeros((1, TQ), F32), jnp.full((SUBLANES, TQ), -1, I32)))
        j_tie = jnp.max(jmax, axis=0, keepdims=True)
        j_ref[...] = jnp.where(tie_q, jnp.broadcast_to(j_tie, j_ref.shape), j_ref[...])

    j_fin = j_ref[0:1, :]

    def logits(c, s):
        g = (2 * s) // (ATT_HEADS // ATT_KV_HEADS)
        kg = kd_ref[0, pl.ds(pl.multiple_of(c * CH, CH), CH), g * LANES:(g + 1) * LANES]
        return jnp.dot(kg, qt_ref[s], preferred_element_type=F32)

    def span_step(c_new, n_new, n_old, m_old):
        m_loc = [jnp.full((SUBLANES, TQ), -jnp.inf, F32) for _ in range(ATT_HEADS)]
        for u in range(max(n_new, n_old)):
            rows = slice(u * CH, (u + 1) * CH)
            if u < n_new:
                off_new = pl.multiple_of((c_new + u) * CH, CH)
                k = keys_ref[pl.ds(off_new, CH), :]
                sel = (k > t_fin) | ((k == t_fin) & (off_new + kpos_iota <= j_fin))
                bias = jnp.where(sel, 0.0, NEG_BIG)
            for s in range(n_pairs):
                if u < n_old:
                    ps = [jnp.exp2(stage_ref[s, rows, e * TQ:(e + 1) * TQ] - m_old[2 * s + e]).astype(BF16)
                          for e in range(2)]
                    p_ref[s, rows, :] = jnp.concatenate(ps, axis=1)
                if u < n_new:
                    lt2 = logits(c_new + u, s)
                    for e in range(2):
                        h = 2 * s + e
                        lt = lt2[:, e * TQ:(e + 1) * TQ] + bias
                        stage_ref[s, rows, e * TQ:(e + 1) * TQ] = lt
                        m_loc[h] = jnp.maximum(m_loc[h], _fold_rows(lt, jnp.maximum))
        return m_loc

    def advance_max(m_run, m_loc):
        return tuple(jnp.maximum(m_run[h], jnp.max(m_loc[h], axis=0, keepdims=True)) for h in range(ATT_HEADS))

    def pv_span(c0, n_sub, m_acc, m_shift):
        def body(_, carry):
            off = pl.multiple_of(c0 * CH, HALF * CH)
            for s in range(n_pairs):
                g = (2 * s) // (ATT_HEADS // ATT_KV_HEADS)
                vgt = vt_ref[0, g * VT_ROWS:(g + 1) * VT_ROWS, pl.ds(off, n_sub * CH)]
                pv = jnp.dot(vgt, p_ref[s, :n_sub * CH, :], preferred_element_type=F32)
                for e in range(2):
                    h = 2 * s + e
                    cols = slice(e * TQ, (e + 1) * TQ)
                    acc_ref[s, :, cols] = jnp.exp2(m_acc[h] - m_shift[h]) * acc_ref[s, :, cols] + pv[:, cols]
            return carry
        lax.fori_loop(0, jnp.minimum(nch, 1), body, 0)

    def first_full(i, carry):
        return carry[0], advance_max(carry[1], span_step(0, UNROLL, 0, None))

    def next_full(i, carry):
        m_acc, m_run = carry
        m_loc = span_step(i * UNROLL, UNROLL, UNROLL, m_run)
        pv_span((i - 1) * UNROLL, UNROLL, m_acc, m_run)
        return m_run, advance_max(m_run, m_loc)

    def half_after_full(i, carry):
        m_acc, m_run = carry
        m_loc = span_step(c_half, HALF, UNROLL, m_run)
        pv_span(c_half - UNROLL, UNROLL, m_acc, m_run)
        return m_run, advance_max(m_run, m_loc)

    def half_alone(i, carry):
        return carry[0], advance_max(carry[1], span_step(c_half, HALF, 0, None))

    def drain_full(i, carry):
        m_acc, m_run = carry
        span_step(0, 0, UNROLL, m_run)
        pv_span(c_half - UNROLL, UNROLL, m_acc, m_run)
        return m_run, m_run

    def drain_half(i, carry):
        m_acc, m_run = carry
        span_step(0, 0, HALF, m_run)
        pv_span(c_half, HALF, m_acc, m_run)
        return m_run, m_run

    acc_ref[...] = jnp.zeros(acc_ref.shape, F32)
    m_none = tuple(jnp.full((1, TQ), -jnp.inf, F32) for _ in range(ATT_HEADS))
    has_full = jnp.minimum(n_full, 1)
    carry = lax.fori_loop(0, has_full, first_full, (m_none, m_none))
    carry = lax.fori_loop(1, n_full, next_full, carry)
    carry = lax.fori_loop(0, has_half * has_full, half_after_full, carry)
    carry = lax.fori_loop(0, has_half * (1 - has_full), half_alone, carry)
    carry = lax.fori_loop(0, (1 - has_half) * has_full, drain_full, carry)
    lax.fori_loop(0, has_half, drain_half, carry)

    for s in range(n_pairs):
        o_t = jnp.concatenate(
            [acc_ref[s, :HEAD_DIM, e * TQ:(e + 1) * TQ] / acc_ref[s, HEAD_DIM:HEAD_DIM + 1, e * TQ:(e + 1) * TQ]
             for e in range(2)], axis=0)
        o_ref[0, :, s * LANES:(s + 1) * LANES] = o_t.T.astype(o_ref.dtype)


def _dsa_attention(qi, wit, q, ki2, kd, vt, topk):
    b, s, _ = q.shape
    lk = kd.shape[1]
    assert s % TQ == 0 and lk % (UNROLL * CH) == 0 and lk >= N_META + s and topk <= CH
    qblk = lambda bi, j: (bi, j, 0)
    kblk = lambda bi, j: (bi, 0, 0)
    return pl.pallas_call(
        functools.partial(_attn_kernel, topk),
        grid=(b, s // TQ),
        in_specs=[
            pl.BlockSpec((1, TQ, IDX_HEADS * IDX_DIM), qblk),
            pl.BlockSpec((1, IDX_HEADS, TQ), lambda bi, j: (bi, 0, j)),
            pl.BlockSpec((1, TQ, ATT_W), qblk),
            pl.BlockSpec((1, lk, LANES), kblk),
            pl.BlockSpec((1, lk, 2 * KV_W), kblk),
            pl.BlockSpec((1, ATT_KV_HEADS * VT_ROWS, lk), kblk),
        ],
        out_specs=pl.BlockSpec((1, TQ, ATT_W), qblk),
        out_shape=jax.ShapeDtypeStruct((b, s, ATT_W), BF16),
        scratch_shapes=[
            pltpu.VMEM((lk, TQ), I32),
            pltpu.VMEM((CH, TQ), I32),
            pltpu.VMEM((ATT_HEADS // 2, UNROLL * CH, 2 * TQ), F32),
            pltpu.VMEM((ATT_HEADS // 2, UNROLL * CH, 2 * TQ), BF16),
            pltpu.VMEM((IDX_HEADS // 2, LANES, 2 * TQ), BF16),
            pltpu.VMEM((ATT_HEADS // 2, LANES, 2 * TQ), BF16),
            pltpu.VMEM((SUBLANES, TQ), I32),
            pltpu.VMEM((ATT_HEADS // 2, VT_ROWS, 2 * TQ), F32),
        ],
        compiler_params=pltpu.CompilerParams(
            dimension_semantics=("arbitrary", "arbitrary"), vmem_limit_bytes=VMEM_LIMIT),
        name="dsa_attn",
    )(qi, wit, q, ki2, kd, vt)


def _merge_ln_kernel(att_ref, cu_ref, halo_ref, gb_ref, h_ref, cw_ref,
                     wg_ref, wa_ref, wc_ref, wo_ref, g_ref, b_ref, o_ref):
    tm = cu_ref.shape[1]
    hb = h_ref[0].astype(BF16)
    g_att = jnp.dot(hb, wg_ref[:, :D_MODEL], preferred_element_type=F32)
    g_conv = jnp.dot(hb, wg_ref[:, D_MODEL:], preferred_element_type=F32)
    cu = cu_ref[0]
    halo = halo_ref[0, 0]
    row = lax.broadcasted_iota(I32, (tm, CONV_W), 0)
    prev1 = jnp.where(row == 0, halo[7:8], pltpu.roll(cu, 1, 0))
    prev2 = jnp.where(row == 0, halo[6:7], jnp.where(row == 1, halo[7:8], pltpu.roll(cu, 2, 0)))
    cw = cw_ref[...]
    conv = cw[0:1] * prev2 + cw[1:2] * prev1 + cw[2:3] * cu
    y_conv = jnp.dot((gb_ref[0] * conv).astype(BF16), wc_ref[...], preferred_element_type=F32)
    y_att = jnp.dot(att_ref[0], wa_ref[...], preferred_element_type=F32)
    merged = jax.nn.sigmoid(g_att) * y_att + jax.nn.sigmoid(g_conv) * y_conv
    mix = jnp.dot(merged.astype(BF16), wo_ref[...], preferred_element_type=F32)
    o_ref[0] = _layer_norm(DEEPNORM_ALPHA * h_ref[0] + mix, g_ref[...], b_ref[...])


def _merge_ln(att, cu, halo, gb, h1, conv_w8, wg, wa, wc, wo, g, b):
    bsz, s, _ = cu.shape
    tm = s // halo.shape[1]
    blk = lambda bi, i: (bi, i, 0)
    const = lambda bi, i: (0, 0)
    return pl.pallas_call(
        _merge_ln_kernel,
        grid=(bsz, s // tm),
        in_specs=[
            pl.BlockSpec((1, tm, ATT_W), blk),
            pl.BlockSpec((1, tm, CONV_W), blk),
            pl.BlockSpec((1, 1, 8, CONV_W), lambda bi, i: (bi, i, 0, 0)),
            pl.BlockSpec((1, tm, CONV_W), blk),
            pl.BlockSpec((1, tm, D_MODEL), blk),
            pl.BlockSpec((8, CONV_W), const),
            pl.BlockSpec((D_MODEL, 2 * D_MODEL), const),
            pl.BlockSpec((ATT_W, D_MODEL), const),
            pl.BlockSpec((CONV_W, D_MODEL), const),
            pl.BlockSpec((D_MODEL, D_MODEL), const),
            pl.BlockSpec((1, D_MODEL), const),
            pl.BlockSpec((1, D_MODEL), const),
        ],
        out_specs=pl.BlockSpec((1, tm, D_MODEL), blk),
        out_shape=jax.ShapeDtypeStruct((bsz, s, D_MODEL), F32),
        compiler_params=pltpu.CompilerParams(
            dimension_semantics=("arbitrary", "arbitrary"), vmem_limit_bytes=VMEM_LIMIT),
        name="merge_ln",
    )(att, cu, halo, gb, h1, conv_w8, wg, wa, wc, wo, g, b)


def _rope_tables(pos):
    half = HEAD_DIM // 2
    inv_freq = ROPE_THETA ** (-jnp.arange(half, dtype=F32) / half)
    ang = pos.astype(F32)[:, None] * inv_freq[None, :]
    cos, sin = jnp.cos(ang), jnp.sin(ang)
    return jnp.tile(cos, (1, LANES // half)), jnp.tile(jnp.concatenate([-sin, sin], axis=1), (1, LANES // HEAD_DIM))


def _split_w_in(w_in):
    widths = (ATT_W, KV_W, KV_W, IDX_HEADS * IDX_DIM, IDX_DIM, IDX_HEADS,
              CONV_W, CONV_W, CONV_W, D_MODEL, D_MODEL)
    offs = np.cumsum(widths)[:-1].tolist()
    wq, wk, wv, wqi, wki, wwi, wu, wgb, wgc, wga, wgv = jnp.split(w_in, offs, axis=1)
    dup = lambda w: jnp.concatenate([w[:, :HEAD_DIM], w[:, :HEAD_DIM], w[:, HEAD_DIM:], w[:, HEAD_DIM:]], axis=1)
    w_a = jnp.concatenate(
        [wq, wqi, dup(wk), wki, wki, dup(wv), wwi, jnp.zeros((D_MODEL, LANES - IDX_HEADS), w_in.dtype)], axis=1)
    w_b = jnp.concatenate([wu, wgc, wgb], axis=1)
    w_g = jnp.concatenate([wga, wgv], axis=1)
    return w_a.astype(BF16), w_b.astype(BF16), w_g.astype(BF16)


def kernel(x, meta_tokens, ffn1_w_gate, ffn1_w_up, ffn1_w_down, ln1_g, ln1_b, w_in, conv_w, w_att_out, w_conv_out, w_o, ln2_g, ln2_b, ffn2_w_gate, ffn2_w_up, ffn2_w_down, ln3_g, ln3_b):
    bsz, seq, _ = x.shape
    l_tot = N_META + seq
    topk = min(TOPK_MAX, l_tot // 4)
    lk = -(-l_tot // (UNROLL * CH)) * (UNROLL * CH)
    lyr = 0
    bf = lambda w: w.astype(BF16)
    vec = lambda v: v[lyr][None, :]

    w1 = (bf(ffn1_w_gate[lyr]), bf(ffn1_w_up[lyr]), bf(ffn1_w_down[lyr]), vec(ln1_g), vec(ln1_b))
    w_a, w_b, w_g = _split_w_in(w_in[lyr])
    pos = jnp.arange(l_tot, dtype=I32)
    cos, sin = _rope_tables(pos)

    hm = _ffn_ln(meta_tokens.astype(F32), *w1)
    _, _, kd_m, ki_m, vd_m, _ = _proj_rope(hm, w_a, cos[:N_META], sin[:N_META], N_META)
    cu_m = _proj_gate(hm, w_b)[0]

    h1 = _ffn_ln(x.reshape(bsz * seq, D_MODEL), *w1)
    q, qi, kd, ki2, vd, wi = _proj_rope(h1, w_a, cos[N_META:], sin[N_META:], seq)
    cu, gb = _proj_gate(h1, w_b)

    def with_meta(meta_rows, real):
        wdt = real.shape[-1]
        return jnp.concatenate(
            [jnp.broadcast_to(meta_rows[None], (bsz, N_META, wdt)), real.reshape(bsz, seq, wdt),
             jnp.zeros((bsz, lk - l_tot, wdt), real.dtype)], axis=1)

    per_seq = lambda a: a.reshape(bsz, seq, a.shape[-1])
    v_keys = with_meta(vd_m, vd)
    ones_pad = jnp.zeros((bsz, lk, VT_ROWS - HEAD_DIM), BF16).at[:, :, 0].set(1.0)
    vt = jnp.swapaxes(jnp.concatenate(
        [v_keys[:, :, :HEAD_DIM], ones_pad, v_keys[:, :, 2 * HEAD_DIM:3 * HEAD_DIM], ones_pad], axis=2), 1, 2)
    wit = jnp.swapaxes(per_seq(wi)[:, :, :IDX_HEADS], 1, 2)
    att = _dsa_attention(per_seq(qi), wit, per_seq(q), with_meta(ki_m, ki2), with_meta(kd_m, kd), vt, topk)

    tm = min(seq, 512)
    cu3 = per_seq(cu)
    tails = cu3.reshape(bsz, seq // tm, tm, CONV_W)[:, :-1, tm - 8:, :]
    halo = jnp.concatenate([jnp.broadcast_to(cu_m[None, None, N_META - 8:], (bsz, 1, 8, CONV_W)), tails], axis=1)
    conv_w8 = jnp.concatenate([conv_w[lyr].astype(F32), jnp.zeros((8 - CONV_K, CONV_W), F32)], axis=0)
    h2 = _merge_ln(att, cu3, halo, per_seq(gb), per_seq(h1), conv_w8, w_g,
                   bf(w_att_out[lyr]), bf(w_conv_out[lyr]), bf(w_o[lyr]), vec(ln2_g), vec(ln2_b))

    h3 = _ffn_ln(h2.reshape(bsz * seq, D_MODEL), bf(ffn2_w_gate[lyr]), bf(ffn2_w_up[lyr]), bf(ffn2_w_down[lyr]),
                 vec(ln3_g), vec(ln3_b))
    return h3.reshape(bsz, seq, D_MODEL)
```

```python
import functools
import math

import numpy as np
import jax
import jax.numpy as jnp
from jax import lax
from jax.experimental import pallas as pl
from jax.experimental.pallas import tpu as pltpu

F32 = jnp.float32
BF16 = jnp.bfloat16
I32 = jnp.int32

D_MODEL = 1024
N_META = 16
ATT_HEADS = 8
ATT_KV_HEADS = 2
HEAD_DIM = 64
ATT_W = ATT_HEADS * HEAD_DIM
KV_W = ATT_KV_HEADS * HEAD_DIM
IDX_HEADS = 8
IDX_DIM = 64
TOPK_MAX = 256
ROPE_THETA = 10000.0
CONV_W = D_MODEL // 2
CONV_K = 3
D_FF = 2816
LN_EPS = 1e-5
DEPTH = 1
DEEPNORM_ALPHA = (2.0 * DEPTH) ** 0.25

LANES = 128
SUBLANES = 8
TQ = 256
CH = 256
VT_ROWS = 80
UNROLL = 4
HALF = UNROLL // 2
STEPS_PER_EXIT_TEST = 2
UNTESTED_STEPS = 14
INTERP_STEPS = 10
KEY_STEP_PERIOD = 4
MAX_SEARCH_STEPS = 160
INT_MAX = 2 ** 31 - 1
INT_MIN = -(2 ** 31)
LOG2E = 1.4426950408889634
NEG_BIG = -1e30
VMEM_LIMIT = 52 * 1024 * 1024


def _layer_norm(y, g, b):
    mu = jnp.mean(y, axis=-1, keepdims=True)
    yc = y - mu
    var = jnp.mean(yc * yc, axis=-1, keepdims=True)
    return yc * lax.rsqrt(var + LN_EPS) * g + b


def _ffn_ln_kernel(x_ref, wg_ref, wu_ref, wd_ref, g_ref, b_ref, o_ref, xb_ref):
    f = pl.program_id(1)

    @pl.when(f == 0)
    def _():
        o_ref[...] = jnp.zeros_like(o_ref)
        xb_ref[...] = x_ref[...].astype(BF16)

    xb = xb_ref[...]
    gt = jnp.dot(xb, wg_ref[...], preferred_element_type=F32)
    up = jnp.dot(xb, wu_ref[...], preferred_element_type=F32)
    a = (gt * jax.nn.sigmoid(gt)) * up
    o_ref[...] += jnp.dot(a.astype(BF16), wd_ref[...], preferred_element_type=F32)

    @pl.when(f == pl.num_programs(1) - 1)
    def _():
        tr = min(x_ref.shape[0], 256)

        def norm_rows(r, carry):
            rows = pl.ds(pl.multiple_of(r * tr, tr), tr)
            y = DEEPNORM_ALPHA * x_ref[rows, :] + 0.5 * o_ref[rows, :]
            o_ref[rows, :] = _layer_norm(y, g_ref[...], b_ref[...])
            return carry

        lax.fori_loop(0, x_ref.shape[0] // tr, norm_rows, 0)


def _ffn_ln(x2d, wg, wu, wd, g, b):
    n = x2d.shape[0]
    tm = min(n, 2048)
    tf = 256 if tm > 256 else D_FF
    assert n % tm == 0 and D_FF % tf == 0
    return pl.pallas_call(
        _ffn_ln_kernel,
        grid=(n // tm, D_FF // tf),
        in_specs=[
            pl.BlockSpec((tm, D_MODEL), lambda i, f: (i, 0)),
            pl.BlockSpec((D_MODEL, tf), lambda i, f: (0, f)),
            pl.BlockSpec((D_MODEL, tf), lambda i, f: (0, f)),
            pl.BlockSpec((tf, D_MODEL), lambda i, f: (f, 0)),
            pl.BlockSpec((1, D_MODEL), lambda i, f: (0, 0)),
            pl.BlockSpec((1, D_MODEL), lambda i, f: (0, 0)),
        ],
        out_specs=pl.BlockSpec((tm, D_MODEL), lambda i, f: (i, 0)),
        out_shape=jax.ShapeDtypeStruct((n, D_MODEL), F32),
        scratch_shapes=[pltpu.VMEM((tm, D_MODEL), BF16)],
        compiler_params=pltpu.CompilerParams(
            dimension_semantics=("arbitrary", "arbitrary"), vmem_limit_bytes=VMEM_LIMIT),
        name="ffn_ln",
    )(x2d, wg, wu, wd, g, b)


_PROJ_A_W = ATT_W + IDX_HEADS * IDX_DIM + 2 * KV_W + LANES + 2 * KV_W + LANES


def _proj_rope_kernel(x_ref, w_ref, cos_ref, sin_ref, q_ref, qi_ref, kd_ref, ki_ref, vd_ref, wi_ref):
    tm = x_ref.shape[0]
    xb = x_ref[...].astype(BF16)
    cos = cos_ref[...]
    sin = sin_ref[...]
    lane = lax.broadcasted_iota(I32, (tm, LANES), 1)
    first_half = (lane & (HEAD_DIM // 2)) == 0

    def rope(s):
        partner = jnp.where(first_half, pltpu.roll(s, LANES - HEAD_DIM // 2, 1),
                            pltpu.roll(s, HEAD_DIM // 2, 1))
        return s * cos + partner * sin

    def dot_cols(c0, width):
        return jnp.dot(xb, w_ref[:, c0:c0 + width], preferred_element_type=F32)

    outs = ((q_ref, ATT_W, HEAD_DIM ** -0.5 * LOG2E), (qi_ref, IDX_HEADS * IDX_DIM, 1.0), (kd_ref, 2 * KV_W, 1.0))
    c0 = 0
    for ref, width, scale in outs:
        for j in range(0, width, 2 * LANES):
            p = dot_cols(c0 + j, 2 * LANES)
            for s in range(2):
                r = rope(p[:, s * LANES:(s + 1) * LANES])
                if scale != 1.0:
                    r = r * scale
                ref[:, j + s * LANES:j + (s + 1) * LANES] = r.astype(ref.dtype)
        c0 += width
    p = dot_cols(c0, LANES)
    ki_ref[...] = rope(p).astype(ki_ref.dtype)
    c0 += LANES
    vd_ref[...] = dot_cols(c0, 2 * KV_W).astype(vd_ref.dtype)
    c0 += 2 * KV_W
    wi_ref[...] = dot_cols(c0, LANES) * ((IDX_HEADS * IDX_DIM) ** -0.5)


def _proj_rope(h2d, w_a, cos, sin, rows_per_seq):
    n = h2d.shape[0]
    tm = min(rows_per_seq, 512)
    tiles_per_seq = rows_per_seq // tm
    assert n % tm == 0 and rows_per_seq % tm == 0
    row = lambda i: (i, 0)
    tab = lambda i: (i % tiles_per_seq, 0)
    widths = (ATT_W, IDX_HEADS * IDX_DIM, 2 * KV_W, LANES, 2 * KV_W, LANES)
    dtypes = (BF16, BF16, BF16, BF16, BF16, F32)
    return pl.pallas_call(
        _proj_rope_kernel,
        grid=(n // tm,),
        in_specs=[
            pl.BlockSpec((tm, D_MODEL), row),
            pl.BlockSpec((D_MODEL, _PROJ_A_W), lambda i: (0, 0)),
            pl.BlockSpec((tm, LANES), tab),
            pl.BlockSpec((tm, LANES), tab),
        ],
        out_specs=[pl.BlockSpec((tm, w), row) for w in widths],
        out_shape=[jax.ShapeDtypeStruct((n, w), dt) for w, dt in zip(widths, dtypes)],
        compiler_params=pltpu.CompilerParams(
            dimension_semantics=("arbitrary",), vmem_limit_bytes=VMEM_LIMIT),
        name="proj_rope",
    )(h2d, w_a, cos, sin)


_PROJ_B_W = 3 * CONV_W


def _proj_gate_kernel(x_ref, w_ref, cu_ref, gb_ref):
    xb = x_ref[...].astype(BF16)

    def dot_cols(c0, width):
        return jnp.dot(xb, w_ref[:, c0:c0 + width], preferred_element_type=F32)

    cu_ref[...] = dot_cols(0, CONV_W) * dot_cols(CONV_W, CONV_W)
    gb_ref[...] = dot_cols(2 * CONV_W, CONV_W)


def _proj_gate(h2d, w_b):
    n = h2d.shape[0]
    tm = min(n, 512)
    assert n % tm == 0
    row = lambda i: (i, 0)
    widths = (CONV_W, CONV_W)
    return pl.pallas_call(
        _proj_gate_kernel,
        grid=(n // tm,),
        in_specs=[pl.BlockSpec((tm, D_MODEL), row), pl.BlockSpec((D_MODEL, _PROJ_B_W), lambda i: (0, 0))],
        out_specs=[pl.BlockSpec((tm, w), row) for w in widths],
        out_shape=[jax.ShapeDtypeStruct((n, w), F32) for w in widths],
        compiler_params=pltpu.CompilerParams(
            dimension_semantics=("arbitrary",), vmem_limit_bytes=VMEM_LIMIT),
        name="proj_gate",
    )(h2d, w_b)


def _fold_rows(x, op):
    parts = [x[i:i + SUBLANES] for i in range(0, x.shape[0], SUBLANES)]
    while len(parts) > 1:
        parts = [op(parts[i], parts[i + 1]) for i in range(0, len(parts), 2)]
    return parts[0]


def _key_to_val(key):
    return pltpu.bitcast(jnp.where(key < 0, key ^ 0x7FFFFFFF, key), F32)


def _val_to_key(val):
    bits = pltpu.bitcast(jnp.where(val == 0.0, 0.0, val), I32)
    return jnp.where(bits < 0, bits ^ 0x7FFFFFFF, bits)


def _chunk_loop(nch, body, init):
    n_main = nch // UNROLL

    def main(i, carry):
        for u in range(UNROLL):
            carry = body(i * UNROLL + u, carry)
        return carry

    carry = lax.fori_loop(0, n_main, main, init)
    return lax.fori_loop(n_main * UNROLL, nch, body, carry)


def _attn_kernel(topk, qi_ref, wit_ref, q_ref, ki_ref, kd_ref, vt_ref, o_ref,
                 keys_ref, kmax_ref, stage_ref, p_ref, qit_ref, qt_ref, acc_ref):
    jblk = pl.program_id(1)
    qpos0 = N_META + jblk * TQ
    nch = (qpos0 + TQ - 1) // CH + 1
    kf = float(topk)
    n_pairs = ATT_HEADS // 2

    lane = lax.broadcasted_iota(I32, (TQ, LANES), 1)
    lo_half = lane < HEAD_DIM
    qpos_row = qpos0 + lax.broadcasted_iota(I32, (1, TQ), 1)
    searchable = (qpos_row + 1).astype(F32) > kf

    for s in range(n_pairs):
        for src, dst in ((qi_ref, qit_ref), (q_ref, qt_ref)):
            slab = src[0, :, s * LANES:(s + 1) * LANES].astype(F32)
            even = jnp.where(lo_half, slab, 0.0).T
            odd = jnp.where(lo_half, 0.0, slab).T
            dst[s] = jnp.concatenate([even, odd], axis=1).astype(BF16)

    kpos_iota = lax.broadcasted_iota(I32, (CH, TQ), 0)
    qpos = qpos0 + lax.broadcasted_iota(I32, (CH, TQ), 1)

    def score_chunk(c, carry):
        off = pl.multiple_of(c * CH, CH)
        kic = ki_ref[0, pl.ds(off, CH), :]
        acc = jnp.zeros((CH, TQ), F32)
        for s in range(IDX_HEADS // 2):
            d = jnp.dot(kic, qit_ref[s], preferred_element_type=F32)
            for e in range(2):
                w = wit_ref[0, 2 * s + e:2 * s + e + 1, :]
                acc = acc + jnp.maximum(d[:, e * TQ:(e + 1) * TQ], 0.0) * w
        key = jnp.where(off + kpos_iota <= qpos, _val_to_key(acc), INT_MIN)
        keys_ref[pl.ds(off, CH), :] = key
        kmax_ref[...] = jnp.maximum(kmax_ref[...], key)
        return carry

    n_full = (nch + 1) // UNROLL
    has_half = ((nch + 1) % UNROLL) // HALF
    c_half = n_full * UNROLL

    def score_span(c0, n_sub):
        for u in range(n_sub):
            score_chunk(c0 + u, 0)
        return 0

    kmax_ref[...] = jnp.full(kmax_ref.shape, INT_MIN, I32)
    lax.fori_loop(0, n_full, lambda i, c: score_span(i * UNROLL, UNROLL), 0)
    lax.fori_loop(0, has_half, lambda i, c: score_span(c_half, HALF), 0)

    def count(pred):
        def body(c, acc):
            off = pl.multiple_of(c * CH, CH)
            hit = jnp.where(pred(keys_ref[pl.ds(off, CH), :], off + kpos_iota), 1.0, 0.0)
            return acc + _fold_rows(hit, jnp.add)
        acc = _chunk_loop(nch, body, jnp.zeros((SUBLANES, TQ), F32))
        return jnp.sum(acc, axis=0, keepdims=True)

    kmax = kmax_ref[...]
    lo0 = jnp.min(_fold_rows(kmax, jnp.minimum), axis=0, keepdims=True)
    hi0 = jnp.minimum(jnp.max(_fold_rows(kmax, jnp.maximum), axis=0, keepdims=True), INT_MAX - 1) + 1

    def live(lo, hi, c_lo):
        return searchable & (c_lo != kf) & (hi - lo != 1)

    def n_live(lo, hi, c_lo):
        return jnp.max(jnp.where(live(lo, hi, c_lo), 1.0, 0.0))

    def pick(it, lo, hi, c_lo, c_hi):
        top = hi - 1
        v_lo, v_top = _key_to_val(lo), _key_to_val(top)
        log_lo = jnp.log(c_lo)
        frac = (log_lo - math.log(kf)) / (log_lo - jnp.log(jnp.maximum(c_hi, 0.5)))
        interpolate = (it >= 1) & (it <= INTERP_STEPS) & (c_lo < 1e8)
        frac = jnp.where(interpolate, jnp.minimum(jnp.maximum(frac, 0.1), 0.9), 0.5)
        mid_val = _val_to_key(v_lo + (v_top - v_lo) * frac)
        mid_key = lo + lax.shift_right_logical(hi - lo, 1)
        key_step = (it > INTERP_STEPS) & (it % KEY_STEP_PERIOD == KEY_STEP_PERIOD - 1)
        cand = jnp.where(key_step, mid_key, mid_val)
        cand = jnp.where((lo < 0) & (top >= 0), 0, cand)
        cand = jnp.where((lo == 0) & (top >= 1), 1, cand)
        cand = jnp.minimum(jnp.maximum(cand, lo + 1), top)
        return jnp.where(live(lo, hi, c_lo), cand, lo)

    def search_steps(st, n_steps):
        it, lo, hi, c_lo, c_hi = st
        for _ in range(n_steps):
            cand = pick(it, lo, hi, c_lo, c_hi)
            cnt = count(lambda k, pos: k >= cand)
            ok = cnt >= kf
            lo = jnp.where(ok, cand, lo)
            c_lo = jnp.where(ok, cnt, c_lo)
            hi = jnp.where(ok, hi, cand)
            c_hi = jnp.where(ok, c_hi, cnt)
            it = it + 1
        return it, lo, hi, c_lo, c_hi

    def bis_cond(st):
        return (st[0] < MAX_SEARCH_STEPS) & (st[5] > 0.0)

    def bis_body(st):
        st = search_steps(st[:5], STEPS_PER_EXIT_TEST)
        return st + (n_live(st[1], st[2], st[3]),)

    c_unknown = jnp.full((1, TQ), 1e9, F32)
    st = (jnp.int32(0), lo0, hi0, c_unknown, jnp.zeros((1, TQ), F32))
    st = lax.fori_loop(0, UNTESTED_STEPS // STEPS_PER_EXIT_TEST,
                       lambda _, st: search_steps(st, STEPS_PER_EXIT_TEST), st)
    _, t_fin, _, ct_fin, c_above, _ = lax.while_loop(bis_cond, bis_body, st + (n_live(st[1], st[2], st[3]),))
    t_fin = jnp.where(searchable, t_fin, INT_MIN + 1)
    open_fin = jnp.max(jnp.where(searchable & (ct_fin != kf), 1.0, 0.0))

    @pl.when(open_fin > 0.0)
    def _():
        tie_q = searchable & (ct_fin != kf)
        need = kf - c_above
        tri = jnp.where(lax.broadcasted_iota(I32, (CH, CH), 0) >= lax.broadcasted_iota(I32, (CH, CH), 1),
                        1.0, 0.0).astype(BF16)

        def body(c, seen):
            off = pl.multiple_of(c * CH, CH)
            k = keys_ref[pl.ds(off, CH), :]
            is_tie = k == t_fin
            rank = jnp.dot(tri, jnp.where(is_tie, 1.0, 0.0).astype(BF16), preferred_element_type=F32) + seen
            keys_ref[pl.ds(off, CH), :] = jnp.where(tie_q & is_tie & (rank > need), t_fin - 1, k)
            return rank[CH - 1:CH, :]

        _chunk_loop(nch, body, jnp.zeros((1, TQ), F32))

    def logits(c, s):
        g = (2 * s) // (ATT_HEADS // ATT_KV_HEADS)
        kg = kd_ref[0, pl.ds(pl.multiple_of(c * CH, CH), CH), g * LANES:(g + 1) * LANES]
        return jnp.dot(kg, qt_ref[s], preferred_element_type=F32)

    def span_step(c_new, n_new, n_old, m_old):
        m_loc = [jnp.full((SUBLANES, TQ), -jnp.inf, F32) for _ in range(ATT_HEADS)]
        for u in range(max(n_new, n_old)):
            rows = slice(u * CH, (u + 1) * CH)
            if u < n_new:
                off_new = pl.multiple_of((c_new + u) * CH, CH)
                k = keys_ref[pl.ds(off_new, CH), :]
                bias = jnp.where(k >= t_fin, 0.0, NEG_BIG)
            for s in range(n_pairs):
                if u < n_old:
                    ps = [jnp.exp2(stage_ref[s, rows, e * TQ:(e + 1) * TQ] - m_old[2 * s + e]).astype(BF16)
                          for e in range(2)]
                    p_ref[s, rows, :] = jnp.concatenate(ps, axis=1)
                if u < n_new:
                    lt2 = logits(c_new + u, s)
                    for e in range(2):
                        h = 2 * s + e
                        lt = lt2[:, e * TQ:(e + 1) * TQ] + bias
                        stage_ref[s, rows, e * TQ:(e + 1) * TQ] = lt
                        m_loc[h] = jnp.maximum(m_loc[h], _fold_rows(lt, jnp.maximum))
        return m_loc

    def advance_max(m_run, m_loc):
        return tuple(jnp.maximum(m_run[h], jnp.max(m_loc[h], axis=0, keepdims=True)) for h in range(ATT_HEADS))

    def pv_span(c0, n_sub, m_acc, m_shift):
        def body(_, carry):
            off = pl.multiple_of(c0 * CH, HALF * CH)
            for s in range(n_pairs):
                g = (2 * s) // (ATT_HEADS // ATT_KV_HEADS)
                vgt = vt_ref[0, g * VT_ROWS:(g + 1) * VT_ROWS, pl.ds(off, n_sub * CH)]
                pv = jnp.dot(vgt, p_ref[s, :n_sub * CH, :], preferred_element_type=F32)
                for e in range(2):
                    h = 2 * s + e
                    cols = slice(e * TQ, (e + 1) * TQ)
                    acc_ref[s, :, cols] = jnp.exp2(m_acc[h] - m_shift[h]) * acc_ref[s, :, cols] + pv[:, cols]
            return carry
        lax.fori_loop(0, jnp.minimum(nch, 1), body, 0)

    def first_full(i, carry):
        return carry[0], advance_max(carry[1], span_step(0, UNROLL, 0, None))

    def next_full(i, carry):
        m_acc, m_run = carry
        m_loc = span_step(i * UNROLL, UNROLL, UNROLL, m_run)
        pv_span((i - 1) * UNROLL, UNROLL, m_acc, m_run)
        return m_run, advance_max(m_run, m_loc)

    def half_after_full(i, carry):
        m_acc, m_run = carry
        m_loc = span_step(c_half, HALF, UNROLL, m_run)
        pv_span(c_half - UNROLL, UNROLL, m_acc, m_run)
        return m_run, advance_max(m_run, m_loc)

    def half_alone(i, carry):
        return carry[0], advance_max(carry[1], span_step(c_half, HALF, 0, None))

    def drain_full(i, carry):
        m_acc, m_run = carry
        span_step(0, 0, UNROLL, m_run)
        pv_span(c_half - UNROLL, UNROLL, m_acc, m_run)
        return m_run, m_run

    def drain_half(i, carry):
        m_acc, m_run = carry
        span_step(0, 0, HALF, m_run)
        pv_span(c_half, HALF, m_acc, m_run)
        return m_run, m_run

    acc_ref[...] = jnp.zeros(acc_ref.shape, F32)
    m_none = tuple(jnp.full((1, TQ), -jnp.inf, F32) for _ in range(ATT_HEADS))
    has_full = jnp.minimum(n_full, 1)
    carry = lax.fori_loop(0, has_full, first_full, (m_none, m_none))
    carry = lax.fori_loop(1, n_full, next_full, carry)
    carry = lax.fori_loop(0, has_half * has_full, half_after_full, carry)
    carry = lax.fori_loop(0, has_half * (1 - has_full), half_alone, carry)
    carry = lax.fori_loop(0, (1 - has_half) * has_full, drain_full, carry)
    lax.fori_loop(0, has_half, drain_half, carry)

    for s in range(n_pairs):
        o_t = jnp.concatenate(
            [acc_ref[s, :HEAD_DIM, e * TQ:(e + 1) * TQ] / acc_ref[s, HEAD_DIM:HEAD_DIM + 1, e * TQ:(e + 1) * TQ]
             for e in range(2)], axis=0)
        o_ref[0, :, s * LANES:(s + 1) * LANES] = o_t.T.astype(o_ref.dtype)


def _dsa_attention(qi, wit, q, ki2, kd, vt, topk):
    b, s, _ = q.shape
    lk = kd.shape[1]
    assert s % TQ == 0 and lk % (UNROLL * CH) == 0 and lk >= N_META + s and topk <= CH
    qblk = lambda bi, j: (bi, j, 0)
    kblk = lambda bi, j: (bi, 0, 0)
    return pl.pallas_call(
        functools.partial(_attn_kernel, topk),
        grid=(b, s // TQ),
        in_specs=[
            pl.BlockSpec((1, TQ, IDX_HEADS * IDX_DIM), qblk),
            pl.BlockSpec((1, IDX_HEADS, TQ), lambda bi, j: (bi, 0, j)),
            pl.BlockSpec((1, TQ, ATT_W), qblk),
            pl.BlockSpec((1, lk, LANES), kblk),
            pl.BlockSpec((1, lk, 2 * KV_W), kblk),
            pl.BlockSpec((1, ATT_KV_HEADS * VT_ROWS, lk), kblk),
        ],
        out_specs=pl.BlockSpec((1, TQ, ATT_W), qblk),
        out_shape=jax.ShapeDtypeStruct((b, s, ATT_W), BF16),
        scratch_shapes=[
            pltpu.VMEM((lk, TQ), I32),
            pltpu.VMEM((CH, TQ), I32),
            pltpu.VMEM((ATT_HEADS // 2, UNROLL * CH, 2 * TQ), F32),
            pltpu.VMEM((ATT_HEADS // 2, UNROLL * CH, 2 * TQ), BF16),
            pltpu.VMEM((IDX_HEADS // 2, LANES, 2 * TQ), BF16),
            pltpu.VMEM((ATT_HEADS // 2, LANES, 2 * TQ), BF16),
            pltpu.VMEM((ATT_HEADS // 2, VT_ROWS, 2 * TQ), F32),
        ],
        compiler_params=pltpu.CompilerParams(
            dimension_semantics=("arbitrary", "arbitrary"), vmem_limit_bytes=VMEM_LIMIT),
        name="dsa_attn",
    )(qi, wit, q, ki2, kd, vt)


def _merge_ln_kernel(att_ref, cu_ref, halo_ref, gb_ref, h_ref, cw_ref,
                     wg_ref, wa_ref, wc_ref, wo_ref, g_ref, b_ref, o_ref):
    tm = cu_ref.shape[1]
    hb = h_ref[0].astype(BF16)
    g_att = jnp.dot(hb, wg_ref[:, :D_MODEL], preferred_element_type=F32)
    g_conv = jnp.dot(hb, wg_ref[:, D_MODEL:], preferred_element_type=F32)
    cu = cu_ref[0]
    halo = halo_ref[0, 0]
    row = lax.broadcasted_iota(I32, (tm, CONV_W), 0)
    prev1 = jnp.where(row == 0, halo[7:8], pltpu.roll(cu, 1, 0))
    prev2 = jnp.where(row == 0, halo[6:7], jnp.where(row == 1, halo[7:8], pltpu.roll(cu, 2, 0)))
    cw = cw_ref[...]
    conv = cw[0:1] * prev2 + cw[1:2] * prev1 + cw[2:3] * cu
    y_conv = jnp.dot((gb_ref[0] * conv).astype(BF16), wc_ref[...], preferred_element_type=F32)
    y_att = jnp.dot(att_ref[0], wa_ref[...], preferred_element_type=F32)
    merged = jax.nn.sigmoid(g_att) * y_att + jax.nn.sigmoid(g_conv) * y_conv
    mix = jnp.dot(merged.astype(BF16), wo_ref[...], preferred_element_type=F32)
    o_ref[0] = _layer_norm(DEEPNORM_ALPHA * h_ref[0] + mix, g_ref[...], b_ref[...])


def _merge_ln(att, cu, halo, gb, h1, conv_w8, wg, wa, wc, wo, g, b):
    bsz, s, _ = cu.shape
    tm = s // halo.shape[1]
    blk = lambda bi, i: (bi, i, 0)
    const = lambda bi, i: (0, 0)
    return pl.pallas_call(
        _merge_ln_kernel,
        grid=(bsz, s // tm),
        in_specs=[
            pl.BlockSpec((1, tm, ATT_W), blk),
            pl.BlockSpec((1, tm, CONV_W), blk),
            pl.BlockSpec((1, 1, 8, CONV_W), lambda bi, i: (bi, i, 0, 0)),
            pl.BlockSpec((1, tm, CONV_W), blk),
            pl.BlockSpec((1, tm, D_MODEL), blk),
            pl.BlockSpec((8, CONV_W), const),
            pl.BlockSpec((D_MODEL, 2 * D_MODEL), const),
            pl.BlockSpec((ATT_W, D_MODEL), const),
            pl.BlockSpec((CONV_W, D_MODEL), const),
            pl.BlockSpec((D_MODEL, D_MODEL), const),
            pl.BlockSpec((1, D_MODEL), const),
            pl.BlockSpec((1, D_MODEL), const),
        ],
        out_specs=pl.BlockSpec((1, tm, D_MODEL), blk),
        out_shape=jax.ShapeDtypeStruct((bsz, s, D_MODEL), F32),
        compiler_params=pltpu.CompilerParams(
            dimension_semantics=("arbitrary", "arbitrary"), vmem_limit_bytes=VMEM_LIMIT),
        name="merge_ln",
    )(att, cu, halo, gb, h1, conv_w8, wg, wa, wc, wo, g, b)


def _rope_tables(pos):
    half = HEAD_DIM // 2
    inv_freq = ROPE_THETA ** (-jnp.arange(half, dtype=F32) / half)
    ang = pos.astype(F32)[:, None] * inv_freq[None, :]
    cos, sin = jnp.cos(ang), jnp.sin(ang)
    return jnp.tile(cos, (1, LANES // half)), jnp.tile(jnp.concatenate([-sin, sin], axis=1), (1, LANES // HEAD_DIM))


def _split_w_in(w_in):
    widths = (ATT_W, KV_W, KV_W, IDX_HEADS * IDX_DIM, IDX_DIM, IDX_HEADS,
              CONV_W, CONV_W, CONV_W, D_MODEL, D_MODEL)
    offs = np.cumsum(widths)[:-1].tolist()
    wq, wk, wv, wqi, wki, wwi, wu, wgb, wgc, wga, wgv = jnp.split(w_in, offs, axis=1)
    dup = lambda w: jnp.concatenate([w[:, :HEAD_DIM], w[:, :HEAD_DIM], w[:, HEAD_DIM:], w[:, HEAD_DIM:]], axis=1)
    w_a = jnp.concatenate(
        [wq, wqi, dup(wk), wki, wki, dup(wv), wwi, jnp.zeros((D_MODEL, LANES - IDX_HEADS), w_in.dtype)], axis=1)
    w_b = jnp.concatenate([wu, wgc, wgb], axis=1)
    w_g = jnp.concatenate([wga, wgv], axis=1)
    return w_a.astype(BF16), w_b.astype(BF16), w_g.astype(BF16)


def kernel(x, meta_tokens, ffn1_w_gate, ffn1_w_up, ffn1_w_down, ln1_g, ln1_b, w_in, conv_w, w_att_out, w_conv_out, w_o, ln2_g, ln2_b, ffn2_w_gate, ffn2_w_up, ffn2_w_down, ln3_g, ln3_b):
    bsz, seq, _ = x.shape
    l_tot = N_META + seq
    topk = min(TOPK_MAX, l_tot // 4)
    lk = -(-l_tot // (UNROLL * CH)) * (UNROLL * CH)
    lyr = 0
    bf = lambda w: w.astype(BF16)
    vec = lambda v: v[lyr][None, :]

    w1 = (bf(ffn1_w_gate[lyr]), bf(ffn1_w_up[lyr]), bf(ffn1_w_down[lyr]), vec(ln1_g), vec(ln1_b))
    w_a, w_b, w_g = _split_w_in(w_in[lyr])
    pos = jnp.arange(l_tot, dtype=I32)
    cos, sin = _rope_tables(pos)

    hm = _ffn_ln(meta_tokens.astype(F32), *w1)
    _, _, kd_m, ki_m, vd_m, _ = _proj_rope(hm, w_a, cos[:N_META], sin[:N_META], N_META)
    cu_m = _proj_gate(hm, w_b)[0]

    h1 = _ffn_ln(x.reshape(bsz * seq, D_MODEL), *w1)
    q, qi, kd, ki2, vd, wi = _proj_rope(h1, w_a, cos[N_META:], sin[N_META:], seq)
    cu, gb = _proj_gate(h1, w_b)

    def with_meta(meta_rows, real):
        wdt = real.shape[-1]
        return jnp.concatenate(
            [jnp.broadcast_to(meta_rows[None], (bsz, N_META, wdt)), real.reshape(bsz, seq, wdt),
             jnp.zeros((bsz, lk - l_tot, wdt), real.dtype)], axis=1)

    per_seq = lambda a: a.reshape(bsz, seq, a.shape[-1])
    v_keys = with_meta(vd_m, vd)
    ones_pad = jnp.zeros((bsz, lk, VT_ROWS - HEAD_DIM), BF16).at[:, :, 0].set(1.0)
    vt = jnp.swapaxes(jnp.concatenate(
        [v_keys[:, :, :HEAD_DIM], ones_pad, v_keys[:, :, 2 * HEAD_DIM:3 * HEAD_DIM], ones_pad], axis=2), 1, 2)
    wit = jnp.swapaxes(per_seq(wi)[:, :, :IDX_HEADS], 1, 2)
    att = _dsa_attention(per_seq(qi), wit, per_seq(q), with_meta(ki_m, ki2), with_meta(kd_m, kd), vt, topk)

    tm = min(seq, 512)
    cu3 = per_seq(cu)
    tails = cu3.reshape(bsz, seq // tm, tm, CONV_W)[:, :-1, tm - 8:, :]
    halo = jnp.concatenate([jnp.broadcast_to(cu_m[None, None, N_META - 8:], (bsz, 1, 8, CONV_W)), tails], axis=1)
    conv_w8 = jnp.concatenate([conv_w[lyr].astype(F32), jnp.zeros((8 - CONV_K, CONV_W), F32)], axis=0)
    h2 = _merge_ln(att, cu3, halo, per_seq(gb), per_seq(h1), conv_w8, w_g,
                   bf(w_att_out[lyr]), bf(w_conv_out[lyr]), bf(w_o[lyr]), vec(ln2_g), vec(ln2_b))

    h3 = _ffn_ln(h2.reshape(bsz * seq, D_MODEL), bf(ffn2_w_gate[lyr]), bf(ffn2_w_up[lyr]), bf(ffn2_w_down[lyr]),
                 vec(ln3_g), vec(ln3_b))
    return h3.reshape(bsz, seq, D_MODEL)
```
